```python
import math
import jax, jax.numpy as jnp
from jax import lax
import numpy as np

D_MODEL = 2048
BATCH = 4
SEQ = 4096
DEPTH = 2

GRID_W = 64
CTX_LEN = 256

ATT_HEADS = 8
ATT_KV_HEADS = 2
HEAD_DIM = 128
WINDOW = 128
ATT_BLOCK = 128
ROPE_BASE = 10000.0
SSM_WIDTH = 512
SSM_GROUP = 16
SSM_GROUPS = SSM_WIDTH // SSM_GROUP
SSM_STATE = 64
GLA_HEADS = 4
GLA_DK = 64
GLA_DV = 128
GLA_RANK = 16
GLA_TAU = 16.0
GLA_CHUNK = 64
N_EXPERTS = 16
EXPERT_FF = 2048
CAPACITY_FACTOR = 2
N_MOD = 6
ALPHA = (2 * DEPTH) ** 0.25
BETA = (8 * DEPTH) ** -0.25
LN_EPS = 1e-6
NEG_INF = -1e30

ATT_Q_W = ATT_HEADS * HEAD_DIM
ATT_KV_W = ATT_KV_HEADS * HEAD_DIM
GLA_QK_W = GLA_HEADS * GLA_DK
GLA_V_W = GLA_HEADS * GLA_DV
IN_SIZES = (ATT_Q_W, ATT_KV_W, ATT_KV_W, SSM_WIDTH, GLA_QK_W, GLA_QK_W, GLA_V_W, GLA_V_W, 2 * GLA_RANK)
IN_SPLITS = tuple(int(s) for s in np.cumsum(IN_SIZES)[:-1])
N_IN = sum(IN_SIZES)
MIX_WIDTH = ATT_Q_W + SSM_WIDTH + GLA_V_W

kernel_name = 'hybrid_s5_swa_gla_ec_moe_diffusion_trunk'


def layer_norm(x, g=None, b=None):
    xf = x.astype(jnp.float32)
    mu = jnp.mean(xf, axis=-1, keepdims=True)
    var = jnp.mean(jnp.square(xf - mu), axis=-1, keepdims=True)
    y = (xf - mu) * lax.rsqrt(var + LN_EPS)
    if g is not None:
        y = y * g + b
    return y.astype(x.dtype)


def rope_rotate(x, ang):
    m = x.shape[-1] // 2
    x1, x2 = x[..., :m], x[..., m:]
    cos = jnp.cos(ang)[:, None, :]
    sin = jnp.sin(ang)[:, None, :]
    return jnp.concatenate([x1 * cos - x2 * sin, x2 * cos + x1 * sin], axis=-1)


def axial_rope(x, rows, cols):
    half = x.shape[-1] // 2
    nf = half // 2
    inv = ROPE_BASE ** (-jnp.arange(nf, dtype=jnp.float32) / nf)
    ang_r = rows.astype(jnp.float32)[:, None] * inv
    ang_c = cols.astype(jnp.float32)[:, None] * inv
    xf = x.astype(jnp.float32)
    out = jnp.concatenate([rope_rotate(xf[..., :half], ang_r), rope_rotate(xf[..., half:], ang_c)], axis=-1)
    return out.astype(x.dtype)


def window_attention(q, k, v, kc, vc, sink):
    Bt, L, Hq, Dh = q.shape
    Hkv = k.shape[2]
    G = Hq // Hkv
    W = ATT_BLOCK
    nb = L // W
    Lc = kc.shape[1]
    qb = (q * (Dh ** -0.5)).reshape(Bt, nb, W, Hkv, G, Dh)

    def band(t):
        tp = jnp.pad(t, ((0, 0), (W, W), (0, 0), (0, 0))).reshape(Bt, nb + 2, W, Hkv, Dh)
        return jnp.concatenate([tp[:, :-2], tp[:, 1:-1], tp[:, 2:]], axis=2)

    kw, vw = band(k), band(v)
    s_loc = jnp.einsum('bnqhgd,bnkhd->bnhgqk', qb, kw).astype(jnp.float32)
    qi = jnp.arange(W)
    kj = jnp.arange(3 * W)
    rel = qi[:, None] + W - kj[None, :]
    key_pos = jnp.arange(nb)[:, None] * W - W + kj[None, :]
    valid = (jnp.abs(rel) <= WINDOW)[None] & ((key_pos >= 0) & (key_pos < L))[:, None, :]
    s_loc = jnp.where(valid[None, :, None, None], s_loc, NEG_INF)
    s_ctx = jnp.einsum('bnqhgd,bchd->bnhgqc', qb, kc).astype(jnp.float32)
    s_sink = jnp.broadcast_to(sink.astype(jnp.float32).reshape(Hkv, G)[None, None, :, :, None, None],
                              s_ctx.shape[:-1] + (1,))
    p = jax.nn.softmax(jnp.concatenate([s_loc, s_ctx, s_sink], axis=-1), axis=-1)
    o = (jnp.einsum('bnhgqk,bnkhd->bnqhgd', p[..., :3 * W].astype(vw.dtype), vw)
         + jnp.einsum('bnhgqc,bchd->bnqhgd', p[..., 3 * W:3 * W + Lc].astype(vc.dtype), vc))
    return o.reshape(Bt, L, Hq * Dh)


def ctx_attention(qc, kc, vc, sink):
    Bt, Lc, Hq, Dh = qc.shape
    Hkv = kc.shape[2]
    G = Hq // Hkv
    qg = (qc * (Dh ** -0.5)).reshape(Bt, Lc, Hkv, G, Dh)
    s = jnp.einsum('bqhgd,bkhd->bhgqk', qg, kc).astype(jnp.float32)
    s_sink = jnp.broadcast_to(sink.astype(jnp.float32).reshape(Hkv, G)[None, :, :, None, None], s.shape[:-1] + (1,))
    p = jax.nn.softmax(jnp.concatenate([s, s_sink], axis=-1), axis=-1)[..., :Lc]
    o = jnp.einsum('bhgqk,bkhd->bqhgd', p.astype(vc.dtype), vc)
    return o.reshape(Bt, Lc, Hq * Dh)


def _linrec_combine(left, right):
    a1, b1 = left
    a2, b2 = right
    return a1 * a2, a2 * b1 + b2


def s5_scan(u, lam_re, lam_im, log_dt, b_re, b_im, c_re, c_im, h0, reverse, readout):
    L = u.shape[1]
    lam = lax.complex(lam_re.astype(jnp.float32), lam_im.astype(jnp.float32))
    lam_dt = lam * jnp.exp(log_dt.astype(jnp.float32))[:, None]
    lam_bar = jnp.exp(lam_dt)
    b_bar = ((lam_bar - 1.0) / lam)[..., None] * lax.complex(b_re.astype(jnp.float32), b_im.astype(jnp.float32))
    bu = jnp.einsum('blgh,gph->blgp', u.astype(jnp.float32).astype(jnp.complex64), b_bar)
    a = jnp.broadcast_to(lam_bar, bu.shape)
    _, h = lax.associative_scan(_linrec_combine, (a, bu), axis=1, reverse=reverse)
    if h0 is not None:
        steps = jnp.arange(1, L + 1, dtype=jnp.float32)
        if reverse:
            steps = steps[::-1]
        h = h + jnp.exp(lam_dt[None] * steps[:, None, None])[None] * h0[:, None]
    h_last = h[:, 0] if reverse else h[:, -1]
    if not readout:
        return None, h_last
    c_mat = lax.complex(c_re.astype(jnp.float32), c_im.astype(jnp.float32))
    y = jnp.real(jnp.einsum('blgp,ghp->blgh', h, c_mat))
    return y, h_last


def s5_readout(y, u, p):
    Bt, L = y.shape[:2]
    z = jax.nn.gelu(y.reshape(Bt, L, SSM_WIDTH) + p['ssm_d'] * u.astype(jnp.float32))
    return z * jax.nn.sigmoid(z @ p['ssm_w_glu'] + p['ssm_b_glu'])


def s5_mixer(u, uc, p, need_ctx):
    Bt, L, _ = u.shape
    Lc = uc.shape[1]
    ug = u.astype(jnp.float32).reshape(Bt, L, SSM_GROUPS, SSM_GROUP)
    ucg = uc.astype(jnp.float32).reshape(Bt, Lc, SSM_GROUPS, SSM_GROUP)

    def dir_args(d):
        return (p['ssm_lam_re'][d], p['ssm_lam_im'][d], p['ssm_log_dt'][d], p['ssm_b_re'][d],
                p['ssm_b_im'][d], p['ssm_c_re'][d], p['ssm_c_im'][d])

    yc_f, hc_f = s5_scan(ucg, *dir_args(0), None, False, need_ctx)
    yc_b, hc_b = s5_scan(ucg, *dir_args(1), None, True, need_ctx)
    y_f, _ = s5_scan(ug, *dir_args(0), hc_f, False, True)
    y_b, _ = s5_scan(ug, *dir_args(1), hc_b, True, True)
    out = s5_readout(y_f + y_b, u, p)
    out_c = s5_readout(yc_f + yc_b, uc, p) if need_ctx else None
    return out, out_c


def gla_chunked(q, k, v, g, s0):
    Bt, L, H, K = q.shape
    V = v.shape[-1]
    T = GLA_CHUNK
    n = L // T
    qc = q.reshape(Bt, n, T, H, K)
    kc = k.reshape(Bt, n, T, H, K)
    vc = v.reshape(Bt, n, T, H, V)
    b = jnp.cumsum(g.reshape(Bt, n, T, H, K), axis=2)
    b_last = b[:, :, -1]
    q_in = qc * jnp.exp(b)
    attn = jnp.einsum('bnthk,bnshk->bnhts', q_in, kc * jnp.exp(-b))
    attn = jnp.where(jnp.tril(jnp.ones((T, T), dtype=bool)), attn, 0.0)
    o = jnp.einsum('bnhts,bnshv->bnthv', attn, vc)
    upd = jnp.einsum('bnshk,bnshv->bnhkv', kc * jnp.exp(b_last[:, :, None] - b), vc)
    decay = jnp.exp(b_last)
    if s0 is None:
        s0 = jnp.zeros((Bt, H, K, V), jnp.float32)

    def step(s, inp):
        dec, du = inp
        return dec[..., None] * s + du, s

    s_fin, s_prev = lax.scan(step, s0, (jnp.moveaxis(decay, 1, 0), jnp.moveaxis(upd, 1, 0)))
    o = o + jnp.einsum('bnthk,nbhkv->bnthv', q_in, s_prev)
    return o.reshape(Bt, L, H, V), s_fin


def gla_scan(q, k, v, g, s0, reverse):
    if not reverse:
        return gla_chunked(q, k, v, g, s0)
    fl = lambda t: jnp.flip(t, axis=1)
    o, s = gla_chunked(fl(q), fl(k), fl(v), fl(g), s0)
    return fl(o), s


def gla_prepare(q, k, v, z, p):
    Bt, L = q.shape[:2]
    qh = q.astype(jnp.float32).reshape(Bt, L, GLA_HEADS, GLA_DK) * (GLA_DK ** -0.5)
    kh = k.astype(jnp.float32).reshape(Bt, L, GLA_HEADS, GLA_DK)
    vh = v.astype(jnp.float32).reshape(Bt, L, GLA_HEADS, GLA_DV)
    zf = z.astype(jnp.float32)
    gs = [(jax.nn.log_sigmoid(zf[..., d * GLA_RANK:(d + 1) * GLA_RANK] @ p['gla_w_gate'][d] + p['gla_b_gate'][d])
           / GLA_TAU).reshape(Bt, L, GLA_HEADS, GLA_DK) for d in range(2)]
    return qh, kh, vh, gs


def gla_readout(o, r, p):
    Bt, L = o.shape[:2]
    o = o * lax.rsqrt(jnp.mean(jnp.square(o), axis=-1, keepdims=True) + LN_EPS) * p['gla_norm_g']
    return o.reshape(Bt, L, GLA_V_W) * jax.nn.silu(r.astype(jnp.float32))


def gla_mixer(q, k, v, r, z, qc_, kc_, vc_, rc, zc, p, need_ctx):
    qh, kh, vh, gs = gla_prepare(q, k, v, z, p)
    qch, kch, vch, gcs = gla_prepare(qc_, kc_, vc_, zc, p)
    oc_f, sc_f = gla_scan(qch, kch, vch, gcs[0], None, False)
    oc_b, sc_b = gla_scan(qch, kch, vch, gcs[1], None, True)
    o_f, _ = gla_scan(qh, kh, vh, gs[0], sc_f, False)
    o_b, _ = gla_scan(qh, kh, vh, gs[1], sc_b, True)
    out = gla_readout(o_f + o_b, r, p)
    out_c = gla_readout(oc_f + oc_b, rc, p) if need_ctx else None
    return out, out_c


def mixer_block(h, hc, p, rows, cols, need_ctx):
    Bt, L, _ = h.shape
    Lc = hc.shape[1]
    aq, ak, av, su, gq, gk, gv, gr, gz = jnp.split(h @ p['w_in'], IN_SPLITS, axis=-1)
    caq, cak, cav, csu, cgq, cgk, cgv, cgr, cgz = jnp.split(hc @ p['w_in'], IN_SPLITS, axis=-1)
    q = axial_rope(aq.reshape(Bt, L, ATT_HEADS, HEAD_DIM), rows, cols)
    k = axial_rope(ak.reshape(Bt, L, ATT_KV_HEADS, HEAD_DIM), rows, cols)
    v = av.reshape(Bt, L, ATT_KV_HEADS, HEAD_DIM)
    kc = cak.reshape(Bt, Lc, ATT_KV_HEADS, HEAD_DIM)
    vc = cav.reshape(Bt, Lc, ATT_KV_HEADS, HEAD_DIM)
    att = window_attention(q, k, v, kc, vc, p['attn_sink'])
    ssm, ssm_c = s5_mixer(su, csu, p, need_ctx)
    gla, gla_c = gla_mixer(gq, gk, gv, gr, gz, cgq, cgk, cgv, cgr, cgz, p, need_ctx)
    mix = jnp.concatenate([att.astype(h.dtype), ssm.astype(h.dtype), gla.astype(h.dtype)], axis=-1)
    if not need_ctx:
        return mix, None
    att_c = ctx_attention(caq.reshape(Bt, Lc, ATT_HEADS, HEAD_DIM), kc, vc, p['attn_sink'])
    mix_c = jnp.concatenate([att_c.astype(hc.dtype), ssm_c.astype(hc.dtype), gla_c.astype(hc.dtype)], axis=-1)
    return mix, mix_c


def expert_choice_ffn(h, router, w_gate, w_up, w_down):
    Bt, N, _ = h.shape
    cap = CAPACITY_FACTOR * N // N_EXPERTS
    aff = jax.nn.softmax(jnp.einsum('bnd,de->bne', h, router).astype(jnp.float32), axis=-1)
    g, idx = lax.top_k(jnp.swapaxes(aff, 1, 2), cap)
    bidx = jnp.arange(Bt)[:, None, None]
    xe = h[bidx, idx]
    hid = jax.nn.silu(jnp.einsum('becd,edf->becf', xe, w_gate)) * jnp.einsum('becd,edf->becf', xe, w_up)
    ye = jnp.einsum('becf,efd->becd', hid, w_down) * g[..., None].astype(h.dtype)
    return jnp.zeros_like(h).at[bidx, idx].add(ye)


def trunk_layer(x, xc, mod, mod_c, p, rows, cols, need_ctx):
    sh1, sc1, gt1, sh2, sc2, gt2 = jnp.split(mod, N_MOD, axis=-1)
    csh1, csc1, cgt1, csh2, csc2, cgt2 = jnp.split(mod_c, N_MOD, axis=-1)
    h = layer_norm(x) * (1.0 + sc1) + sh1
    hc = layer_norm(xc) * (1.0 + csc1) + csh1
    mix, mix_c = mixer_block(h, hc, p, rows, cols, need_ctx)
    x = layer_norm(ALPHA * x + gt1 * (mix @ p['w_out']), p['ln1_g'], p['ln1_b'])
    h2 = layer_norm(x) * (1.0 + sc2) + sh2
    ffn = expert_choice_ffn(h2, p['router'], p['exp_w_gate'], p['exp_w_up'], p['exp_w_down'])
    x = layer_norm(ALPHA * x + gt2 * ffn, p['ln2_g'], p['ln2_b'])
    if not need_ctx:
        return x, None
    xc = layer_norm(ALPHA * xc + cgt1 * (mix_c @ p['w_out']), p['ln1_g'], p['ln1_b'])
    hc2 = layer_norm(xc) * (1.0 + csc2) + csh2
    ffn_c = expert_choice_ffn(hc2, p['router'], p['exp_w_gate'], p['exp_w_up'], p['exp_w_down'])
    xc = layer_norm(ALPHA * xc + cgt2 * ffn_c, p['ln2_g'], p['ln2_b'])
    return x, xc


def setup_inputs(seed: int = 0) -> dict:
    key = jax.random.key(seed)
    ks = jax.random.split(key, 32)
    f32 = jnp.float32
    D = D_MODEL
    G, P, H = SSM_GROUPS, SSM_STATE, SSM_GROUP

    def nrm(k, shape, s):
        return jax.random.normal(k, shape, f32) * s

    lam_im_base = jnp.pi * jnp.arange(P, dtype=f32)
    return {
        'x': nrm(ks[0], (BATCH, SEQ, D), 1.0),
        'c': nrm(ks[1], (BATCH, D), 1.0),
        'ctx': nrm(ks[2], (BATCH, CTX_LEN, D), 1.0),
        'c_ctx': nrm(ks[3], (D,), 1.0),
        'w_ada': nrm(ks[4], (DEPTH, D, N_MOD * D), 0.5 * D ** -0.5),
        'b_ada': nrm(ks[5], (DEPTH, N_MOD * D), 0.01),
        'w_in': nrm(ks[6], (DEPTH, D, N_IN), D ** -0.5),
        'attn_sink': nrm(ks[7], (DEPTH, ATT_HEADS), 0.5),
        'ssm_lam_re': -0.5 + nrm(ks[8], (DEPTH, 2, G, P), 0.01),
        'ssm_lam_im': lam_im_base + nrm(ks[9], (DEPTH, 2, G, P), 0.01),
        'ssm_log_dt': jax.random.uniform(ks[10], (DEPTH, 2, G), f32, math.log(1e-3), math.log(1e-1)),
        'ssm_b_re': nrm(ks[11], (DEPTH, 2, G, P, H), (2 * H) ** -0.5),
        'ssm_b_im': nrm(ks[12], (DEPTH, 2, G, P, H), (2 * H) ** -0.5),
        'ssm_c_re': nrm(ks[13], (DEPTH, 2, G, H, P), P ** -0.5),
        'ssm_c_im': nrm(ks[14], (DEPTH, 2, G, H, P), P ** -0.5),
        'ssm_d': nrm(ks[15], (DEPTH, SSM_WIDTH), 1.0),
        'ssm_w_glu': nrm(ks[16], (DEPTH, SSM_WIDTH, SSM_WIDTH), SSM_WIDTH ** -0.5),
        'ssm_b_glu': nrm(ks[17], (DEPTH, SSM_WIDTH), 0.01),
        'gla_w_gate': nrm(ks[18], (DEPTH, 2, GLA_RANK, GLA_QK_W), GLA_RANK ** -0.5),
        'gla_b_gate': nrm(ks[19], (DEPTH, 2, GLA_QK_W), 0.1),
        'gla_norm_g': 1.0 + nrm(ks[20], (DEPTH, GLA_DV), 0.01),
        'w_out': nrm(ks[21], (DEPTH, MIX_WIDTH, D), BETA * MIX_WIDTH ** -0.5),
        'ln1_g': 1.0 + nrm(ks[22], (DEPTH, D), 0.01),
        'ln1_b': nrm(ks[23], (DEPTH, D), 0.01),
        'ln2_g': 1.0 + nrm(ks[24], (DEPTH, D), 0.01),
        'ln2_b': nrm(ks[25], (DEPTH, D), 0.01),
        'router': nrm(ks[26], (DEPTH, D, N_EXPERTS), D ** -0.5),
        'exp_w_gate': nrm(ks[27], (DEPTH, N_EXPERTS, D, EXPERT_FF), D ** -0.5),
        'exp_w_up': nrm(ks[28], (DEPTH, N_EXPERTS, D, EXPERT_FF), D ** -0.5),
        'exp_w_down': nrm(ks[29], (DEPTH, N_EXPERTS, EXPERT_FF, D), BETA * EXPERT_FF ** -0.5),
    }


def reference(x, c, ctx, c_ctx, w_ada, b_ada, w_in, attn_sink, ssm_lam_re, ssm_lam_im, ssm_log_dt,
              ssm_b_re, ssm_b_im, ssm_c_re, ssm_c_im, ssm_d, ssm_w_glu, ssm_b_glu, gla_w_gate,
              gla_b_gate, gla_norm_g, w_out, ln1_g, ln1_b, ln2_g, ln2_b, router, exp_w_gate,
              exp_w_up, exp_w_down):
    L = x.shape[1]
    ROWS = L // GRID_W
    rows = jnp.repeat(jnp.arange(ROWS), GRID_W)
    cols = jnp.tile(jnp.arange(GRID_W), ROWS)
    xc = ctx
    for l in range(DEPTH):
        need_ctx = l < DEPTH - 1
        p = {
            'w_in': w_in[l], 'attn_sink': attn_sink[l],
            'ssm_lam_re': ssm_lam_re[l], 'ssm_lam_im': ssm_lam_im[l], 'ssm_log_dt': ssm_log_dt[l],
            'ssm_b_re': ssm_b_re[l], 'ssm_b_im': ssm_b_im[l], 'ssm_c_re': ssm_c_re[l], 'ssm_c_im': ssm_c_im[l],
            'ssm_d': ssm_d[l], 'ssm_w_glu': ssm_w_glu[l], 'ssm_b_glu': ssm_b_glu[l],
            'gla_w_gate': gla_w_gate[l], 'gla_b_gate': gla_b_gate[l], 'gla_norm_g': gla_norm_g[l],
            'w_out': w_out[l], 'ln1_g': ln1_g[l], 'ln1_b': ln1_b[l], 'ln2_g': ln2_g[l], 'ln2_b': ln2_b[l],
            'router': router[l], 'exp_w_gate': exp_w_gate[l], 'exp_w_up': exp_w_up[l], 'exp_w_down': exp_w_down[l],
        }
        mod = (jax.nn.silu(c) @ w_ada[l] + b_ada[l])[:, None, :]
        mod_c = (jax.nn.silu(c_ctx) @ w_ada[l] + b_ada[l])[None, None, :]
        x, xc = trunk_layer(x, xc, mod, mod_c, p, rows, cols, need_ctx)
    return x
```

```python
import functools
import math

import jax
import jax.numpy as jnp
from jax import lax
from jax.experimental import pallas as pl
from jax.experimental.pallas import tpu as pltpu

F32 = jnp.float32
BF16 = jnp.bfloat16
HIGHEST = lax.Precision.HIGHEST

D_MODEL = 2048
DEPTH = 2
GRID_W = 64
CTX_LEN = 256
ATT_HEADS = 8
ATT_KV_HEADS = 2
HEAD_DIM = 128
ATT_BLOCK = 128
ROPE_BASE = 10000.0
SSM_WIDTH = 512
SSM_GROUP = 16
SSM_GROUPS = 32
SSM_STATE = 64
GLA_HEADS = 4
GLA_DK = 64
GLA_DV = 128
GLA_RANK = 16
GLA_TAU = 16.0
GLA_CHUNK = 64
N_EXPERTS = 16
EXPERT_FF = 2048
CAPACITY_FACTOR = 2
N_MOD = 6
ALPHA = (2 * DEPTH) ** 0.25
LN_EPS = 1e-6
NEG_INF = -1e30

ATT_Q_W = ATT_HEADS * HEAD_DIM
ATT_KV_W = ATT_KV_HEADS * HEAD_DIM
GLA_QK_W = GLA_HEADS * GLA_DK
GLA_V_W = GLA_HEADS * GLA_DV
N_IN = 3616
N_IN_PAD = 3712
GZ_PAD = N_IN_PAD - 3584

TOKEN_TILE = 256
SSM_T = 32
SSM_ROWS = 8
SSM_NCTX = CTX_LEN // SSM_T
VMEM_LIMIT = 56 * 1024 * 1024


def _cparams(sem, vmem=VMEM_LIMIT):
    return pltpu.CompilerParams(dimension_semantics=sem, vmem_limit_bytes=vmem)


def _dot(a, b, precision=None):
    return jnp.dot(a, b, preferred_element_type=F32, precision=precision)


def _dot_nt(a, b, precision=None):
    return lax.dot_general(a, b, (((1,), (1,)), ((), ())), preferred_element_type=F32, precision=precision)


def _dot_tn(a, b, precision=None):
    return lax.dot_general(a, b, (((0,), (0,)), ((), ())), preferred_element_type=F32, precision=precision)


def _sigmoid(x):
    return 1.0 / (1.0 + jnp.exp(-x))


def _ln(x):
    mu = jnp.mean(x, axis=-1, keepdims=True)
    xc = x - mu
    var = jnp.mean(xc * xc, axis=-1, keepdims=True)
    return xc * lax.rsqrt(var + LN_EPS)


def _ada_kernel(c_ref, w_ref, b_ref, o_ref):
    c = c_ref[...]
    s = c * _sigmoid(c)
    o_ref[0] = _dot(s, w_ref[0], HIGHEST) + b_ref[0]


def _ada(cc, w_ada, b_ada):
    depth, d, n = w_ada.shape
    tn = 1024
    return pl.pallas_call(
        _ada_kernel,
        grid=(depth, n // tn),
        in_specs=[
            pl.BlockSpec((8, d), lambda l, j: (0, 0)),
            pl.BlockSpec((1, d, tn), lambda l, j: (l, 0, j)),
            pl.BlockSpec((1, 1, tn), lambda l, j: (l, 0, j)),
        ],
        out_specs=pl.BlockSpec((1, 8, tn), lambda l, j: (l, 0, j)),
        out_shape=jax.ShapeDtypeStruct((depth, 8, n), F32),
        compiler_params=_cparams(("arbitrary", "arbitrary")),
    )(cc, w_ada, b_ada.reshape(depth, 1, n))


def _inproj_kernel(x_ref, m_ref, cos_ref, sa_ref, sb_ref, w_ref,
                   q_ref, k_ref, v_ref, su_ref, gqk_ref, gv_ref, gr_ref, gz_ref):
    x = x_ref[0]
    m = m_ref[0]
    h = (_ln(x) * (1.0 + m[1:2]) + m[0:1]).astype(BF16)
    cos = cos_ref[...]
    sa = sa_ref[...]
    sb = sb_ref[...]

    def rope(t):
        return t * cos + pltpu.roll(t, 96, 1) * sa + pltpu.roll(t, 32, 1) * sb

    scale = HEAD_DIM ** -0.5
    q = _dot(h, w_ref[:, 0:ATT_Q_W])
    for hd in range(ATT_HEADS):
        sl = slice(hd * HEAD_DIM, (hd + 1) * HEAD_DIM)
        q_ref[0, :, sl] = (rope(q[:, sl]) * scale).astype(BF16)
    kk = _dot(h, w_ref[:, 1024:1280])
    for hd in range(ATT_KV_HEADS):
        sl = slice(hd * HEAD_DIM, (hd + 1) * HEAD_DIM)
        k_ref[0, :, sl] = rope(kk[:, sl]).astype(BF16)
    v_ref[0] = _dot(h, w_ref[:, 1280:1536]).astype(BF16)
    su_ref[0] = _dot(h, w_ref[:, 1536:2048])
    gqk_ref[0] = _dot(h, w_ref[:, 2048:2560])
    gv_ref[0] = _dot(h, w_ref[:, 2560:3072])
    gr_ref[0] = _dot(h, w_ref[:, 3072:3584])
    gz_ref[0] = _dot(h, w_ref[:, 3584:N_IN_PAD])


def _inproj(x_all, mods, cos_t, sa_t, sb_t, w_in_bf16):
    bsz, nt, d = x_all.shape
    tm = TOKEN_TILE
    tiles = nt // tm
    tok = lambda width: pl.BlockSpec((1, tm, width), lambda b, i: (b, i, 0))
    tab = pl.BlockSpec((tm, HEAD_DIM), lambda b, i: (i, 0))
    shp = lambda width, dt: jax.ShapeDtypeStruct((bsz, nt, width), dt)
    return pl.pallas_call(
        _inproj_kernel,
        grid=(bsz, tiles),
        in_specs=[
            tok(d),
            pl.BlockSpec((1, N_MOD, d), lambda b, i: (2 * b + jnp.minimum(i, 1), 0, 0)),
            tab, tab, tab,
            pl.BlockSpec((d, N_IN_PAD), lambda b, i: (0, 0), pipeline_mode=pl.Buffered(1)),
        ],
        out_specs=[tok(ATT_Q_W), tok(ATT_KV_W), tok(ATT_KV_W), tok(SSM_WIDTH),
                   tok(2 * GLA_QK_W), tok(GLA_V_W), tok(GLA_V_W), tok(GZ_PAD)],
        out_shape=[shp(ATT_Q_W, BF16), shp(ATT_KV_W, BF16), shp(ATT_KV_W, BF16), shp(SSM_WIDTH, F32),
                   shp(2 * GLA_QK_W, F32), shp(GLA_V_W, F32), shp(GLA_V_W, F32), shp(GZ_PAD, F32)],
        compiler_params=_cparams(("arbitrary", "arbitrary")),
    )(x_all, mods, cos_t, sa_t, sb_t, w_in_bf16)


def _attn_kernel(sink_ref, q_ref, k_ref, v_ref, o_ref, *, blk0, nblk):
    w = ATT_BLOCK
    cb = CTX_LEN // w
    qb = pl.program_id(1) + blk0
    n = qb - cb
    last = nblk - 1
    prev_b = jnp.maximum(qb - 1, cb)
    cur_b = jnp.maximum(qb, cb)
    next_b = jnp.clip(qb + 1, cb, last)
    ok_prev = n >= 1
    ok_cur = n >= 0
    ok_next = jnp.logical_and(n >= 0, qb < last)
    g = ATT_HEADS // ATT_KV_HEADS
    rows = g * w
    ii = lax.broadcasted_iota(jnp.int32, (rows, w), 0) % w
    jj = lax.broadcasted_iota(jnp.int32, (rows, w), 1)
    m_prev = jnp.logical_and(jj >= ii, ok_prev)
    m_cur = jnp.logical_and(jj >= 0, ok_cur)
    m_next = jnp.logical_and(jj <= ii, ok_next)
    rowg = lax.broadcasted_iota(jnp.int32, (rows, 1), 0) // w

    def rows_of(ref, blk, sl):
        return ref[0, pl.ds(pl.multiple_of(blk * w, w), w), sl]

    for kvh in range(ATT_KV_HEADS):
        sl = slice(kvh * HEAD_DIM, (kvh + 1) * HEAD_DIM)
        qs = jnp.concatenate(
            [q_ref[0, :, (kvh * g + a) * HEAD_DIM:(kvh * g + a + 1) * HEAD_DIM] for a in range(g)], axis=0)
        sink = jnp.zeros((rows, 1), F32)
        for a in range(g):
            sink = jnp.where(rowg == a, sink_ref[kvh * g + a], sink)
        s_p = jnp.where(m_prev, _dot_nt(qs, rows_of(k_ref, prev_b, sl)), NEG_INF)
        s_c = jnp.where(m_cur, _dot_nt(qs, rows_of(k_ref, cur_b, sl)), NEG_INF)
        s_n = jnp.where(m_next, _dot_nt(qs, rows_of(k_ref, next_b, sl)), NEG_INF)
        s_x = _dot_nt(qs, k_ref[0, 0:CTX_LEN, sl])
        mx = jnp.maximum(jnp.maximum(jnp.max(s_p, axis=-1, keepdims=True), jnp.max(s_c, axis=-1, keepdims=True)),
                         jnp.maximum(jnp.max(s_n, axis=-1, keepdims=True), jnp.max(s_x, axis=-1, keepdims=True)))
        mx = jnp.maximum(mx, sink)
        p_p = jnp.exp(s_p - mx)
        p_c = jnp.exp(s_c - mx)
        p_n = jnp.exp(s_n - mx)
        p_x = jnp.exp(s_x - mx)
        den = (jnp.sum(p_p, axis=-1, keepdims=True) + jnp.sum(p_c, axis=-1, keepdims=True)
               + jnp.sum(p_n, axis=-1, keepdims=True) + jnp.sum(p_x, axis=-1, keepdims=True)
               + jnp.exp(sink - mx))
        o = (_dot(p_p.astype(BF16), rows_of(v_ref, prev_b, sl)) + _dot(p_c.astype(BF16), rows_of(v_ref, cur_b, sl))
             + _dot(p_n.astype(BF16), rows_of(v_ref, next_b, sl)) + _dot(p_x.astype(BF16), v_ref[0, 0:CTX_LEN, sl]))
        o = o / den
        for a in range(g):
            hs = slice((kvh * g + a) * HEAD_DIM, (kvh * g + a + 1) * HEAD_DIM)
            o_ref[0, :, hs] = o[a * w:(a + 1) * w].astype(BF16)


def _attention(sink, q_all, k_all, v_all, with_ctx):
    bsz, nt, _ = q_all.shape
    w = ATT_BLOCK
    nblk = nt // w
    blk0 = 0 if with_ctx else CTX_LEN // w
    return pl.pallas_call(
        functools.partial(_attn_kernel, blk0=blk0, nblk=nblk),
        grid=(bsz, nblk - blk0),
        in_specs=[
            pl.BlockSpec(memory_space=pltpu.SMEM),
            pl.BlockSpec((1, w, ATT_Q_W), lambda b, j: (b, j + blk0, 0)),
            pl.BlockSpec((1, nt, ATT_KV_W), lambda b, j: (b, 0, 0)),
            pl.BlockSpec((1, nt, ATT_KV_W), lambda b, j: (b, 0, 0)),
        ],
        out_specs=pl.BlockSpec((1, w, ATT_Q_W), lambda b, j: (b, j, 0)),
        out_shape=jax.ShapeDtypeStruct((bsz, (nblk - blk0) * w, ATT_Q_W), BF16),
        compiler_params=_cparams(("arbitrary", "arbitrary")),
    )(sink, q_all, k_all, v_all)


def _ssm_param_kernel(lr_ref, li_ref, ldt_ref, bre_ref, bim_ref, cre_ref, cim_ref,
                      ktoe_ref, lr_out, nli_out, gr_out, gi_out, a1_ref, a2_ref):
    t = SSM_T
    hp = (SSM_GROUP, SSM_STATE)
    per_dir = []
    for d in range(2):
        lam_r = lr_ref[d, 0]
        lam_i = li_ref[d, 0]
        dt = jnp.exp(ldt_ref[d, 0])
        ldr = lam_r * dt
        ldi = lam_i * dt
        mag = jnp.exp(ldr)
        lbr = mag * jnp.cos(ldi)
        lbi = mag * jnp.sin(ldi)
        den = lam_r * lam_r + lam_i * lam_i
        nr = lbr - 1.0
        cf_r = (nr * lam_r + lbi * lam_i) / den
        cf_i = (lbi * lam_r - nr * lam_i) / den
        br = bre_ref[d, 0]
        bi = bim_ref[d, 0]
        bbr = cf_r * br - cf_i * bi
        bbi = cf_r * bi + cf_i * br
        cr = cre_ref[d, 0]
        ci = cim_ref[d, 0]
        ks = []
        for k in range(t + 1):
            kf = float(k)
            mk = jnp.exp(kf * ldr)
            wr = mk * jnp.cos(kf * ldi)
            wi = mk * jnp.sin(kf * ldi)
            l_r = cr * wr - ci * wi
            l_i = cr * wi + ci * wr
            lr_out[d, 0, k] = l_r
            nli_out[d, 0, k] = -l_i
            gr_out[d, 0, k] = bbr * wr - bbi * wi
            gi_out[d, 0, k] = bbr * wi + bbi * wr
            if k < t:
                ks.append(_dot_nt(l_r, bbr, HIGHEST) - _dot_nt(l_i, bbi, HIGHEST))
            else:
                a1_ref[d, 0] = jnp.concatenate([wr, wr], axis=1)
                a2_ref[d, 0] = jnp.concatenate([-wi, wi], axis=1)
        per_dir.append(ks)
    kf_, kb_ = per_dir
    for lag in range(-(t - 1), t):
        if lag > 0:
            val = kf_[lag]
        elif lag < 0:
            val = kb_[-lag]
        else:
            val = kf_[0] + kb_[0]
        ktoe_ref[0, lag + t - 1] = val


def _ssm_params(lam_re, lam_im, log_dt, b_re, b_im, c_re, c_im):
    g, p, h, t = SSM_GROUPS, SSM_STATE, SSM_GROUP, SSM_T
    row = lambda a: a.reshape(2, g, 1, p)
    ldt = jnp.broadcast_to(log_dt[:, :, None, None], (2, g, 1, p))
    bt = lambda a: jnp.swapaxes(a, 2, 3)
    vec_spec = pl.BlockSpec((2, 1, 1, p), lambda i: (0, i, 0, 0))
    mat_spec = pl.BlockSpec((2, 1, h, p), lambda i: (0, i, 0, 0))
    pw_spec = pl.BlockSpec((2, 1, t + 1, h, p), lambda i: (0, i, 0, 0, 0))
    pw_shape = jax.ShapeDtypeStruct((2, g, t + 1, h, p), F32)
    a_spec = pl.BlockSpec((2, 1, 1, 2 * p), lambda i: (0, i, 0, 0))
    a_shape = jax.ShapeDtypeStruct((2, g, 1, 2 * p), F32)
    return pl.pallas_call(
        _ssm_param_kernel,
        grid=(g,),
        in_specs=[vec_spec, vec_spec, vec_spec, mat_spec, mat_spec, mat_spec, mat_spec],
        out_specs=[pl.BlockSpec((1, 2 * t - 1, h, h), lambda i: (i, 0, 0, 0)),
                   pw_spec, pw_spec, pw_spec, pw_spec, a_spec, a_spec],
        out_shape=[jax.ShapeDtypeStruct((g, 2 * t - 1, h, h), F32),
                   pw_shape, pw_shape, pw_shape, pw_shape, a_shape, a_shape],
        compiler_params=_cparams(("arbitrary",)),
    )(row(lam_re), row(lam_im), ldt, bt(b_re), bt(b_im), c_re, c_im)


def _ssm_operators(ktoe, l_r, nl_i, g_r, g_i, a1, a2):
    g, p, h, t = SSM_GROUPS, SSM_STATE, SSM_GROUP, SSM_T
    s_idx = jnp.arange(t)
    lag = s_idx[None, :] - s_idx[:, None] + (t - 1)
    mt = ktoe[:, lag]
    mt = mt.transpose(0, 1, 4, 2, 3).reshape(g, t * h, t * h)
    f_f = jnp.concatenate([g_r[0][:, t - 1 - s_idx], g_i[0][:, t - 1 - s_idx]], axis=-1)
    f_b = jnp.concatenate([g_r[1][:, s_idx], g_i[1][:, s_idx]], axis=-1)
    ft = jnp.concatenate([f_f, f_b], axis=-1).reshape(g, t * h, 4 * p)
    r1 = jnp.concatenate([mt, ft], axis=-1).astype(BF16)
    e_f = jnp.concatenate([l_r[0][:, 1 + s_idx], nl_i[0][:, 1 + s_idx]], axis=-1)
    e_b = jnp.concatenate([l_r[1][:, t - s_idx], nl_i[1][:, t - s_idx]], axis=-1)
    r2 = jnp.concatenate([e_f, e_b], axis=-1).reshape(g, t * h, 4 * p)
    r2 = jnp.swapaxes(r2, 1, 2).astype(BF16)
    flat = lambda a: a.reshape(1, g * 2 * p)
    return r1, r2, flat(a1[0]), flat(a2[0]), flat(a1[1]), flat(a2[1])


def _ssm_phase1_kernel(u_ref, r1_ref, y_ref, dsf_ref, dsb_ref):
    th = SSM_T * SSM_GROUP
    pp = 2 * SSM_STATE
    out = _dot(u_ref[0], r1_ref[0])
    y_ref[0] = out[:, 0:th]
    dsf_ref[...] = out[:, th:th + pp]
    dsb_ref[...] = out[:, th + pp:th + 2 * pp]


def _ssm_phase1(ug, r1):
    g, n, th = ug.shape
    pp = 2 * SSM_STATE
    return pl.pallas_call(
        _ssm_phase1_kernel,
        grid=(g,),
        in_specs=[pl.BlockSpec((1, n, th), lambda i: (i, 0, 0)),
                  pl.BlockSpec((1, th, th + 2 * pp), lambda i: (i, 0, 0))],
        out_specs=[pl.BlockSpec((1, n, th), lambda i: (i, 0, 0)),
                   pl.BlockSpec((n, pp), lambda i: (0, i)),
                   pl.BlockSpec((n, pp), lambda i: (0, i))],
        out_shape=[jax.ShapeDtypeStruct((g, n, th), F32),
                   jax.ShapeDtypeStruct((n, g * pp), F32),
                   jax.ShapeDtypeStruct((n, g * pp), F32)],
        compiler_params=_cparams(("arbitrary",)),
    )(ug, r1)


def _ssm_scan_kernel(dsf_ref, dsb_ref, a1f_ref, a2f_ref, a1b_ref, a2b_ref, sf_ref, sb_ref, *, nc):
    r = SSM_ROWS
    width = dsf_ref.shape[1]
    p = SSM_STATE
    lane = lax.broadcasted_iota(jnp.int32, (r, width), 1) % (2 * p)
    first_half = lane < p

    def swap(s):
        return jnp.where(first_half, pltpu.roll(s, width - p, 1), pltpu.roll(s, p, 1))

    def run(ds_ref, a1_ref, a2_ref, out_ref, chunk_of_step):
        a1 = a1_ref[...]
        a2 = a2_ref[...]

        def body(i, s):
            row = pl.multiple_of(chunk_of_step(i) * r, r)
            out_ref[pl.ds(row, r), :] = s
            return a1 * s + a2 * swap(s) + ds_ref[pl.ds(row, r), :]

        lax.fori_loop(0, nc, body, jnp.zeros((r, width), F32))

    run(dsf_ref, a1f_ref, a2f_ref, sf_ref, lambda i: i)
    run(dsb_ref, a1b_ref, a2b_ref, sb_ref,
        lambda i: jnp.where(i < SSM_NCTX, SSM_NCTX - 1 - i, nc + SSM_NCTX - 1 - i))


def _ssm_scan(dsf, dsb, a1f, a2f, a1b, a2b):
    n, width = dsf.shape
    nc = n // SSM_ROWS
    wb = 1024
    blk = pl.BlockSpec((n, wb), lambda i: (0, i))
    arow = pl.BlockSpec((1, wb), lambda i: (0, i))
    return pl.pallas_call(
        functools.partial(_ssm_scan_kernel, nc=nc),
        grid=(width // wb,),
        in_specs=[blk, blk, arow, arow, arow, arow],
        out_specs=[blk, blk],
        out_shape=[jax.ShapeDtypeStruct((n, width), F32), jax.ShapeDtypeStruct((n, width), F32)],
        compiler_params=_cparams(("arbitrary",)),
    )(dsf, dsb, a1f, a2f, a1b, a2b)


def _ssm_phase2_kernel(y_ref, sf_ref, sb_ref, r2_ref, o_ref):
    pp = 2 * SSM_STATE
    o_ref[0] = (y_ref[0] + _dot(sf_ref[...].astype(BF16), r2_ref[0, 0:pp, :])
                + _dot(sb_ref[...].astype(BF16), r2_ref[0, pp:2 * pp, :]))


def _ssm_phase2(yg, sf, sb, r2):
    g, n, th = yg.shape
    pp = 2 * SSM_STATE
    return pl.pallas_call(
        _ssm_phase2_kernel,
        grid=(g,),
        in_specs=[pl.BlockSpec((1, n, th), lambda i: (i, 0, 0)),
                  pl.BlockSpec((n, pp), lambda i: (0, i)),
                  pl.BlockSpec((n, pp), lambda i: (0, i)),
                  pl.BlockSpec((1, 2 * pp, th), lambda i: (i, 0, 0))],
        out_specs=pl.BlockSpec((1, n, th), lambda i: (i, 0, 0)),
        out_shape=jax.ShapeDtypeStruct((g, n, th), F32),
        compiler_params=_cparams(("arbitrary",)),
    )(yg, sf, sb, r2)


def _ssm_mix(su_all, ops):
    r1, r2, a1f, a2f, a1b, a2b = ops
    bsz, nt, _ = su_all.shape
    g, h, t, r = SSM_GROUPS, SSM_GROUP, SSM_T, SSM_ROWS
    nc = nt // t
    ug = su_all.astype(BF16).reshape(bsz, nc, t, g, h).transpose(3, 1, 0, 2, 4)
    ug = jnp.pad(ug, ((0, 0), (0, 0), (0, r - bsz), (0, 0), (0, 0))).reshape(g, nc * r, t * h)
    y1, dsf, dsb = _ssm_phase1(ug, r1)
    sf, sb = _ssm_scan(dsf, dsb, a1f, a2f, a1b, a2b)
    yg = _ssm_phase2(y1, sf, sb, r2)
    yg = yg.reshape(g, nc, r, t, h)[:, :, :bsz]
    return yg.transpose(2, 1, 3, 0, 4).reshape(bsz, nt, g * h)


def _gla_direction(qk, v, z, wg, bg, s_ref, d, o_ref, reverse):
    tb = qk.shape[0]
    c_len = GLA_CHUNK
    x = _dot(z.astype(BF16), wg) + bg
    gate = (jnp.minimum(x, 0.0) - jnp.log(1.0 + jnp.exp(-jnp.abs(x)))) * (1.0 / GLA_TAU)
    ti = lax.broadcasted_iota(jnp.int32, (c_len, c_len), 0)
    si = lax.broadcasted_iota(jnp.int32, (c_len, c_len), 1)
    keep = (si >= ti) if reverse else (si <= ti)
    tri = jnp.where(keep, 1.0, 0.0).astype(F32)
    chunks = range(tb // c_len)
    for c in (reversed(chunks) if reverse else chunks):
        rs = slice(c * c_len, (c + 1) * c_len)
        bcum = _dot(tri, gate[rs], HIGHEST)
        blast = bcum[0:1] if reverse else bcum[c_len - 1:c_len]
        q_in = qk[rs, 0:GLA_QK_W] * (GLA_DK ** -0.5) * jnp.exp(bcum)
        k_c = qk[rs, GLA_QK_W:2 * GLA_QK_W]
        k_in = (k_c * jnp.exp(-bcum)).astype(BF16)
        k_up = (k_c * jnp.exp(blast - bcum)).astype(BF16)
        q_in = q_in.astype(BF16)
        dec = jnp.exp(blast)
        for hd in range(GLA_HEADS):
            ks = slice(hd * GLA_DK, (hd + 1) * GLA_DK)
            vs = slice(hd * GLA_DV, (hd + 1) * GLA_DV)
            vh = v[rs, vs].astype(BF16)
            att = jnp.where(keep, _dot_nt(q_in[:, ks], k_in[:, ks]), 0.0)
            st = s_ref[d, hd]
            o_ref[0, rs, vs] = _dot(att.astype(BF16), vh) + _dot_nt(q_in[:, ks], st.astype(BF16))
            s_ref[d, hd] = st * dec[:, ks] + _dot_tn(vh, k_up[:, ks])


def _gla_kernel(qkf_ref, vf_ref, zf_ref, qkb_ref, vb_ref, zb_ref, wg_ref, bg_ref, of_ref, ob_ref, s_ref):
    @pl.when(pl.program_id(1) == 0)
    def _():
        s_ref[...] = jnp.zeros_like(s_ref)

    _gla_direction(qkf_ref[0], vf_ref[0], zf_ref[0], wg_ref[0], bg_ref[0], s_ref, 0, of_ref, False)
    _gla_direction(qkb_ref[0], vb_ref[0], zb_ref[0], wg_ref[1], bg_ref[1], s_ref, 1, ob_ref, True)


def _gla(gqk, gv, gz, wg_pad, bg):
    bsz, nt, _ = gqk.shape
    tb = TOKEN_TILE
    tiles = nt // tb
    fwd = lambda b, i: (b, i, 0)
    bwd = lambda b, i: (b, jnp.where(i == 0, 0, tiles - i), 0)
    spec = lambda width, im: pl.BlockSpec((1, tb, width), im)
    return pl.pallas_call(
        _gla_kernel,
        grid=(bsz, tiles),
        in_specs=[spec(2 * GLA_QK_W, fwd), spec(GLA_V_W, fwd), spec(GZ_PAD, fwd),
                  spec(2 * GLA_QK_W, bwd), spec(GLA_V_W, bwd), spec(GZ_PAD, bwd),
                  pl.BlockSpec((2, GZ_PAD, GLA_QK_W), lambda b, i: (0, 0, 0)),
                  pl.BlockSpec((2, 1, GLA_QK_W), lambda b, i: (0, 0, 0))],
        out_specs=[spec(GLA_V_W, fwd), spec(GLA_V_W, bwd)],
        out_shape=[jax.ShapeDtypeStruct((bsz, nt, GLA_V_W), F32), jax.ShapeDtypeStruct((bsz, nt, GLA_V_W), F32)],
        scratch_shapes=[pltpu.VMEM((2, GLA_HEADS, GLA_DV, GLA_DK), F32)],
        compiler_params=_cparams(("arbitrary", "arbitrary")),
    )(gqk, gv, gz, gqk, gv, gz, wg_pad, bg)


def _outproj_kernel(x_ref, m_ref, att_ref, y_ref, u_ref, of_ref, ob_ref, r_ref,
                    ssmd_ref, wglu_ref, bglu_ref, gng_ref, wo_ref, ln1g_ref, ln1b_ref, router_ref,
                    x1_ref, h2_ref, aff_ref):
    m = m_ref[0]
    zin = y_ref[0] + ssmd_ref[...] * u_ref[0]
    z = 0.5 * zin * (1.0 + jnp.tanh(math.sqrt(2.0 / math.pi) * (zin + 0.044715 * (zin * zin * zin))))
    ssm = z * _sigmoid(_dot(z.astype(BF16), wglu_ref[...]) + bglu_ref[...])
    proj = _dot(att_ref[0], wo_ref[0:ATT_Q_W, :]) + _dot(ssm.astype(BF16), wo_ref[ATT_Q_W:ATT_Q_W + SSM_WIDTH, :])
    o = of_ref[0] + ob_ref[0]
    r = r_ref[0]
    gng = gng_ref[...]
    base = ATT_Q_W + SSM_WIDTH
    for hd in range(GLA_HEADS):
        vs = slice(hd * GLA_DV, (hd + 1) * GLA_DV)
        oh = o[:, vs]
        rh = r[:, vs]
        oh = oh * lax.rsqrt(jnp.mean(oh * oh, axis=-1, keepdims=True) + LN_EPS) * gng
        gla = oh * (rh * _sigmoid(rh))
        proj = proj + _dot(gla.astype(BF16), wo_ref[base + hd * GLA_DV:base + (hd + 1) * GLA_DV, :])
    x1 = _ln(ALPHA * x_ref[0] + m[2:3] * proj) * ln1g_ref[...] + ln1b_ref[...]
    x1_ref[0] = x1
    h2 = _ln(x1) * (1.0 + m[4:5]) + m[3:4]
    h2_ref[0] = h2.astype(BF16)
    logits = _dot(h2, router_ref[...], HIGHEST)
    col = lax.broadcasted_iota(jnp.int32, logits.shape, 1)
    logits = jnp.where(col < N_EXPERTS, logits, NEG_INF)
    e = jnp.exp(logits - jnp.max(logits, axis=-1, keepdims=True))
    aff_ref[0] = e / jnp.sum(e, axis=-1, keepdims=True)


def _outproj(x_all, mods, att, y, su, o_f, o_b, gr, ssm_d, w_glu, b_glu, gn_g, w_out, ln1_g, ln1_b, router_pad,
             with_ctx):
    bsz, nt, d = x_all.shape
    tm = TOKEN_TILE
    t0 = 0 if with_ctx else CTX_LEN // tm
    tiles = nt // tm - t0
    att_t0 = 0 if with_ctx else -t0
    tok_in = lambda width: pl.BlockSpec((1, tm, width), lambda b, i: (b, i + t0, 0))
    tok_out = lambda width: pl.BlockSpec((1, tm, width), lambda b, i: (b, i, 0))
    const = lambda shape: pl.BlockSpec(shape, lambda b, i: tuple(0 for _ in shape))
    once = lambda shape: pl.BlockSpec(shape, lambda b, i: tuple(0 for _ in shape), pipeline_mode=pl.Buffered(1))
    shp = lambda width, dt: jax.ShapeDtypeStruct((bsz, tiles * tm, width), dt)
    return pl.pallas_call(
        _outproj_kernel,
        grid=(bsz, tiles),
        in_specs=[
            tok_in(d),
            pl.BlockSpec((1, N_MOD, d), lambda b, i: (2 * b + jnp.minimum(i + t0, 1), 0, 0)),
            pl.BlockSpec((1, tm, ATT_Q_W), lambda b, i: (b, i + t0 + att_t0, 0)),
            tok_in(SSM_WIDTH), tok_in(SSM_WIDTH), tok_in(GLA_V_W), tok_in(GLA_V_W), tok_in(GLA_V_W),
            const((1, SSM_WIDTH)), const((SSM_WIDTH, SSM_WIDTH)), const((1, SSM_WIDTH)), const((1, GLA_DV)),
            once((d, d)), const((1, d)), const((1, d)), const((d, 128)),
        ],
        out_specs=[tok_out(d), tok_out(d), tok_out(128)],
        out_shape=[shp(d, F32), shp(d, BF16), shp(128, F32)],
        compiler_params=_cparams(("arbitrary", "arbitrary")),
    )(x_all, mods, att, y, su, o_f, o_b, gr, ssm_d, w_glu, b_glu, gn_g, w_out, ln1_g, ln1_b, router_pad)


def _expert_kernel(x_ref, g_ref, wg_ref, wu_ref, wd_ref, o_ref, *, nf):
    f = pl.program_id(2)
    x = x_ref[0]
    a = _dot(x, wg_ref[0].astype(BF16))
    u = _dot(x, wu_ref[0].astype(BF16))
    hid = (a * _sigmoid(a)) * u
    y = _dot(hid.astype(BF16), wd_ref[0].astype(BF16))

    @pl.when(f == 0)
    def _():
        o_ref[0] = y

    @pl.when(f > 0)
    def _():
        o_ref[0] += y

    @pl.when(f == nf - 1)
    def _():
        o_ref[0] = o_ref[0] * g_ref[0]


def _experts(xe, ge, w_gate, w_up, w_down):
    e, r, d = xe.shape
    ff = w_gate.shape[2]
    tf = 256
    nf = ff // tf
    mt = 2
    tr = r // mt
    return pl.pallas_call(
        functools.partial(_expert_kernel, nf=nf),
        grid=(e, mt, nf),
        in_specs=[pl.BlockSpec((1, tr, d), lambda i, j, f: (i, j, 0)),
                  pl.BlockSpec((1, tr, 1), lambda i, j, f: (i, j, 0)),
                  pl.BlockSpec((1, d, tf), lambda i, j, f: (i, 0, f)),
                  pl.BlockSpec((1, d, tf), lambda i, j, f: (i, 0, f)),
                  pl.BlockSpec((1, tf, d), lambda i, j, f: (i, f, 0))],
        out_specs=pl.BlockSpec((1, tr, d), lambda i, j, f: (i, j, 0)),
        out_shape=jax.ShapeDtypeStruct((e, r, d), F32),
        compiler_params=_cparams(("arbitrary", "arbitrary", "arbitrary")),
    )(xe, ge, w_gate, w_up, w_down)


def _final_kernel(x1_ref, ffn_ref, m_ref, g_ref, b_ref, o_ref):
    m = m_ref[0]
    o_ref[0] = _ln(ALPHA * x1_ref[0] + m[5:6] * ffn_ref[0]) * g_ref[...] + b_ref[...]


def _final(x1, ffn, mods, ln2_g, ln2_b, with_ctx):
    bsz, nt, d = x1.shape
    tm = TOKEN_TILE
    kind0 = 0 if with_ctx else 1
    tok = pl.BlockSpec((1, tm, d), lambda b, i: (b, i, 0))
    vec = pl.BlockSpec((1, d), lambda b, i: (0, 0))
    return pl.pallas_call(
        _final_kernel,
        grid=(bsz, nt // tm),
        in_specs=[tok, tok,
                  pl.BlockSpec((1, N_MOD, d), lambda b, i: (2 * b + jnp.minimum(i + kind0, 1), 0, 0)),
                  vec, vec],
        out_specs=tok,
        out_shape=jax.ShapeDtypeStruct((bsz, nt, d), F32),
        compiler_params=_cparams(("arbitrary", "arbitrary")),
    )(x1, ffn, mods, ln2_g, ln2_b)


def _route(aff, cap):
    gate, idx = lax.top_k(jnp.swapaxes(aff, 1, 2), cap)
    return gate, idx


def _rope_tables(seq):
    half = HEAD_DIM // 2
    nf = half // 2
    inv = ROPE_BASE ** (-jnp.arange(nf, dtype=F32) / nf)
    pos = jnp.arange(seq)
    ang_r = (pos // GRID_W).astype(F32)[:, None] * inv
    ang_c = (pos % GRID_W).astype(F32)[:, None] * inv
    zeros = jnp.zeros_like(ang_r)
    cos = jnp.concatenate([jnp.cos(ang_r), jnp.cos(ang_r), jnp.cos(ang_c), jnp.cos(ang_c)], axis=-1)
    sa = jnp.concatenate([-jnp.sin(ang_r), zeros, -jnp.sin(ang_c), zeros], axis=-1)
    sb = jnp.concatenate([zeros, jnp.sin(ang_r), zeros, jnp.sin(ang_c)], axis=-1)
    ctx = lambda fill: jnp.full((CTX_LEN, HEAD_DIM), fill, F32)
    return (jnp.concatenate([ctx(1.0), cos], axis=0), jnp.concatenate([ctx(0.0), sa], axis=0),
            jnp.concatenate([ctx(0.0), sb], axis=0))


def kernel(x, c, ctx, c_ctx, w_ada, b_ada, w_in, attn_sink, ssm_lam_re, ssm_lam_im, ssm_log_dt,
           ssm_b_re, ssm_b_im, ssm_c_re, ssm_c_im, ssm_d, ssm_w_glu, ssm_b_glu, gla_w_gate,
           gla_b_gate, gla_norm_g, w_out, ln1_g, ln1_b, ln2_g, ln2_b, router, exp_w_gate,
           exp_w_up, exp_w_down):
    bsz, seq, d = x.shape
    n_exp = router.shape[-1]
    cc = jnp.zeros((8, d), F32).at[:bsz].set(c).at[bsz].set(c_ctx)
    mod_all = _ada(cc, w_ada, b_ada)
    cos_t, sa_t, sb_t = _rope_tables(seq)
    x_all = jnp.concatenate([ctx, x], axis=1)
    bidx = jnp.arange(bsz)[None, :, None]

    for l in range(DEPTH):
        with_ctx = l < DEPTH - 1
        m = mod_all[l].reshape(8, N_MOD, d)
        mods = jnp.stack([jnp.broadcast_to(m[bsz], (bsz, N_MOD, d)), m[:bsz]], axis=1).reshape(2 * bsz, N_MOD, d)
        w_in_p = jnp.pad(w_in[l], ((0, 0), (0, N_IN_PAD - N_IN))).astype(BF16)
        q, k, v, su, gqk, gv, gr, gz = _inproj(x_all, mods, cos_t, sa_t, sb_t, w_in_p)

        att = _attention(attn_sink[l], q, k, v, with_ctx)

        ops = _ssm_operators(*_ssm_params(ssm_lam_re[l], ssm_lam_im[l], ssm_log_dt[l], ssm_b_re[l],
                                          ssm_b_im[l], ssm_c_re[l], ssm_c_im[l]))
        y = _ssm_mix(su, ops)

        wg_pad = jnp.zeros((2, GZ_PAD, GLA_QK_W), F32)
        wg_pad = wg_pad.at[0, 0:GLA_RANK].set(gla_w_gate[l, 0]).at[1, GLA_RANK:2 * GLA_RANK].set(gla_w_gate[l, 1])
        o_f, o_b = _gla(gqk, gv, gz, wg_pad.astype(BF16), gla_b_gate[l][:, None, :])

        router_pad = jnp.pad(router[l], ((0, 0), (0, 128 - n_exp)))
        x1, h2, aff = _outproj(x_all, mods, att, y, su, o_f, o_b, gr,
                               ssm_d[l][None], ssm_w_glu[l].astype(BF16), ssm_b_glu[l][None], gla_norm_g[l][None],
                               w_out[l].astype(BF16), ln1_g[l][None], ln1_b[l][None], router_pad, with_ctx)

        t0 = CTX_LEN if with_ctx else 0
        cap = CAPACITY_FACTOR * seq // n_exp
        gate, idx = _route(aff[:, t0:, :n_exp], cap)
        idx = idx + t0
        if with_ctx:
            cap_c = CAPACITY_FACTOR * CTX_LEN // n_exp
            gate_c, idx_c = _route(aff[:, :CTX_LEN, :n_exp], cap_c)
            gate = jnp.concatenate([gate, gate_c], axis=-1)
            idx = jnp.concatenate([idx, idx_c], axis=-1)
        idx_e = jnp.swapaxes(idx, 0, 1)
        gate_e = jnp.swapaxes(gate, 0, 1)
        slots = idx_e.shape[-1]
        xe = h2[bidx, idx_e].reshape(n_exp, bsz * slots, d)
        ye = _experts(xe, gate_e.reshape(n_exp, bsz * slots, 1), exp_w_gate[l], exp_w_up[l], exp_w_down[l])
        ffn = jnp.zeros(x1.shape, F32).at[bidx, idx_e].add(ye.reshape(n_exp, bsz, slots, d))

        x_new = _final(x1, ffn, mods, ln2_g[l][None], ln2_b[l][None], with_ctx)
        x_all = x_new
    return x_all
```

```python
import functools
import math

import jax
import jax.numpy as jnp
from jax import lax
from jax.experimental import pallas as pl
from jax.experimental.pallas import tpu as pltpu

F32 = jnp.float32
BF16 = jnp.bfloat16
HIGHEST = lax.Precision.HIGHEST

D_MODEL = 2048
DEPTH = 2
GRID_W = 64
CTX_LEN = 256
ATT_HEADS = 8
ATT_KV_HEADS = 2
HEAD_DIM = 128
ATT_BLOCK = 128
ROPE_BASE = 10000.0
SSM_WIDTH = 512
SSM_GROUP = 16
SSM_GROUPS = 32
SSM_STATE = 64
GLA_HEADS = 4
GLA_DK = 64
GLA_DV = 128
GLA_RANK = 16
GLA_TAU = 16.0
GLA_CHUNK = 64
N_EXPERTS = 16
EXPERT_FF = 2048
CAPACITY_FACTOR = 2
N_MOD = 6
ALPHA = (2 * DEPTH) ** 0.25
LN_EPS = 1e-6
NEG_INF = -1e30

ATT_Q_W = ATT_HEADS * HEAD_DIM
ATT_KV_W = ATT_KV_HEADS * HEAD_DIM
GLA_QK_W = GLA_HEADS * GLA_DK
GLA_V_W = GLA_HEADS * GLA_DV
N_IN = 3616
N_IN_PAD = 3712
GZ_PAD = N_IN_PAD - 3584

TOKEN_TILE = 256
BATCH_TILE = 2
SSM_T = 32
SSM_NCTX = CTX_LEN // SSM_T
VMEM_LIMIT = 56 * 1024 * 1024


def _cparams(sem, vmem=VMEM_LIMIT):
    return pltpu.CompilerParams(dimension_semantics=sem, vmem_limit_bytes=vmem)


def _dot(a, b, precision=None):
    return jnp.dot(a, b, preferred_element_type=F32, precision=precision)


def _dot_nt(a, b, precision=None):
    return lax.dot_general(a, b, (((1,), (1,)), ((), ())), preferred_element_type=F32, precision=precision)


def _dot_tn(a, b, precision=None):
    return lax.dot_general(a, b, (((0,), (0,)), ((), ())), preferred_element_type=F32, precision=precision)


def _sigmoid(x):
    return 1.0 / (1.0 + jnp.exp(-x))


def _ln(x):
    mu = jnp.mean(x, axis=-1, keepdims=True)
    xc = x - mu
    var = jnp.mean(xc * xc, axis=-1, keepdims=True)
    return xc * lax.rsqrt(var + LN_EPS)


def _ada_kernel(c_ref, w_ref, b_ref, o_ref):
    c = c_ref[...]
    s = c * _sigmoid(c)
    o_ref[0] = _dot(s, w_ref[0], HIGHEST) + b_ref[0]


def _ada(cc, w_ada, b_ada):
    depth, d, n = w_ada.shape
    tn = 1024
    return pl.pallas_call(
        _ada_kernel,
        grid=(depth, n // tn),
        in_specs=[
            pl.BlockSpec((8, d), lambda l, j: (0, 0)),
            pl.BlockSpec((1, d, tn), lambda l, j: (l, 0, j)),
            pl.BlockSpec((1, 1, tn), lambda l, j: (l, 0, j)),
        ],
        out_specs=pl.BlockSpec((1, 8, tn), lambda l, j: (l, 0, j)),
        out_shape=jax.ShapeDtypeStruct((depth, 8, n), F32),
        compiler_params=_cparams(("arbitrary", "arbitrary")),
    )(cc, w_ada, b_ada.reshape(depth, 1, n))


def _inproj_kernel(x_ref, m_ref, cos_ref, sa_ref, sb_ref, w_ref,
                   q_ref, k_ref, v_ref, su_ref, gqk_ref, gv_ref, gr_ref, gz_ref):
    nb, tm, d = x_ref.shape
    rows = nb * tm
    m = m_ref[0]
    h = _ln(x_ref[...]) * (1.0 + m[:, 1:2]) + m[:, 0:1]
    h = h.reshape(rows, d).astype(BF16)
    tile = lambda ref: jnp.concatenate([ref[...]] * nb, axis=0)
    cos = tile(cos_ref)
    sa = tile(sa_ref)
    sb = tile(sb_ref)

    def rope(t):
        return t * cos + pltpu.roll(t, 96, 1) * sa + pltpu.roll(t, 32, 1) * sb

    def put(ref, val, sl=slice(None)):
        ref[:, :, sl] = val.reshape(nb, tm, val.shape[-1])

    scale = HEAD_DIM ** -0.5
    q = _dot(h, w_ref[:, 0:ATT_Q_W])
    for hd in range(ATT_HEADS):
        sl = slice(hd * HEAD_DIM, (hd + 1) * HEAD_DIM)
        put(q_ref, (rope(q[:, sl]) * scale).astype(BF16), sl)
    kk = _dot(h, w_ref[:, 1024:1280])
    for hd in range(ATT_KV_HEADS):
        sl = slice(hd * HEAD_DIM, (hd + 1) * HEAD_DIM)
        put(k_ref, rope(kk[:, sl]).astype(BF16), sl)
    put(v_ref, _dot(h, w_ref[:, 1280:1536]).astype(BF16))
    put(su_ref, _dot(h, w_ref[:, 1536:2048]))
    put(gqk_ref, _dot(h, w_ref[:, 2048:2560]))
    put(gv_ref, _dot(h, w_ref[:, 2560:3072]))
    put(gr_ref, _dot(h, w_ref[:, 3072:3584]))
    put(gz_ref, _dot(h, w_ref[:, 3584:N_IN_PAD]))


def _inproj(x_all, mods, cos_t, sa_t, sb_t, w_in_bf16):
    bsz, nt, d = x_all.shape
    tm = TOKEN_TILE
    nb = BATCH_TILE
    tiles = nt // tm
    tok = lambda width: pl.BlockSpec((nb, tm, width), lambda b, i: (b, i, 0))
    tab = pl.BlockSpec((tm, HEAD_DIM), lambda b, i: (i, 0))
    shp = lambda width, dt: jax.ShapeDtypeStruct((bsz, nt, width), dt)
    return pl.pallas_call(
        _inproj_kernel,
        grid=(bsz // nb, tiles),
        in_specs=[
            tok(d),
            pl.BlockSpec((1, nb, N_MOD, d), lambda b, i: (jnp.minimum(i, 1), b, 0, 0)),
            tab, tab, tab,
            pl.BlockSpec((d, N_IN_PAD), lambda b, i: (0, 0), pipeline_mode=pl.Buffered(1)),
        ],
        out_specs=[tok(ATT_Q_W), tok(ATT_KV_W), tok(ATT_KV_W), tok(SSM_WIDTH),
                   tok(2 * GLA_QK_W), tok(GLA_V_W), tok(GLA_V_W), tok(GZ_PAD)],
        out_shape=[shp(ATT_Q_W, BF16), shp(ATT_KV_W, BF16), shp(ATT_KV_W, BF16), shp(SSM_WIDTH, F32),
                   shp(2 * GLA_QK_W, F32), shp(GLA_V_W, F32), shp(GLA_V_W, F32), shp(GZ_PAD, F32)],
        compiler_params=_cparams(("arbitrary", "arbitrary")),
    )(x_all, mods, cos_t, sa_t, sb_t, w_in_bf16)


def _attn_kernel(sink_ref, q_ref, k_ref, v_ref, o_ref, *, blk0, nblk):
    w = ATT_BLOCK
    cb = CTX_LEN // w
    qb = pl.program_id(1) + blk0
    n = qb - cb
    last = nblk - 1
    prev_b = jnp.maximum(qb - 1, cb)
    cur_b = jnp.maximum(qb, cb)
    next_b = jnp.clip(qb + 1, cb, last)
    ok_prev = n >= 1
    ok_cur = n >= 0
    ok_next = jnp.logical_and(n >= 0, qb < last)
    g = ATT_HEADS // ATT_KV_HEADS
    rows = g * w
    ii = lax.broadcasted_iota(jnp.int32, (rows, w), 0) % w
    jj = lax.broadcasted_iota(jnp.int32, (rows, w), 1)
    m_prev = jnp.logical_and(jj >= ii, ok_prev)
    m_cur = jnp.logical_and(jj >= 0, ok_cur)
    m_next = jnp.logical_and(jj <= ii, ok_next)
    rowg = lax.broadcasted_iota(jnp.int32, (rows, 1), 0) // w

    def rows_of(ref, blk, sl):
        return ref[0, pl.ds(pl.multiple_of(blk * w, w), w), sl]

    for kvh in range(ATT_KV_HEADS):
        sl = slice(kvh * HEAD_DIM, (kvh + 1) * HEAD_DIM)
        qs = jnp.concatenate(
            [q_ref[0, :, (kvh * g + a) * HEAD_DIM:(kvh * g + a + 1) * HEAD_DIM] for a in range(g)], axis=0)
        sink = jnp.zeros((rows, 1), F32)
        for a in range(g):
            sink = jnp.where(rowg == a, sink_ref[kvh * g + a], sink)
        s_p = jnp.where(m_prev, _dot_nt(qs, rows_of(k_ref, prev_b, sl)), NEG_INF)
        s_c = jnp.where(m_cur, _dot_nt(qs, rows_of(k_ref, cur_b, sl)), NEG_INF)
        s_n = jnp.where(m_next, _dot_nt(qs, rows_of(k_ref, next_b, sl)), NEG_INF)
        s_x = _dot_nt(qs, k_ref[0, 0:CTX_LEN, sl])
        mx = jnp.maximum(jnp.maximum(jnp.max(s_p, axis=-1, keepdims=True), jnp.max(s_c, axis=-1, keepdims=True)),
                         jnp.maximum(jnp.max(s_n, axis=-1, keepdims=True), jnp.max(s_x, axis=-1, keepdims=True)))
        mx = jnp.maximum(mx, sink)
        p_p = jnp.exp(s_p - mx)
        p_c = jnp.exp(s_c - mx)
        p_n = jnp.exp(s_n - mx)
        p_x = jnp.exp(s_x - mx)
        den = (jnp.sum(p_p, axis=-1, keepdims=True) + jnp.sum(p_c, axis=-1, keepdims=True)
               + jnp.sum(p_n, axis=-1, keepdims=True) + jnp.sum(p_x, axis=-1, keepdims=True)
               + jnp.exp(sink - mx))
        o = (_dot(p_p.astype(BF16), rows_of(v_ref, prev_b, sl)) + _dot(p_c.astype(BF16), rows_of(v_ref, cur_b, sl))
             + _dot(p_n.astype(BF16), rows_of(v_ref, next_b, sl)) + _dot(p_x.astype(BF16), v_ref[0, 0:CTX_LEN, sl]))
        o = o / den
        for a in range(g):
            hs = slice((kvh * g + a) * HEAD_DIM, (kvh * g + a + 1) * HEAD_DIM)
            o_ref[0, :, hs] = o[a * w:(a + 1) * w].astype(BF16)


def _attention(sink, q_all, k_all, v_all, with_ctx):
    bsz, nt, _ = q_all.shape
    w = ATT_BLOCK
    nblk = nt // w
    blk0 = 0 if with_ctx else CTX_LEN // w
    return pl.pallas_call(
        functools.partial(_attn_kernel, blk0=blk0, nblk=nblk),
        grid=(bsz, nblk - blk0),
        in_specs=[
            pl.BlockSpec(memory_space=pltpu.SMEM),
            pl.BlockSpec((1, w, ATT_Q_W), lambda b, j: (b, j + blk0, 0)),
            pl.BlockSpec((1, nt, ATT_KV_W), lambda b, j: (b, 0, 0)),
            pl.BlockSpec((1, nt, ATT_KV_W), lambda b, j: (b, 0, 0)),
        ],
        out_specs=pl.BlockSpec((1, w, ATT_Q_W), lambda b, j: (b, j, 0)),
        out_shape=jax.ShapeDtypeStruct((bsz, (nblk - blk0) * w, ATT_Q_W), BF16),
        compiler_params=_cparams(("arbitrary", "arbitrary")),
    )(sink, q_all, k_all, v_all)


def _ssm_param_kernel(lr_ref, li_ref, ldt_ref, bre_ref, bim_ref, cre_ref, cim_ref,
                      ktoe_ref, lr_out, nli_out, gr_out, gi_out, a1_ref, a2_ref):
    t = SSM_T
    hp = (SSM_GROUP, SSM_STATE)
    per_dir = []
    for d in range(2):
        lam_r = lr_ref[d, 0]
        lam_i = li_ref[d, 0]
        dt = jnp.exp(ldt_ref[d, 0])
        ldr = lam_r * dt
        ldi = lam_i * dt
        mag = jnp.exp(ldr)
        lbr = mag * jnp.cos(ldi)
        lbi = mag * jnp.sin(ldi)
        den = lam_r * lam_r + lam_i * lam_i
        nr = lbr - 1.0
        cf_r = (nr * lam_r + lbi * lam_i) / den
        cf_i = (lbi * lam_r - nr * lam_i) / den
        br = bre_ref[d, 0]
        bi = bim_ref[d, 0]
        bbr = cf_r * br - cf_i * bi
        bbi = cf_r * bi + cf_i * br
        cr = cre_ref[d, 0]
        ci = cim_ref[d, 0]
        lrs, lis = [], []
        for k in range(t + 1):
            kf = float(k)
            mk = jnp.exp(kf * ldr)
            wr = mk * jnp.cos(kf * ldi)
            wi = mk * jnp.sin(kf * ldi)
            l_r = cr * wr - ci * wi
            l_i = cr * wi + ci * wr
            lr_out[d, 0, k] = l_r
            nli_out[d, 0, k] = -l_i
            gr_out[d, 0, k] = bbr * wr - bbi * wi
            gi_out[d, 0, k] = bbr * wi + bbi * wr
            if k < t:
                lrs.append(l_r)
                lis.append(l_i)
            else:
                a1_ref[d, 0] = wr
                a2_ref[d, 0] = wi
        taps = (_dot_nt(jnp.concatenate(lrs, axis=0), bbr, HIGHEST)
                - _dot_nt(jnp.concatenate(lis, axis=0), bbi, HIGHEST))
        per_dir.append([taps[k * SSM_GROUP:(k + 1) * SSM_GROUP] for k in range(t)])
    kf_, kb_ = per_dir
    for lag in range(-(t - 1), t):
        if lag > 0:
            val = kf_[lag]
        elif lag < 0:
            val = kb_[-lag]
        else:
            val = kf_[0] + kb_[0]
        ktoe_ref[0, lag + t - 1] = val


def _ssm_params(lam_re, lam_im, log_dt, b_re, b_im, c_re, c_im):
    g, p, h, t = SSM_GROUPS, SSM_STATE, SSM_GROUP, SSM_T
    row = lambda a: a.reshape(2, g, 1, p)
    ldt = jnp.broadcast_to(log_dt[:, :, None, None], (2, g, 1, p))
    bt = lambda a: jnp.swapaxes(a, 2, 3)
    vec_spec = pl.BlockSpec((2, 1, 1, p), lambda i: (0, i, 0, 0))
    mat_spec = pl.BlockSpec((2, 1, h, p), lambda i: (0, i, 0, 0))
    pw_spec = pl.BlockSpec((2, 1, t + 1, h, p), lambda i: (0, i, 0, 0, 0))
    pw_shape = jax.ShapeDtypeStruct((2, g, t + 1, h, p), F32)
    a_spec = pl.BlockSpec((2, 1, 1, p), lambda i: (0, i, 0, 0))
    a_shape = jax.ShapeDtypeStruct((2, g, 1, p), F32)
    return pl.pallas_call(
        _ssm_param_kernel,
        grid=(g,),
        in_specs=[vec_spec, vec_spec, vec_spec, mat_spec, mat_spec, mat_spec, mat_spec],
        out_specs=[pl.BlockSpec((1, 2 * t - 1, h, h), lambda i: (i, 0, 0, 0)),
                   pw_spec, pw_spec, pw_spec, pw_spec, a_spec, a_spec],
        out_shape=[jax.ShapeDtypeStruct((g, 2 * t - 1, h, h), F32),
                   pw_shape, pw_shape, pw_shape, pw_shape, a_shape, a_shape],
        compiler_params=_cparams(("arbitrary",)),
    )(row(lam_re), row(lam_im), ldt, bt(b_re), bt(b_im), c_re, c_im)


def _ssm_operators(ktoe, l_r, nl_i, g_r, g_i, a_re, a_im):
    g, p, h, t = SSM_GROUPS, SSM_STATE, SSM_GROUP, SSM_T
    s_idx = jnp.arange(t)
    lag = s_idx[None, :] - s_idx[:, None] + (t - 1)
    mt = ktoe[:, lag]
    mt = mt.transpose(0, 1, 4, 2, 3).reshape(g, t * h, t * h)
    f_f = jnp.concatenate([g_r[0][:, t - 1 - s_idx], g_i[0][:, t - 1 - s_idx]], axis=-1)
    f_b = jnp.concatenate([g_r[1][:, s_idx], g_i[1][:, s_idx]], axis=-1)
    ft = jnp.concatenate([f_f, f_b], axis=-1).reshape(g, t * h, 4 * p)
    r1 = jnp.concatenate([mt, ft], axis=-1).astype(BF16)
    e_f = jnp.concatenate([l_r[0][:, 1 + s_idx], nl_i[0][:, 1 + s_idx]], axis=-1)
    e_b = jnp.concatenate([l_r[1][:, t - s_idx], nl_i[1][:, t - s_idx]], axis=-1)
    r2 = jnp.concatenate([e_f, e_b], axis=-1).reshape(g, t * h, 4 * p)
    r2 = jnp.swapaxes(r2, 1, 2).astype(BF16)
    flat = lambda a: a.reshape(1, g * p)
    return r1, r2, flat(a_re[0]), flat(a_im[0]), flat(a_re[1]), flat(a_im[1])


_LANE_GROUPS = 128 // SSM_GROUP
_ROWS8 = 8


def _to_groups_kernel(su_ref, ug_ref, *, nc):
    t, h = SSM_T, SSM_GROUP
    lane_grp = lax.broadcasted_iota(jnp.int32, (_ROWS8, 128), 1) // h

    def body(rb, carry):
        base = pl.multiple_of(rb * (_ROWS8 * t), _ROWS8 * t)
        rows = pl.ds(pl.multiple_of(rb * _ROWS8, _ROWS8), _ROWS8)
        ps = [su_ref[0, pl.ds(base + s, _ROWS8, stride=t), :] for s in range(t)]
        for gm in range(_LANE_GROUPS):
            for j in range(t * h // 128):
                acc = jnp.zeros((_ROWS8, 128), F32)
                for sm in range(_LANE_GROUPS):
                    shift = ((sm - gm) * h) % 128
                    src = ps[_LANE_GROUPS * j + sm]
                    acc = jnp.where(lane_grp == sm, pltpu.roll(src, shift, 1) if shift else src, acc)
                ug_ref[gm, rows, j * 128:(j + 1) * 128] = acc
        return carry

    lax.fori_loop(0, nc // _ROWS8, body, 0)


def _to_groups(su):
    bsz, nt, w = su.shape
    nc = nt // SSM_T
    return pl.pallas_call(
        functools.partial(_to_groups_kernel, nc=nc),
        grid=(bsz, w // 128),
        in_specs=[pl.BlockSpec((1, nt, 128), lambda b, a: (b, 0, a))],
        out_specs=pl.BlockSpec((_LANE_GROUPS, nc, SSM_T * SSM_GROUP), lambda b, a: (a, b, 0)),
        out_shape=jax.ShapeDtypeStruct((SSM_GROUPS, bsz * nc, SSM_T * SSM_GROUP), F32),
        compiler_params=_cparams(("arbitrary", "arbitrary")),
    )(su)


def _from_groups_kernel(yg_ref, y_ref, *, nc):
    t, h = SSM_T, SSM_GROUP
    lane_grp = lax.broadcasted_iota(jnp.int32, (_ROWS8, 128), 1) // h

    def body(rb, carry):
        base = pl.multiple_of(rb * (_ROWS8 * t), _ROWS8 * t)
        rows = pl.ds(pl.multiple_of(rb * _ROWS8, _ROWS8), _ROWS8)
        for tt in range(t):
            j, sm = divmod(tt, _LANE_GROUPS)
            acc = jnp.zeros((_ROWS8, 128), F32)
            for gm in range(_LANE_GROUPS):
                shift = ((gm - sm) * h) % 128
                src = yg_ref[gm, rows, j * 128:(j + 1) * 128]
                acc = jnp.where(lane_grp == gm, pltpu.roll(src, shift, 1) if shift else src, acc)
            y_ref[0, pl.ds(base + tt, _ROWS8, stride=t), :] = acc
        return carry

    lax.fori_loop(0, nc // _ROWS8, body, 0)


def _from_groups(yg, bsz):
    g, n, th = yg.shape
    nc = n // bsz
    nt = nc * SSM_T
    return pl.pallas_call(
        functools.partial(_from_groups_kernel, nc=nc),
        grid=(bsz, g // _LANE_GROUPS),
        in_specs=[pl.BlockSpec((_LANE_GROUPS, nc, th), lambda b, a: (a, b, 0))],
        out_specs=pl.BlockSpec((1, nt, 128), lambda b, a: (b, 0, a)),
        out_shape=jax.ShapeDtypeStruct((bsz, nt, g * SSM_GROUP), F32),
        compiler_params=_cparams(("arbitrary", "arbitrary")),
    )(yg)


def _ssm_phase1_kernel(u_ref, r1_ref, y_ref, fr_ref, fi_ref, br_ref, bi_ref):
    th = SSM_T * SSM_GROUP
    p = SSM_STATE
    outs = [_dot(u_ref[k].astype(BF16), r1_ref[k]) for k in range(2)]
    for k in range(2):
        y_ref[k] = outs[k][:, 0:th]
    for idx, ref in enumerate((fr_ref, fi_ref, br_ref, bi_ref)):
        lo = th + idx * p
        ref[...] = jnp.concatenate([o[:, lo:lo + p] for o in outs], axis=1)


def _ssm_phase1(ug, r1):
    g, n, th = ug.shape
    p = SSM_STATE
    plane = pl.BlockSpec((n, 2 * p), lambda i: (0, i))
    plane_shape = jax.ShapeDtypeStruct((n, g * p), F32)
    return pl.pallas_call(
        _ssm_phase1_kernel,
        grid=(g // 2,),
        in_specs=[pl.BlockSpec((2, n, th), lambda i: (i, 0, 0)),
                  pl.BlockSpec((2, th, th + 4 * p), lambda i: (i, 0, 0))],
        out_specs=[pl.BlockSpec((2, n, th), lambda i: (i, 0, 0)), plane, plane, plane, plane],
        out_shape=[jax.ShapeDtypeStruct((g, n, th), F32), plane_shape, plane_shape, plane_shape, plane_shape],
        compiler_params=_cparams(("arbitrary",)),
    )(ug, r1)


def _ssm_scan_kernel(dfr_ref, dfi_ref, dbr_ref, dbi_ref, afr_ref, afi_ref, abr_ref, abi_ref,
                     sfr_ref, sfi_ref, sbr_ref, sbi_ref, *, nc, bsz):
    width = dfr_ref.shape[1]

    def run(dr_ref, di_ref, ar_ref, ai_ref, or_ref, oi_ref, chunk_of_step):
        ar = ar_ref[...]
        ai = ai_ref[...]

        def body(i, s):
            sr, si = s
            rows = pl.ds(chunk_of_step(i), bsz, stride=nc)
            or_ref[rows, :] = sr
            oi_ref[rows, :] = si
            return (ar * sr - ai * si + dr_ref[rows, :], ar * si + ai * sr + di_ref[rows, :])

        zero = jnp.zeros((bsz, width), F32)
        lax.fori_loop(0, nc, body, (zero, zero))

    run(dfr_ref, dfi_ref, afr_ref, afi_ref, sfr_ref, sfi_ref, lambda i: i)
    run(dbr_ref, dbi_ref, abr_ref, abi_ref, sbr_ref, sbi_ref,
        lambda i: jnp.where(i < SSM_NCTX, SSM_NCTX - 1 - i, nc + SSM_NCTX - 1 - i))


def _ssm_scan(planes, decays, bsz):
    n, width = planes[0].shape
    blk = pl.BlockSpec((n, 128), lambda i: (0, i))
    arow = pl.BlockSpec((1, 128), lambda i: (0, i))
    shape = jax.ShapeDtypeStruct((n, width), F32)
    return pl.pallas_call(
        functools.partial(_ssm_scan_kernel, nc=n // bsz, bsz=bsz),
        grid=(width // 128,),
        in_specs=[blk] * 4 + [arow] * 4,
        out_specs=[blk] * 4,
        out_shape=[shape] * 4,
        compiler_params=_cparams(("arbitrary",)),
    )(*planes, *decays)


def _ssm_phase2_kernel(y_ref, sfr_ref, sfi_ref, sbr_ref, sbi_ref, r2_ref, o_ref):
    p = SSM_STATE
    for k in range(2):
        acc = y_ref[k]
        for idx, ref in enumerate((sfr_ref, sfi_ref, sbr_ref, sbi_ref)):
            acc = acc + _dot(ref[:, k * p:(k + 1) * p].astype(BF16), r2_ref[k, idx * p:(idx + 1) * p, :])
        o_ref[k] = acc


def _ssm_phase2(yg, states, r2):
    g, n, th = yg.shape
    p = SSM_STATE
    plane = pl.BlockSpec((n, 2 * p), lambda i: (0, i))
    return pl.pallas_call(
        _ssm_phase2_kernel,
        grid=(g // 2,),
        in_specs=[pl.BlockSpec((2, n, th), lambda i: (i, 0, 0)), plane, plane, plane, plane,
                  pl.BlockSpec((2, 4 * p, th), lambda i: (i, 0, 0))],
        out_specs=pl.BlockSpec((2, n, th), lambda i: (i, 0, 0)),
        out_shape=jax.ShapeDtypeStruct((g, n, th), F32),
        compiler_params=_cparams(("arbitrary",)),
    )(yg, *states, r2)


def _ssm_mix(su_all, ops):
    r1, r2 = ops[0], ops[1]
    bsz = su_all.shape[0]
    ug = _to_groups(su_all)
    y1, *planes = _ssm_phase1(ug, r1)
    states = _ssm_scan(planes, ops[2:], bsz)
    return _from_groups(_ssm_phase2(y1, states, r2), bsz)


def _gla_direction(qk, v, z, wg, bg, s_ref, d, o_ref, reverse):
    tb = qk.shape[0]
    c_len = GLA_CHUNK
    x = _dot(z.astype(BF16), wg) + bg
    gate = (jnp.minimum(x, 0.0) - jnp.log(1.0 + jnp.exp(-jnp.abs(x)))) * (1.0 / GLA_TAU)
    ti = lax.broadcasted_iota(jnp.int32, (c_len, c_len), 0)
    si = lax.broadcasted_iota(jnp.int32, (c_len, c_len), 1)
    keep = (si >= ti) if reverse else (si <= ti)
    tri = jnp.where(keep, 1.0, 0.0).astype(F32)
    chunks = range(tb // c_len)
    for c in (reversed(chunks) if reverse else chunks):
        rs = slice(c * c_len, (c + 1) * c_len)
        bcum = _dot(tri, gate[rs], HIGHEST)
        blast = bcum[0:1] if reverse else bcum[c_len - 1:c_len]
        q_in = qk[rs, 0:GLA_QK_W] * (GLA_DK ** -0.5) * jnp.exp(bcum)
        k_c = qk[rs, GLA_QK_W:2 * GLA_QK_W]
        k_in = (k_c * jnp.exp(-bcum)).astype(BF16)
        k_up = (k_c * jnp.exp(blast - bcum)).astype(BF16)
        q_in = q_in.astype(BF16)
        dec = jnp.exp(blast)
        for hd in range(GLA_HEADS):
            ks = slice(hd * GLA_DK, (hd + 1) * GLA_DK)
            vs = slice(hd * GLA_DV, (hd + 1) * GLA_DV)
            vh = v[rs, vs].astype(BF16)
            att = jnp.where(keep, _dot_nt(q_in[:, ks], k_in[:, ks]), 0.0)
            st = s_ref[d, hd]
            o_ref[0, rs, vs] = _dot(att.astype(BF16), vh) + _dot_nt(q_in[:, ks], st.astype(BF16))
            s_ref[d, hd] = st * dec[:, ks] + _dot_tn(vh, k_up[:, ks])


def _gla_kernel(qkf_ref, vf_ref, zf_ref, qkb_ref, vb_ref, zb_ref, wg_ref, bg_ref, of_ref, ob_ref, s_ref):
    @pl.when(pl.program_id(1) == 0)
    def _():
        s_ref[...] = jnp.zeros_like(s_ref)

    _gla_direction(qkf_ref[0], vf_ref[0], zf_ref[0], wg_ref[0], bg_ref[0], s_ref, 0, of_ref, False)
    _gla_direction(qkb_ref[0], vb_ref[0], zb_ref[0], wg_ref[1], bg_ref[1], s_ref, 1, ob_ref, True)


def _gla(gqk, gv, gz, wg_pad, bg):
    bsz, nt, _ = gqk.shape
    tb = TOKEN_TILE
    tiles = nt // tb
    fwd = lambda b, i: (b, i, 0)
    bwd = lambda b, i: (b, jnp.where(i == 0, 0, tiles - i), 0)
    spec = lambda width, im: pl.BlockSpec((1, tb, width), im)
    return pl.pallas_call(
        _gla_kernel,
        grid=(bsz, tiles),
        in_specs=[spec(2 * GLA_QK_W, fwd), spec(GLA_V_W, fwd), spec(GZ_PAD, fwd),
                  spec(2 * GLA_QK_W, bwd), spec(GLA_V_W, bwd), spec(GZ_PAD, bwd),
                  pl.BlockSpec((2, GZ_PAD, GLA_QK_W), lambda b, i: (0, 0, 0)),
                  pl.BlockSpec((2, 1, GLA_QK_W), lambda b, i: (0, 0, 0))],
        out_specs=[spec(GLA_V_W, fwd), spec(GLA_V_W, bwd)],
        out_shape=[jax.ShapeDtypeStruct((bsz, nt, GLA_V_W), F32), jax.ShapeDtypeStruct((bsz, nt, GLA_V_W), F32)],
        scratch_shapes=[pltpu.VMEM((2, GLA_HEADS, GLA_DV, GLA_DK), F32)],
        compiler_params=_cparams(("arbitrary", "arbitrary")),
    )(gqk, gv, gz, gqk, gv, gz, wg_pad, bg)


def _outproj_kernel(x_ref, m_ref, att_ref, y_ref, u_ref, of_ref, ob_ref, r_ref,
                    ssmd_ref, wglu_ref, bglu_ref, gng_ref, wo_ref, ln1g_ref, ln1b_ref, router_ref,
                    x1_ref, h2_ref, aff_ref):
    nb, tm, d = x_ref.shape
    rows = nb * tm
    flat = lambda ref: ref[...].reshape(rows, ref.shape[-1])
    m = m_ref[0]
    zin = flat(y_ref) + ssmd_ref[...] * flat(u_ref)
    z = 0.5 * zin * (1.0 + jnp.tanh(math.sqrt(2.0 / math.pi) * (zin + 0.044715 * (zin * zin * zin))))
    ssm = z * _sigmoid(_dot(z.astype(BF16), wglu_ref[...]) + bglu_ref[...])
    proj = _dot(flat(att_ref), wo_ref[0:ATT_Q_W, :]) + _dot(ssm.astype(BF16), wo_ref[ATT_Q_W:ATT_Q_W + SSM_WIDTH, :])
    o = flat(of_ref) + flat(ob_ref)
    r = flat(r_ref)
    gng = gng_ref[...]
    base = ATT_Q_W + SSM_WIDTH
    for hd in range(GLA_HEADS):
        vs = slice(hd * GLA_DV, (hd + 1) * GLA_DV)
        oh = o[:, vs]
        rh = r[:, vs]
        oh = oh * lax.rsqrt(jnp.mean(oh * oh, axis=-1, keepdims=True) + LN_EPS) * gng
        gla = oh * (rh * _sigmoid(rh))
        proj = proj + _dot(gla.astype(BF16), wo_ref[base + hd * GLA_DV:base + (hd + 1) * GLA_DV, :])
    proj = proj.reshape(nb, tm, d)
    x1 = _ln(ALPHA * x_ref[...] + m[:, 2:3] * proj) * ln1g_ref[...] + ln1b_ref[...]
    x1_ref[...] = x1
    h2 = (_ln(x1) * (1.0 + m[:, 4:5]) + m[:, 3:4]).reshape(rows, d)
    h_hi = h2.astype(BF16)
    h2_ref[...] = h_hi.reshape(nb, tm, d)
    h_lo = (h2 - h_hi.astype(F32)).astype(BF16)
    l_hi = _dot(h_hi, router_ref[...])
    logits = l_hi + pltpu.roll(l_hi, 128 - N_EXPERTS, 1) + _dot(h_lo, router_ref[...])
    col = lax.broadcasted_iota(jnp.int32, logits.shape, 1)
    logits = jnp.where(col < N_EXPERTS, logits, NEG_INF)
    e = jnp.exp(logits - jnp.max(logits, axis=-1, keepdims=True))
    aff_ref[...] = (e / jnp.sum(e, axis=-1, keepdims=True)).reshape(nb, tm, 128)


def _outproj(x_all, mods, att, y, su, o_f, o_b, gr, ssm_d, w_glu, b_glu, gn_g, w_out, ln1_g, ln1_b, router_split,
             with_ctx):
    bsz, nt, d = x_all.shape
    tm = TOKEN_TILE
    nb = BATCH_TILE
    t0 = 0 if with_ctx else CTX_LEN // tm
    tiles = nt // tm - t0
    att_t0 = 0 if with_ctx else -t0
    tok_in = lambda width: pl.BlockSpec((nb, tm, width), lambda b, i: (b, i + t0, 0))
    tok_out = lambda width: pl.BlockSpec((nb, tm, width), lambda b, i: (b, i, 0))
    const = lambda shape: pl.BlockSpec(shape, lambda b, i: tuple(0 for _ in shape))
    once = lambda shape: pl.BlockSpec(shape, lambda b, i: tuple(0 for _ in shape), pipeline_mode=pl.Buffered(1))
    shp = lambda width, dt: jax.ShapeDtypeStruct((bsz, tiles * tm, width), dt)
    return pl.pallas_call(
        _outproj_kernel,
        grid=(bsz // nb, tiles),
        in_specs=[
            tok_in(d),
            pl.BlockSpec((1, nb, N_MOD, d), lambda b, i: (jnp.minimum(i + t0, 1), b, 0, 0)),
            pl.BlockSpec((nb, tm, ATT_Q_W), lambda b, i: (b, i + t0 + att_t0, 0)),
            tok_in(SSM_WIDTH), tok_in(SSM_WIDTH), tok_in(GLA_V_W), tok_in(GLA_V_W), tok_in(GLA_V_W),
            const((1, SSM_WIDTH)), const((SSM_WIDTH, SSM_WIDTH)), const((1, SSM_WIDTH)), const((1, GLA_DV)),
            once((d, d)), const((1, d)), const((1, d)), const((d, 128)),
        ],
        out_specs=[tok_out(d), tok_out(d), tok_out(128)],
        out_shape=[shp(d, F32), shp(d, BF16), shp(128, F32)],
        compiler_params=_cparams(("arbitrary", "arbitrary")),
    )(x_all, mods, att, y, su, o_f, o_b, gr, ssm_d, w_glu, b_glu, gn_g, w_out, ln1_g, ln1_b, router_split)


def _expert_kernel(x_ref, g_ref, wg_ref, wu_ref, wd_ref, o_ref, hid_ref, *, nf, tf):
    s = pl.program_id(1)

    @pl.when(s < nf)
    def _():
        x = x_ref[0]
        a = _dot(x, wg_ref[0, 0].astype(BF16))
        u = _dot(x, wu_ref[0, 0].astype(BF16))
        col = pl.multiple_of(s * tf, tf)
        hid_ref[:, pl.ds(col, tf)] = ((a * _sigmoid(a)) * u).astype(BF16)

    @pl.when(s >= nf)
    def _():
        o_ref[0] = _dot(hid_ref[...], wd_ref[0, 0].astype(BF16)) * g_ref[0]


def _experts(xe, ge, w_gate, w_up, w_down, layer):
    e, r, d = xe.shape
    ff = w_gate.shape[3]
    tf = 256
    nf = ff // tf
    nd = d // tf
    up = lambda i, s: (layer, i, 0, jnp.minimum(s, nf - 1))
    down = lambda i, s: (layer, i, 0, jnp.maximum(s - nf, 0))
    return pl.pallas_call(
        functools.partial(_expert_kernel, nf=nf, tf=tf),
        grid=(e, nf + nd),
        in_specs=[pl.BlockSpec((1, r, d), lambda i, s: (i, 0, 0), pipeline_mode=pl.Buffered(1)),
                  pl.BlockSpec((1, r, 1), lambda i, s: (i, 0, 0)),
                  pl.BlockSpec((1, 1, d, tf), up),
                  pl.BlockSpec((1, 1, d, tf), up),
                  pl.BlockSpec((1, 1, ff, tf), down)],
        out_specs=pl.BlockSpec((1, r, tf), lambda i, s: (i, 0, jnp.maximum(s - nf, 0))),
        out_shape=jax.ShapeDtypeStruct((e, r, d), F32),
        scratch_shapes=[pltpu.VMEM((r, ff), BF16)],
        compiler_params=_cparams(("arbitrary", "arbitrary")),
    )(xe, ge, w_gate, w_up, w_down)


def _final_kernel(x1_ref, ffn_ref, m_ref, g_ref, b_ref, o_ref):
    m = m_ref[0, 0]
    o_ref[0] = _ln(ALPHA * x1_ref[0] + m[5:6] * ffn_ref[0]) * g_ref[...] + b_ref[...]


def _final(x1, ffn, mods, ln2_g, ln2_b, with_ctx):
    bsz, nt, d = x1.shape
    tm = TOKEN_TILE
    kind0 = 0 if with_ctx else 1
    tok = pl.BlockSpec((1, tm, d), lambda b, i: (b, i, 0))
    vec = pl.BlockSpec((1, d), lambda b, i: (0, 0))
    return pl.pallas_call(
        _final_kernel,
        grid=(bsz, nt // tm),
        in_specs=[tok, tok,
                  pl.BlockSpec((1, 1, N_MOD, d), lambda b, i: (jnp.minimum(i + kind0, 1), b, 0, 0)),
                  vec, vec],
        out_specs=tok,
        out_shape=jax.ShapeDtypeStruct((bsz, nt, d), F32),
        compiler_params=_cparams(("arbitrary", "arbitrary")),
    )(x1, ffn, mods, ln2_g, ln2_b)


def _route(aff, cap):
    gate, idx = lax.top_k(jnp.swapaxes(aff, 1, 2), cap)
    return gate, idx


def _rope_tables(seq):
    half = HEAD_DIM // 2
    nf = half // 2
    inv = ROPE_BASE ** (-jnp.arange(nf, dtype=F32) / nf)
    pos = jnp.arange(seq)
    ang_r = (pos // GRID_W).astype(F32)[:, None] * inv
    ang_c = (pos % GRID_W).astype(F32)[:, None] * inv
    zeros = jnp.zeros_like(ang_r)
    cos = jnp.concatenate([jnp.cos(ang_r), jnp.cos(ang_r), jnp.cos(ang_c), jnp.cos(ang_c)], axis=-1)
    sa = jnp.concatenate([-jnp.sin(ang_r), zeros, -jnp.sin(ang_c), zeros], axis=-1)
    sb = jnp.concatenate([zeros, jnp.sin(ang_r), zeros, jnp.sin(ang_c)], axis=-1)
    ctx = lambda fill: jnp.full((CTX_LEN, HEAD_DIM), fill, F32)
    return (jnp.concatenate([ctx(1.0), cos], axis=0), jnp.concatenate([ctx(0.0), sa], axis=0),
            jnp.concatenate([ctx(0.0), sb], axis=0))


def kernel(x, c, ctx, c_ctx, w_ada, b_ada, w_in, attn_sink, ssm_lam_re, ssm_lam_im, ssm_log_dt,
           ssm_b_re, ssm_b_im, ssm_c_re, ssm_c_im, ssm_d, ssm_w_glu, ssm_b_glu, gla_w_gate,
           gla_b_gate, gla_norm_g, w_out, ln1_g, ln1_b, ln2_g, ln2_b, router, exp_w_gate,
           exp_w_up, exp_w_down):
    bsz, seq, d = x.shape
    n_exp = router.shape[-1]
    cc = jnp.zeros((8, d), F32).at[:bsz].set(c).at[bsz].set(c_ctx)
    mod_all = _ada(cc, w_ada, b_ada)
    cos_t, sa_t, sb_t = _rope_tables(seq)
    x_all = jnp.concatenate([ctx, x], axis=1)
    bidx = jnp.arange(bsz)[None, :, None]

    for l in range(DEPTH):
        with_ctx = l < DEPTH - 1
        m = mod_all[l].reshape(8, N_MOD, d)
        mods = jnp.stack([jnp.broadcast_to(m[bsz], (bsz, N_MOD, d)), m[:bsz]], axis=0)
        w_in_p = jnp.pad(w_in[l], ((0, 0), (0, N_IN_PAD - N_IN))).astype(BF16)
        q, k, v, su, gqk, gv, gr, gz = _inproj(x_all, mods, cos_t, sa_t, sb_t, w_in_p)

        att = _attention(attn_sink[l], q, k, v, with_ctx)

        ops = _ssm_operators(*_ssm_params(ssm_lam_re[l], ssm_lam_im[l], ssm_log_dt[l], ssm_b_re[l],
                                          ssm_b_im[l], ssm_c_re[l], ssm_c_im[l]))
        y = _ssm_mix(su, ops)

        wg_pad = jnp.zeros((2, GZ_PAD, GLA_QK_W), F32)
        wg_pad = wg_pad.at[0, 0:GLA_RANK].set(gla_w_gate[l, 0]).at[1, GLA_RANK:2 * GLA_RANK].set(gla_w_gate[l, 1])
        o_f, o_b = _gla(gqk, gv, gz, wg_pad.astype(BF16), gla_b_gate[l][:, None, :])

        r_hi = router[l].astype(BF16)
        r_lo = (router[l] - r_hi.astype(F32)).astype(BF16)
        router_split = jnp.pad(jnp.concatenate([r_hi, r_lo], axis=1), ((0, 0), (0, 128 - 2 * n_exp)))
        x1, h2, aff = _outproj(x_all, mods, att, y, su, o_f, o_b, gr,
                               ssm_d[l][None], ssm_w_glu[l].astype(BF16), ssm_b_glu[l][None], gla_norm_g[l][None],
                               w_out[l].astype(BF16), ln1_g[l][None], ln1_b[l][None], router_split, with_ctx)

        t0 = CTX_LEN if with_ctx else 0
        cap = CAPACITY_FACTOR * seq // n_exp
        gate, idx = _route(aff[:, t0:, :n_exp], cap)
        idx = idx + t0
        if with_ctx:
            cap_c = CAPACITY_FACTOR * CTX_LEN // n_exp
            gate_c, idx_c = _route(aff[:, :CTX_LEN, :n_exp], cap_c)
            gate = jnp.concatenate([gate, gate_c], axis=-1)
            idx = jnp.concatenate([idx, idx_c], axis=-1)
        idx_e = jnp.swapaxes(idx, 0, 1)
        gate_e = jnp.swapaxes(gate, 0, 1)
        slots = idx_e.shape[-1]
        xe = h2[bidx, idx_e].reshape(n_exp, bsz * slots, d)
        ye = _experts(xe, gate_e.reshape(n_exp, bsz * slots, 1), exp_w_gate, exp_w_up, exp_w_down, l)
        ffn = jnp.zeros(x1.shape, F32).at[bidx, idx_e].add(ye.reshape(n_exp, bsz, slots, d))

        x_new = _final(x1, ffn, mods, ln2_g[l][None], ln2_b[l][None], with_ctx)
        x_all = x_new
    return x_all
```

```python
import functools
import math

import jax
import jax.numpy as jnp
from jax import lax
from jax.experimental import pallas as pl
from jax.experimental.pallas import tpu as pltpu

F32 = jnp.float32
BF16 = jnp.bfloat16
HIGHEST = lax.Precision.HIGHEST

D_MODEL = 2048
DEPTH = 2
GRID_W = 64
CTX_LEN = 256
ATT_HEADS = 8
ATT_KV_HEADS = 2
HEAD_DIM = 128
ATT_BLOCK = 128
ROPE_BASE = 10000.0
SSM_WIDTH = 512
SSM_GROUP = 16
SSM_GROUPS = 32
SSM_STATE = 64
GLA_HEADS = 4
GLA_DK = 64
GLA_DV = 128
GLA_RANK = 16
GLA_TAU = 16.0
GLA_CHUNK = 64
N_EXPERTS = 16
EXPERT_FF = 2048
CAPACITY_FACTOR = 2
N_MOD = 6
ALPHA = (2 * DEPTH) ** 0.25
LN_EPS = 1e-6
NEG_INF = -1e30

ATT_Q_W = ATT_HEADS * HEAD_DIM
ATT_KV_W = ATT_KV_HEADS * HEAD_DIM
GLA_QK_W = GLA_HEADS * GLA_DK
GLA_V_W = GLA_HEADS * GLA_DV
N_IN = 3616
N_IN_PAD = 3712
GZ_PAD = N_IN_PAD - 3584

TOKEN_TILE = 256
BATCH_TILE = 2
SSM_T = 32
SSM_NCTX = CTX_LEN // SSM_T
VMEM_LIMIT = 56 * 1024 * 1024


def _cparams(sem, vmem=VMEM_LIMIT):
    return pltpu.CompilerParams(dimension_semantics=sem, vmem_limit_bytes=vmem)


def _dot(a, b, precision=None):
    return jnp.dot(a, b, preferred_element_type=F32, precision=precision)


def _dot_nt(a, b, precision=None):
    return lax.dot_general(a, b, (((1,), (1,)), ((), ())), preferred_element_type=F32, precision=precision)


def _dot_tn(a, b, precision=None):
    return lax.dot_general(a, b, (((0,), (0,)), ((), ())), preferred_element_type=F32, precision=precision)


def _sigmoid(x):
    return 1.0 / (1.0 + jnp.exp(-x))


def _ln(x):
    mu = jnp.mean(x, axis=-1, keepdims=True)
    xc = x - mu
    var = jnp.mean(xc * xc, axis=-1, keepdims=True)
    return xc * lax.rsqrt(var + LN_EPS)


def _ada_kernel(c_ref, w_ref, b_ref, o_ref):
    c = c_ref[...]
    s = c * _sigmoid(c)
    o_ref[0] = _dot(s, w_ref[0], HIGHEST) + b_ref[0]


def _ada(cc, w_ada, b_ada):
    depth, d, n = w_ada.shape
    tn = 1024
    return pl.pallas_call(
        _ada_kernel,
        grid=(depth, n // tn),
        in_specs=[
            pl.BlockSpec((8, d), lambda l, j: (0, 0)),
            pl.BlockSpec((1, d, tn), lambda l, j: (l, 0, j)),
            pl.BlockSpec((1, 1, tn), lambda l, j: (l, 0, j)),
        ],
        out_specs=pl.BlockSpec((1, 8, tn), lambda l, j: (l, 0, j)),
        out_shape=jax.ShapeDtypeStruct((depth, 8, n), F32),
        compiler_params=_cparams(("arbitrary", "arbitrary")),
    )(cc, w_ada, b_ada.reshape(depth, 1, n))


def _inproj_kernel(x_ref, m_ref, cos_ref, sa_ref, sb_ref, w_ref,
                   q_ref, k_ref, v_ref, su_ref, gqk_ref, gv_ref, gr_ref, gz_ref):
    nb, tm, d = x_ref.shape
    rows = nb * tm
    m = m_ref[0]
    h = _ln(x_ref[...]) * (1.0 + m[:, 1:2]) + m[:, 0:1]
    h = h.reshape(rows, d).astype(BF16)
    tile = lambda ref: jnp.concatenate([ref[...]] * nb, axis=0)
    cos = tile(cos_ref)
    sa = tile(sa_ref)
    sb = tile(sb_ref)

    def rope(t):
        return t * cos + pltpu.roll(t, 96, 1) * sa + pltpu.roll(t, 32, 1) * sb

    def put(ref, val, sl=slice(None)):
        ref[:, :, sl] = val.reshape(nb, tm, val.shape[-1])

    scale = HEAD_DIM ** -0.5
    q = _dot(h, w_ref[:, 0:ATT_Q_W])
    for hd in range(ATT_HEADS):
        sl = slice(hd * HEAD_DIM, (hd + 1) * HEAD_DIM)
        put(q_ref, (rope(q[:, sl]) * scale).astype(BF16), sl)
    kk = _dot(h, w_ref[:, 1024:1280])
    for hd in range(ATT_KV_HEADS):
        sl = slice(hd * HEAD_DIM, (hd + 1) * HEAD_DIM)
        put(k_ref, rope(kk[:, sl]).astype(BF16), sl)
    put(v_ref, _dot(h, w_ref[:, 1280:1536]).astype(BF16))
    put(su_ref, _dot(h, w_ref[:, 1536:2048]))
    put(gqk_ref, _dot(h, w_ref[:, 2048:2560]))
    put(gv_ref, _dot(h, w_ref[:, 2560:3072]))
    put(gr_ref, _dot(h, w_ref[:, 3072:3584]))
    put(gz_ref, _dot(h, w_ref[:, 3584:N_IN_PAD]))


def _inproj(x_all, mods, cos_t, sa_t, sb_t, w_in_bf16):
    bsz, nt, d = x_all.shape
    tm = TOKEN_TILE
    nb = BATCH_TILE
    tiles = nt // tm
    tok = lambda width: pl.BlockSpec((nb, tm, width), lambda b, i: (b, i, 0))
    tab = pl.BlockSpec((tm, HEAD_DIM), lambda b, i: (i, 0))
    shp = lambda width, dt: jax.ShapeDtypeStruct((bsz, nt, width), dt)
    return pl.pallas_call(
        _inproj_kernel,
        grid=(bsz // nb, tiles),
        in_specs=[
            tok(d),
            pl.BlockSpec((1, nb, N_MOD, d), lambda b, i: (jnp.minimum(i, 1), b, 0, 0)),
            tab, tab, tab,
            pl.BlockSpec((d, N_IN_PAD), lambda b, i: (0, 0), pipeline_mode=pl.Buffered(1)),
        ],
        out_specs=[tok(ATT_Q_W), tok(ATT_KV_W), tok(ATT_KV_W), tok(SSM_WIDTH),
                   tok(2 * GLA_QK_W), tok(GLA_V_W), tok(GLA_V_W), tok(GZ_PAD)],
        out_shape=[shp(ATT_Q_W, BF16), shp(ATT_KV_W, BF16), shp(ATT_KV_W, BF16), shp(SSM_WIDTH, F32),
                   shp(2 * GLA_QK_W, F32), shp(GLA_V_W, F32), shp(GLA_V_W, F32), shp(GZ_PAD, F32)],
        compiler_params=_cparams(("arbitrary", "arbitrary")),
    )(x_all, mods, cos_t, sa_t, sb_t, w_in_bf16)


def _attn_kernel(sink_ref, q_ref, k_ref, v_ref, o_ref, *, blk0, nblk):
    w = ATT_BLOCK
    cb = CTX_LEN // w
    qb = pl.program_id(1) + blk0
    n = qb - cb
    last = nblk - 1
    prev_b = jnp.maximum(qb - 1, cb)
    cur_b = jnp.maximum(qb, cb)
    next_b = jnp.clip(qb + 1, cb, last)
    ok_prev = n >= 1
    ok_cur = n >= 0
    ok_next = jnp.logical_and(n >= 0, qb < last)
    g = ATT_HEADS // ATT_KV_HEADS
    rows = g * w
    ii = lax.broadcasted_iota(jnp.int32, (rows, w), 0) % w
    jj = lax.broadcasted_iota(jnp.int32, (rows, w), 1)
    m_prev = jnp.logical_and(jj >= ii, ok_prev)
    m_cur = jnp.logical_and(jj >= 0, ok_cur)
    m_next = jnp.logical_and(jj <= ii, ok_next)
    rowg = lax.broadcasted_iota(jnp.int32, (rows, 1), 0) // w

    def rows_of(ref, blk, sl):
        return ref[0, pl.ds(pl.multiple_of(blk * w, w), w), sl]

    for kvh in range(ATT_KV_HEADS):
        sl = slice(kvh * HEAD_DIM, (kvh + 1) * HEAD_DIM)
        qs = jnp.concatenate(
            [q_ref[0, :, (kvh * g + a) * HEAD_DIM:(kvh * g + a + 1) * HEAD_DIM] for a in range(g)], axis=0)
        sink = jnp.zeros((rows, 1), F32)
        for a in range(g):
            sink = jnp.where(rowg == a, sink_ref[kvh * g + a], sink)
        s_p = jnp.where(m_prev, _dot_nt(qs, rows_of(k_ref, prev_b, sl)), NEG_INF)
        s_c = jnp.where(m_cur, _dot_nt(qs, rows_of(k_ref, cur_b, sl)), NEG_INF)
        s_n = jnp.where(m_next, _dot_nt(qs, rows_of(k_ref, next_b, sl)), NEG_INF)
        s_x = _dot_nt(qs, k_ref[0, 0:CTX_LEN, sl])
        mx = jnp.maximum(jnp.maximum(jnp.max(s_p, axis=-1, keepdims=True), jnp.max(s_c, axis=-1, keepdims=True)),
                         jnp.maximum(jnp.max(s_n, axis=-1, keepdims=True), jnp.max(s_x, axis=-1, keepdims=True)))
        mx = jnp.maximum(mx, sink)
        p_p = jnp.exp(s_p - mx)
        p_c = jnp.exp(s_c - mx)
        p_n = jnp.exp(s_n - mx)
        p_x = jnp.exp(s_x - mx)
        den = (jnp.sum(p_p, axis=-1, keepdims=True) + jnp.sum(p_c, axis=-1, keepdims=True)
               + jnp.sum(p_n, axis=-1, keepdims=True) + jnp.sum(p_x, axis=-1, keepdims=True)
               + jnp.exp(sink - mx))
        o = (_dot(p_p.astype(BF16), rows_of(v_ref, prev_b, sl)) + _dot(p_c.astype(BF16), rows_of(v_ref, cur_b, sl))
             + _dot(p_n.astype(BF16), rows_of(v_ref, next_b, sl)) + _dot(p_x.astype(BF16), v_ref[0, 0:CTX_LEN, sl]))
        o = o / den
        for a in range(g):
            hs = slice((kvh * g + a) * HEAD_DIM, (kvh * g + a + 1) * HEAD_DIM)
            o_ref[0, :, hs] = o[a * w:(a + 1) * w].astype(BF16)


def _attention(sink, q_all, k_all, v_all, with_ctx):
    bsz, nt, _ = q_all.shape
    w = ATT_BLOCK
    nblk = nt // w
    blk0 = 0 if with_ctx else CTX_LEN // w
    return pl.pallas_call(
        functools.partial(_attn_kernel, blk0=blk0, nblk=nblk),
        grid=(bsz, nblk - blk0),
        in_specs=[
            pl.BlockSpec(memory_space=pltpu.SMEM),
            pl.BlockSpec((1, w, ATT_Q_W), lambda b, j: (b, j + blk0, 0)),
            pl.BlockSpec((1, nt, ATT_KV_W), lambda b, j: (b, 0, 0)),
            pl.BlockSpec((1, nt, ATT_KV_W), lambda b, j: (b, 0, 0)),
        ],
        out_specs=pl.BlockSpec((1, w, ATT_Q_W), lambda b, j: (b, j, 0)),
        out_shape=jax.ShapeDtypeStruct((bsz, (nblk - blk0) * w, ATT_Q_W), BF16),
        compiler_params=_cparams(("arbitrary", "arbitrary")),
    )(sink, q_all, k_all, v_all)


def _ssm_param_kernel(lr_ref, li_ref, ldt_ref, bre_ref, bim_ref, cre_ref, cim_ref,
                      ktoe_ref, lr_out, nli_out, gr_out, gi_out, a1_ref, a2_ref):
    t = SSM_T
    hp = (SSM_GROUP, SSM_STATE)
    per_dir = []
    for d in range(2):
        lam_r = lr_ref[d, 0]
        lam_i = li_ref[d, 0]
        dt = jnp.exp(ldt_ref[d, 0])
        ldr = lam_r * dt
        ldi = lam_i * dt
        mag = jnp.exp(ldr)
        lbr = mag * jnp.cos(ldi)
        lbi = mag * jnp.sin(ldi)
        den = lam_r * lam_r + lam_i * lam_i
        nr = lbr - 1.0
        cf_r = (nr * lam_r + lbi * lam_i) / den
        cf_i = (lbi * lam_r - nr * lam_i) / den
        br = bre_ref[d, 0]
        bi = bim_ref[d, 0]
        bbr = cf_r * br - cf_i * bi
        bbi = cf_r * bi + cf_i * br
        cr = cre_ref[d, 0]
        ci = cim_ref[d, 0]
        lrs, lis = [], []
        for k in range(t + 1):
            kf = float(k)
            mk = jnp.exp(kf * ldr)
            wr = mk * jnp.cos(kf * ldi)
            wi = mk * jnp.sin(kf * ldi)
            l_r = cr * wr - ci * wi
            l_i = cr * wi + ci * wr
            lr_out[d, 0, k] = l_r
            nli_out[d, 0, k] = -l_i
            gr_out[d, 0, k] = bbr * wr - bbi * wi
            gi_out[d, 0, k] = bbr * wi + bbi * wr
            if k < t:
                lrs.append(l_r)
                lis.append(l_i)
            else:
                a1_ref[d, 0] = wr
                a2_ref[d, 0] = wi
        taps = (_dot_nt(jnp.concatenate(lrs, axis=0), bbr, HIGHEST)
                - _dot_nt(jnp.concatenate(lis, axis=0), bbi, HIGHEST))
        per_dir.append([taps[k * SSM_GROUP:(k + 1) * SSM_GROUP] for k in range(t)])
    kf_, kb_ = per_dir
    for lag in range(-(t - 1), t):
        if lag > 0:
            val = kf_[lag]
        elif lag < 0:
            val = kb_[-lag]
        else:
            val = kf_[0] + kb_[0]
        ktoe_ref[0, lag + t - 1] = val


def _ssm_params(lam_re, lam_im, log_dt, b_re, b_im, c_re, c_im):
    g, p, h, t = SSM_GROUPS, SSM_STATE, SSM_GROUP, SSM_T
    row = lambda a: a.reshape(2, g, 1, p)
    ldt = jnp.broadcast_to(log_dt[:, :, None, None], (2, g, 1, p))
    bt = lambda a: jnp.swapaxes(a, 2, 3)
    vec_spec = pl.BlockSpec((2, 1, 1, p), lambda i: (0, i, 0, 0))
    mat_spec = pl.BlockSpec((2, 1, h, p), lambda i: (0, i, 0, 0))
    pw_spec = pl.BlockSpec((2, 1, t + 1, h, p), lambda i: (0, i, 0, 0, 0))
    pw_shape = jax.ShapeDtypeStruct((2, g, t + 1, h, p), F32)
    a_spec = pl.BlockSpec((2, 1, 1, p), lambda i: (0, i, 0, 0))
    a_shape = jax.ShapeDtypeStruct((2, g, 1, p), F32)
    return pl.pallas_call(
        _ssm_param_kernel,
        grid=(g,),
        in_specs=[vec_spec, vec_spec, vec_spec, mat_spec, mat_spec, mat_spec, mat_spec],
        out_specs=[pl.BlockSpec((1, 2 * t - 1, h, h), lambda i: (i, 0, 0, 0)),
                   pw_spec, pw_spec, pw_spec, pw_spec, a_spec, a_spec],
        out_shape=[jax.ShapeDtypeStruct((g, 2 * t - 1, h, h), F32),
                   pw_shape, pw_shape, pw_shape, pw_shape, a_shape, a_shape],
        compiler_params=_cparams(("arbitrary",)),
    )(row(lam_re), row(lam_im), ldt, bt(b_re), bt(b_im), c_re, c_im)


def _ssm_operators(ktoe, l_r, nl_i, g_r, g_i, a_re, a_im):
    g, p, h, t = SSM_GROUPS, SSM_STATE, SSM_GROUP, SSM_T
    s_idx = jnp.arange(t)
    lag = s_idx[None, :] - s_idx[:, None] + (t - 1)
    mt = ktoe[:, lag]
    mt = mt.transpose(0, 1, 4, 2, 3).reshape(g, t * h, t * h)
    f_f = jnp.concatenate([g_r[0][:, t - 1 - s_idx], g_i[0][:, t - 1 - s_idx]], axis=-1)
    f_b = jnp.concatenate([g_r[1][:, s_idx], g_i[1][:, s_idx]], axis=-1)
    ft = jnp.concatenate([f_f, f_b], axis=-1).reshape(g, t * h, 4 * p)
    r1 = jnp.concatenate([mt, ft], axis=-1).astype(BF16)
    e_f = jnp.concatenate([l_r[0][:, 1 + s_idx], nl_i[0][:, 1 + s_idx]], axis=-1)
    e_b = jnp.concatenate([l_r[1][:, t - s_idx], nl_i[1][:, t - s_idx]], axis=-1)
    r2 = jnp.concatenate([e_f, e_b], axis=-1).reshape(g, t * h, 4 * p)
    r2 = jnp.swapaxes(r2, 1, 2).astype(BF16)
    flat = lambda a: a.reshape(1, g * p)
    return r1, r2, flat(a_re[0]), flat(a_im[0]), flat(a_re[1]), flat(a_im[1])


_LANE_GROUPS = 128 // SSM_GROUP
_ROWS8 = 8


def _to_groups_kernel(su_ref, ug_ref, *, nc):
    t, h = SSM_T, SSM_GROUP
    lane_grp = lax.broadcasted_iota(jnp.int32, (_ROWS8, 128), 1) // h

    def body(rb, carry):
        base = pl.multiple_of(rb * (_ROWS8 * t), _ROWS8 * t)
        rows = pl.ds(pl.multiple_of(rb * _ROWS8, _ROWS8), _ROWS8)
        ps = [su_ref[0, pl.ds(base + s, _ROWS8, stride=t), :] for s in range(t)]
        for gm in range(_LANE_GROUPS):
            for j in range(t * h // 128):
                acc = jnp.zeros((_ROWS8, 128), F32)
                for sm in range(_LANE_GROUPS):
                    shift = ((sm - gm) * h) % 128
                    src = ps[_LANE_GROUPS * j + sm]
                    acc = jnp.where(lane_grp == sm, pltpu.roll(src, shift, 1) if shift else src, acc)
                ug_ref[gm, rows, j * 128:(j + 1) * 128] = acc
        return carry

    lax.fori_loop(0, nc // _ROWS8, body, 0)


def _to_groups(su):
    bsz, nt, w = su.shape
    nc = nt // SSM_T
    return pl.pallas_call(
        functools.partial(_to_groups_kernel, nc=nc),
        grid=(bsz, w // 128),
        in_specs=[pl.BlockSpec((1, nt, 128), lambda b, a: (b, 0, a))],
        out_specs=pl.BlockSpec((_LANE_GROUPS, nc, SSM_T * SSM_GROUP), lambda b, a: (a, b, 0)),
        out_shape=jax.ShapeDtypeStruct((SSM_GROUPS, bsz * nc, SSM_T * SSM_GROUP), F32),
        compiler_params=_cparams(("arbitrary", "arbitrary")),
    )(su)


def _from_groups_kernel(yg_ref, y_ref, *, nc):
    t, h = SSM_T, SSM_GROUP
    lane_grp = lax.broadcasted_iota(jnp.int32, (_ROWS8, 128), 1) // h

    def body(rb, carry):
        base = pl.multiple_of(rb * (_ROWS8 * t), _ROWS8 * t)
        rows = pl.ds(pl.multiple_of(rb * _ROWS8, _ROWS8), _ROWS8)
        for tt in range(t):
            j, sm = divmod(tt, _LANE_GROUPS)
            acc = jnp.zeros((_ROWS8, 128), F32)
            for gm in range(_LANE_GROUPS):
                shift = ((gm - sm) * h) % 128
                src = yg_ref[gm, rows, j * 128:(j + 1) * 128]
                acc = jnp.where(lane_grp == gm, pltpu.roll(src, shift, 1) if shift else src, acc)
            y_ref[0, pl.ds(base + tt, _ROWS8, stride=t), :] = acc
        return carry

    lax.fori_loop(0, nc // _ROWS8, body, 0)


def _from_groups(yg, bsz):
    g, n, th = yg.shape
    nc = n // bsz
    nt = nc * SSM_T
    return pl.pallas_call(
        functools.partial(_from_groups_kernel, nc=nc),
        grid=(bsz, g // _LANE_GROUPS),
        in_specs=[pl.BlockSpec((_LANE_GROUPS, nc, th), lambda b, a: (a, b, 0))],
        out_specs=pl.BlockSpec((1, nt, 128), lambda b, a: (b, 0, a)),
        out_shape=jax.ShapeDtypeStruct((bsz, nt, g * SSM_GROUP), F32),
        compiler_params=_cparams(("arbitrary", "arbitrary")),
    )(yg)


def _ssm_phase1_kernel(u_ref, r1_ref, y_ref, fr_ref, fi_ref, br_ref, bi_ref):
    th = SSM_T * SSM_GROUP
    p = SSM_STATE
    outs = [_dot(u_ref[k].astype(BF16), r1_ref[k]) for k in range(2)]
    for k in range(2):
        y_ref[k] = outs[k][:, 0:th]
    for idx, ref in enumerate((fr_ref, fi_ref, br_ref, bi_ref)):
        lo = th + idx * p
        ref[...] = jnp.concatenate([o[:, lo:lo + p] for o in outs], axis=1)


def _ssm_phase1(ug, r1):
    g, n, th = ug.shape
    p = SSM_STATE
    plane = pl.BlockSpec((n, 2 * p), lambda i: (0, i))
    plane_shape = jax.ShapeDtypeStruct((n, g * p), F32)
    return pl.pallas_call(
        _ssm_phase1_kernel,
        grid=(g // 2,),
        in_specs=[pl.BlockSpec((2, n, th), lambda i: (i, 0, 0)),
                  pl.BlockSpec((2, th, th + 4 * p), lambda i: (i, 0, 0))],
        out_specs=[pl.BlockSpec((2, n, th), lambda i: (i, 0, 0)), plane, plane, plane, plane],
        out_shape=[jax.ShapeDtypeStruct((g, n, th), F32), plane_shape, plane_shape, plane_shape, plane_shape],
        compiler_params=_cparams(("arbitrary",)),
    )(ug, r1)


def _ssm_scan_kernel(dfr_ref, dfi_ref, dbr_ref, dbi_ref, afr_ref, afi_ref, abr_ref, abi_ref,
                     sfr_ref, sfi_ref, sbr_ref, sbi_ref, *, nc, bsz):
    width = dfr_ref.shape[1]

    def run(dr_ref, di_ref, ar_ref, ai_ref, or_ref, oi_ref, chunk_of_step):
        ar = ar_ref[...]
        ai = ai_ref[...]

        def body(i, s):
            sr, si = s
            rows = pl.ds(chunk_of_step(i), bsz, stride=nc)
            or_ref[rows, :] = sr
            oi_ref[rows, :] = si
            return (ar * sr - ai * si + dr_ref[rows, :], ar * si + ai * sr + di_ref[rows, :])

        zero = jnp.zeros((bsz, width), F32)
        lax.fori_loop(0, nc, body, (zero, zero))

    run(dfr_ref, dfi_ref, afr_ref, afi_ref, sfr_ref, sfi_ref, lambda i: i)
    run(dbr_ref, dbi_ref, abr_ref, abi_ref, sbr_ref, sbi_ref,
        lambda i: jnp.where(i < SSM_NCTX, SSM_NCTX - 1 - i, nc + SSM_NCTX - 1 - i))


def _ssm_scan(planes, decays, bsz):
    n, width = planes[0].shape
    blk = pl.BlockSpec((n, 128), lambda i: (0, i))
    arow = pl.BlockSpec((1, 128), lambda i: (0, i))
    shape = jax.ShapeDtypeStruct((n, width), F32)
    return pl.pallas_call(
        functools.partial(_ssm_scan_kernel, nc=n // bsz, bsz=bsz),
        grid=(width // 128,),
        in_specs=[blk] * 4 + [arow] * 4,
        out_specs=[blk] * 4,
        out_shape=[shape] * 4,
        compiler_params=_cparams(("arbitrary",)),
    )(*planes, *decays)


def _ssm_phase2_kernel(y_ref, sfr_ref, sfi_ref, sbr_ref, sbi_ref, r2_ref, o_ref):
    p = SSM_STATE
    for k in range(2):
        acc = y_ref[k]
        for idx, ref in enumerate((sfr_ref, sfi_ref, sbr_ref, sbi_ref)):
            acc = acc + _dot(ref[:, k * p:(k + 1) * p].astype(BF16), r2_ref[k, idx * p:(idx + 1) * p, :])
        o_ref[k] = acc


def _ssm_phase2(yg, states, r2):
    g, n, th = yg.shape
    p = SSM_STATE
    plane = pl.BlockSpec((n, 2 * p), lambda i: (0, i))
    return pl.pallas_call(
        _ssm_phase2_kernel,
        grid=(g // 2,),
        in_specs=[pl.BlockSpec((2, n, th), lambda i: (i, 0, 0)), plane, plane, plane, plane,
                  pl.BlockSpec((2, 4 * p, th), lambda i: (i, 0, 0))],
        out_specs=pl.BlockSpec((2, n, th), lambda i: (i, 0, 0)),
        out_shape=jax.ShapeDtypeStruct((g, n, th), F32),
        compiler_params=_cparams(("arbitrary",)),
    )(yg, *states, r2)


def _ssm_mix(su_all, ops):
    r1, r2 = ops[0], ops[1]
    bsz = su_all.shape[0]
    ug = _to_groups(su_all)
    y1, *planes = _ssm_phase1(ug, r1)
    states = _ssm_scan(planes, ops[2:], bsz)
    return _from_groups(_ssm_phase2(y1, states, r2), bsz)


def _gla_direction(qk, v, z, wg, bg, s_ref, d, o_ref, reverse):
    tb = qk.shape[0]
    c_len = GLA_CHUNK
    x = _dot(z.astype(BF16), wg) + bg
    gate = (jnp.minimum(x, 0.0) - jnp.log(1.0 + jnp.exp(-jnp.abs(x)))) * (1.0 / GLA_TAU)
    ti = lax.broadcasted_iota(jnp.int32, (c_len, c_len), 0)
    si = lax.broadcasted_iota(jnp.int32, (c_len, c_len), 1)
    keep = (si >= ti) if reverse else (si <= ti)
    tri = jnp.where(keep, 1.0, 0.0).astype(F32)
    chunks = range(tb // c_len)
    for c in (reversed(chunks) if reverse else chunks):
        rs = slice(c * c_len, (c + 1) * c_len)
        bcum = _dot(tri, gate[rs], HIGHEST)
        blast = bcum[0:1] if reverse else bcum[c_len - 1:c_len]
        q_in = qk[rs, 0:GLA_QK_W] * (GLA_DK ** -0.5) * jnp.exp(bcum)
        k_c = qk[rs, GLA_QK_W:2 * GLA_QK_W]
        k_in = (k_c * jnp.exp(-bcum)).astype(BF16)
        k_up = (k_c * jnp.exp(blast - bcum)).astype(BF16)
        q_in = q_in.astype(BF16)
        dec = jnp.exp(blast)
        for hd in range(GLA_HEADS):
            ks = slice(hd * GLA_DK, (hd + 1) * GLA_DK)
            vs = slice(hd * GLA_DV, (hd + 1) * GLA_DV)
            vh = v[rs, vs].astype(BF16)
            att = jnp.where(keep, _dot_nt(q_in[:, ks], k_in[:, ks]), 0.0)
            st = s_ref[d, hd]
            o_ref[0, rs, vs] = _dot(att.astype(BF16), vh) + _dot_nt(q_in[:, ks], st.astype(BF16))
            s_ref[d, hd] = st * dec[:, ks] + _dot_tn(vh, k_up[:, ks])


def _gla_kernel(qkf_ref, vf_ref, zf_ref, qkb_ref, vb_ref, zb_ref, wg_ref, bg_ref, of_ref, ob_ref, s_ref):
    @pl.when(pl.program_id(1) == 0)
    def _():
        s_ref[...] = jnp.zeros_like(s_ref)

    _gla_direction(qkf_ref[0], vf_ref[0], zf_ref[0], wg_ref[0], bg_ref[0], s_ref, 0, of_ref, False)
    _gla_direction(qkb_ref[0], vb_ref[0], zb_ref[0], wg_ref[1], bg_ref[1], s_ref, 1, ob_ref, True)


def _gla(gqk, gv, gz, wg_pad, bg):
    bsz, nt, _ = gqk.shape
    tb = TOKEN_TILE
    tiles = nt // tb
    fwd = lambda b, i: (b, i, 0)
    bwd = lambda b, i: (b, jnp.where(i == 0, 0, tiles - i), 0)
    spec = lambda width, im: pl.BlockSpec((1, tb, width), im)
    return pl.pallas_call(
        _gla_kernel,
        grid=(bsz, tiles),
        in_specs=[spec(2 * GLA_QK_W, fwd), spec(GLA_V_W, fwd), spec(GZ_PAD, fwd),
                  spec(2 * GLA_QK_W, bwd), spec(GLA_V_W, bwd), spec(GZ_PAD, bwd),
                  pl.BlockSpec((2, GZ_PAD, GLA_QK_W), lambda b, i: (0, 0, 0)),
                  pl.BlockSpec((2, 1, GLA_QK_W), lambda b, i: (0, 0, 0))],
        out_specs=[spec(GLA_V_W, fwd), spec(GLA_V_W, bwd)],
        out_shape=[jax.ShapeDtypeStruct((bsz, nt, GLA_V_W), F32), jax.ShapeDtypeStruct((bsz, nt, GLA_V_W), F32)],
        scratch_shapes=[pltpu.VMEM((2, GLA_HEADS, GLA_DV, GLA_DK), F32)],
        compiler_params=_cparams(("arbitrary", "arbitrary")),
    )(gqk, gv, gz, gqk, gv, gz, wg_pad, bg)


def _outproj_kernel(x_ref, m_ref, att_ref, y_ref, u_ref, of_ref, ob_ref, r_ref,
                    ssmd_ref, wglu_ref, bglu_ref, gng_ref, wo_ref, ln1g_ref, ln1b_ref, router_ref,
                    x1_ref, h2_ref, aff_ref):
    nb, tm, d = x_ref.shape
    rows = nb * tm
    flat = lambda ref: ref[...].reshape(rows, ref.shape[-1])
    m = m_ref[0]
    zin = flat(y_ref) + ssmd_ref[...] * flat(u_ref)
    z = 0.5 * zin * (1.0 + jnp.tanh(math.sqrt(2.0 / math.pi) * (zin + 0.044715 * (zin * zin * zin))))
    ssm = z * _sigmoid(_dot(z.astype(BF16), wglu_ref[...]) + bglu_ref[...])
    proj = _dot(flat(att_ref), wo_ref[0:ATT_Q_W, :]) + _dot(ssm.astype(BF16), wo_ref[ATT_Q_W:ATT_Q_W + SSM_WIDTH, :])
    o = flat(of_ref) + flat(ob_ref)
    r = flat(r_ref)
    gng = gng_ref[...]
    base = ATT_Q_W + SSM_WIDTH
    for hd in range(GLA_HEADS):
        vs = slice(hd * GLA_DV, (hd + 1) * GLA_DV)
        oh = o[:, vs]
        rh = r[:, vs]
        oh = oh * lax.rsqrt(jnp.mean(oh * oh, axis=-1, keepdims=True) + LN_EPS) * gng
        gla = oh * (rh * _sigmoid(rh))
        proj = proj + _dot(gla.astype(BF16), wo_ref[base + hd * GLA_DV:base + (hd + 1) * GLA_DV, :])
    proj = proj.reshape(nb, tm, d)
    x1 = _ln(ALPHA * x_ref[...] + m[:, 2:3] * proj) * ln1g_ref[...] + ln1b_ref[...]
    x1_ref[...] = x1
    h2 = (_ln(x1) * (1.0 + m[:, 4:5]) + m[:, 3:4]).reshape(rows, d)
    h_hi = h2.astype(BF16)
    h2_ref[...] = h_hi.reshape(nb, tm, d)
    h_lo = (h2 - h_hi.astype(F32)).astype(BF16)
    l_hi = _dot(h_hi, router_ref[...])
    logits = l_hi + pltpu.roll(l_hi, 128 - N_EXPERTS, 1) + _dot(h_lo, router_ref[...])
    col = lax.broadcasted_iota(jnp.int32, logits.shape, 1)
    logits = jnp.where(col < N_EXPERTS, logits, NEG_INF)
    e = jnp.exp(logits - jnp.max(logits, axis=-1, keepdims=True))
    aff_ref[...] = (e / jnp.sum(e, axis=-1, keepdims=True)).reshape(nb, tm, 128)


def _outproj(x_all, mods, att, y, su, o_f, o_b, gr, ssm_d, w_glu, b_glu, gn_g, w_out, ln1_g, ln1_b, router_split,
             with_ctx):
    bsz, nt, d = x_all.shape
    tm = TOKEN_TILE
    nb = BATCH_TILE
    t0 = 0 if with_ctx else CTX_LEN // tm
    tiles = nt // tm - t0
    att_t0 = 0 if with_ctx else -t0
    tok_in = lambda width: pl.BlockSpec((nb, tm, width), lambda b, i: (b, i + t0, 0))
    tok_out = lambda width: pl.BlockSpec((nb, tm, width), lambda b, i: (b, i, 0))
    const = lambda shape: pl.BlockSpec(shape, lambda b, i: tuple(0 for _ in shape))
    once = lambda shape: pl.BlockSpec(shape, lambda b, i: tuple(0 for _ in shape), pipeline_mode=pl.Buffered(1))
    shp = lambda width, dt: jax.ShapeDtypeStruct((bsz, tiles * tm, width), dt)
    return pl.pallas_call(
        _outproj_kernel,
        grid=(bsz // nb, tiles),
        in_specs=[
            tok_in(d),
            pl.BlockSpec((1, nb, N_MOD, d), lambda b, i: (jnp.minimum(i + t0, 1), b, 0, 0)),
            pl.BlockSpec((nb, tm, ATT_Q_W), lambda b, i: (b, i + t0 + att_t0, 0)),
            tok_in(SSM_WIDTH), tok_in(SSM_WIDTH), tok_in(GLA_V_W), tok_in(GLA_V_W), tok_in(GLA_V_W),
            const((1, SSM_WIDTH)), const((SSM_WIDTH, SSM_WIDTH)), const((1, SSM_WIDTH)), const((1, GLA_DV)),
            once((d, d)), const((1, d)), const((1, d)), const((d, 128)),
        ],
        out_specs=[tok_out(d), tok_out(d), tok_out(128)],
        out_shape=[shp(d, F32), shp(d, BF16), shp(128, F32)],
        compiler_params=_cparams(("arbitrary", "arbitrary")),
    )(x_all, mods, att, y, su, o_f, o_b, gr, ssm_d, w_glu, b_glu, gn_g, w_out, ln1_g, ln1_b, router_split)


def _expert_kernel(x_ref, g_ref, wg_ref, wu_ref, wd_ref, o_ref, hid_ref, *, nf, tf):
    s = pl.program_id(1)

    @pl.when(s < nf)
    def _():
        x = x_ref[0]
        a = _dot(x, wg_ref[0, 0].astype(BF16))
        u = _dot(x, wu_ref[0, 0].astype(BF16))
        col = pl.multiple_of(s * tf, tf)
        hid_ref[:, pl.ds(col, tf)] = ((a * _sigmoid(a)) * u).astype(BF16)

    @pl.when(s >= nf)
    def _():
        o_ref[0] = (_dot(hid_ref[...], wd_ref[0, 0].astype(BF16)) * g_ref[0]).astype(BF16)


def _experts(xe, ge, w_gate, w_up, w_down, layer):
    e, r, d = xe.shape
    ff = w_gate.shape[3]
    tf = 256
    nf = ff // tf
    nd = d // tf
    up = lambda i, s: (layer, i, 0, jnp.minimum(s, nf - 1))
    down = lambda i, s: (layer, i, 0, jnp.maximum(s - nf, 0))
    return pl.pallas_call(
        functools.partial(_expert_kernel, nf=nf, tf=tf),
        grid=(e, nf + nd),
        in_specs=[pl.BlockSpec((1, r, d), lambda i, s: (i, 0, 0), pipeline_mode=pl.Buffered(1)),
                  pl.BlockSpec((1, r, 1), lambda i, s: (i, 0, 0)),
                  pl.BlockSpec((1, 1, d, tf), up),
                  pl.BlockSpec((1, 1, d, tf), up),
                  pl.BlockSpec((1, 1, ff, tf), down)],
        out_specs=pl.BlockSpec((1, r, tf), lambda i, s: (i, 0, jnp.maximum(s - nf, 0))),
        out_shape=jax.ShapeDtypeStruct((e, r, d), BF16),
        scratch_shapes=[pltpu.VMEM((r, ff), BF16)],
        compiler_params=_cparams(("arbitrary", "arbitrary")),
    )(xe, ge, w_gate, w_up, w_down)


COMBINE_WIN = 64
COMBINE_ALIGN = 16


def _combine_kernel(lo_ref, hi_ref, x1_ref, pos_ref, m_ref, g_ref, b_ref, ye_hbm, o_ref, buf, sem,
                    *, tiles, nsteps, rtot):
    n_exp = N_EXPERTS
    win = COMBINE_WIN
    tm = x1_ref.shape[1]
    step = pl.program_id(0) * tiles + pl.program_id(1)
    slot = step % 2
    extra = 2

    def first_row(st, e, k):
        u = (lo_ref[st * n_exp + e] // COMBINE_ALIGN) * COMBINE_ALIGN + k * win
        return u, pl.multiple_of(jnp.minimum(u, rtot - win), COMBINE_ALIGN)

    def copies(st, k, dst):
        return [pltpu.make_async_copy(ye_hbm.at[pl.ds(first_row(st, e, k)[1], win), :], buf.at[dst, e], sem.at[dst])
                for e in range(n_exp)]

    @pl.when(step == 0)
    def _():
        for cp in copies(step, 0, slot):
            cp.start()

    @pl.when(step + 1 < nsteps)
    def _():
        for cp in copies(step + 1, 0, 1 - slot):
            cp.start()

    pos = pos_ref[0]
    lane = lax.broadcasted_iota(jnp.int32, (tm, 128), 1)
    left = lane < win

    def spread(k, src):
        parts = []
        for pair in range(n_exp // 2):
            ea, eb = 2 * pair, 2 * pair + 1
            ua, sa = first_row(step, ea, k)
            ub, sb = first_row(step, eb, k)
            row = jnp.where(left, pos[:, ea:ea + 1], pos[:, eb:eb + 1])
            hit = jnp.logical_and(row - jnp.where(left, sa, sb - win) == lane, row >= jnp.where(left, ua, ub))
            parts.append(jnp.where(hit, 1.0, 0.0).astype(BF16))
        onehot = jnp.concatenate(parts, axis=1)
        return _dot(onehot, buf[src].reshape(n_exp * win, buf.shape[-1]))

    for cp in copies(step, 0, slot):
        cp.wait()
    acc = spread(0, slot)

    rounds = jnp.int32(1)
    for e in range(n_exp):
        span = hi_ref[step * n_exp + e] - (lo_ref[step * n_exp + e] // COMBINE_ALIGN) * COMBINE_ALIGN
        rounds = jnp.maximum(rounds, (span + win - 1) // win)

    def more(k, acc):
        for cp in copies(step, k, extra):
            cp.start()
        for cp in copies(step, k, extra):
            cp.wait()
        return acc + spread(k, extra)

    acc = lax.fori_loop(1, rounds, more, acc)
    m = m_ref[0, 0]
    o_ref[0] = _ln(ALPHA * x1_ref[0] + m[5:6] * acc) * g_ref[...] + b_ref[...]


def _combine(lo, hi, x1, pos, mods, ln2_g, ln2_b, ye, with_ctx):
    bsz, nt, d = x1.shape
    tm = TOKEN_TILE
    tiles = nt // tm
    kind0 = 0 if with_ctx else 1
    tok = lambda width: pl.BlockSpec((1, tm, width), lambda b, i, lo, hi: (b, i, 0))
    vec = pl.BlockSpec((1, d), lambda b, i, lo, hi: (0, 0))
    return pl.pallas_call(
        functools.partial(_combine_kernel, tiles=tiles, nsteps=bsz * tiles, rtot=ye.shape[0]),
        grid_spec=pltpu.PrefetchScalarGridSpec(
            num_scalar_prefetch=2,
            grid=(bsz, tiles),
            in_specs=[tok(d), tok(128),
                      pl.BlockSpec((1, 1, N_MOD, d), lambda b, i, lo, hi: (jnp.minimum(i + kind0, 1), b, 0, 0)),
                      vec, vec, pl.BlockSpec(memory_space=pl.ANY)],
            out_specs=tok(d),
            scratch_shapes=[pltpu.VMEM((3, N_EXPERTS, COMBINE_WIN, d), BF16), pltpu.SemaphoreType.DMA((3,))],
        ),
        out_shape=jax.ShapeDtypeStruct((bsz, nt, d), F32),
        compiler_params=_cparams(("arbitrary", "arbitrary")),
    )(lo, hi, x1, pos, mods, ln2_g, ln2_b, ye)


def _route(aff, cap):
    gate, idx = lax.top_k(jnp.swapaxes(aff, 1, 2), cap)
    return gate, idx


def _rope_tables(seq):
    half = HEAD_DIM // 2
    nf = half // 2
    inv = ROPE_BASE ** (-jnp.arange(nf, dtype=F32) / nf)
    pos = jnp.arange(seq)
    ang_r = (pos // GRID_W).astype(F32)[:, None] * inv
    ang_c = (pos % GRID_W).astype(F32)[:, None] * inv
    zeros = jnp.zeros_like(ang_r)
    cos = jnp.concatenate([jnp.cos(ang_r), jnp.cos(ang_r), jnp.cos(ang_c), jnp.cos(ang_c)], axis=-1)
    sa = jnp.concatenate([-jnp.sin(ang_r), zeros, -jnp.sin(ang_c), zeros], axis=-1)
    sb = jnp.concatenate([zeros, jnp.sin(ang_r), zeros, jnp.sin(ang_c)], axis=-1)
    ctx = lambda fill: jnp.full((CTX_LEN, HEAD_DIM), fill, F32)
    return (jnp.concatenate([ctx(1.0), cos], axis=0), jnp.concatenate([ctx(0.0), sa], axis=0),
            jnp.concatenate([ctx(0.0), sb], axis=0))


def kernel(x, c, ctx, c_ctx, w_ada, b_ada, w_in, attn_sink, ssm_lam_re, ssm_lam_im, ssm_log_dt,
           ssm_b_re, ssm_b_im, ssm_c_re, ssm_c_im, ssm_d, ssm_w_glu, ssm_b_glu, gla_w_gate,
           gla_b_gate, gla_norm_g, w_out, ln1_g, ln1_b, ln2_g, ln2_b, router, exp_w_gate,
           exp_w_up, exp_w_down):
    bsz, seq, d = x.shape
    n_exp = router.shape[-1]
    cc = jnp.zeros((8, d), F32).at[:bsz].set(c).at[bsz].set(c_ctx)
    mod_all = _ada(cc, w_ada, b_ada)
    cos_t, sa_t, sb_t = _rope_tables(seq)
    x_all = jnp.concatenate([ctx, x], axis=1)
    bidx = jnp.arange(bsz)[None, :, None]

    for l in range(DEPTH):
        with_ctx = l < DEPTH - 1
        m = mod_all[l].reshape(8, N_MOD, d)
        mods = jnp.stack([jnp.broadcast_to(m[bsz], (bsz, N_MOD, d)), m[:bsz]], axis=0)
        w_in_p = jnp.pad(w_in[l], ((0, 0), (0, N_IN_PAD - N_IN))).astype(BF16)
        q, k, v, su, gqk, gv, gr, gz = _inproj(x_all, mods, cos_t, sa_t, sb_t, w_in_p)

        att = _attention(attn_sink[l], q, k, v, with_ctx)

        ops = _ssm_operators(*_ssm_params(ssm_lam_re[l], ssm_lam_im[l], ssm_log_dt[l], ssm_b_re[l],
                                          ssm_b_im[l], ssm_c_re[l], ssm_c_im[l]))
        y = _ssm_mix(su, ops)

        wg_pad = jnp.zeros((2, GZ_PAD, GLA_QK_W), F32)
        wg_pad = wg_pad.at[0, 0:GLA_RANK].set(gla_w_gate[l, 0]).at[1, GLA_RANK:2 * GLA_RANK].set(gla_w_gate[l, 1])
        o_f, o_b = _gla(gqk, gv, gz, wg_pad.astype(BF16), gla_b_gate[l][:, None, :])

        r_hi = router[l].astype(BF16)
        r_lo = (router[l] - r_hi.astype(F32)).astype(BF16)
        router_split = jnp.pad(jnp.concatenate([r_hi, r_lo], axis=1), ((0, 0), (0, 128 - 2 * n_exp)))
        x1, h2, aff = _outproj(x_all, mods, att, y, su, o_f, o_b, gr,
                               ssm_d[l][None], ssm_w_glu[l].astype(BF16), ssm_b_glu[l][None], gla_norm_g[l][None],
                               w_out[l].astype(BF16), ln1_g[l][None], ln1_b[l][None], router_split, with_ctx)

        t0 = CTX_LEN if with_ctx else 0
        cap = CAPACITY_FACTOR * seq // n_exp
        gate, idx = _route(aff[:, t0:, :n_exp], cap)
        idx = idx + t0
        if with_ctx:
            cap_c = CAPACITY_FACTOR * CTX_LEN // n_exp
            gate_c, idx_c = _route(aff[:, :CTX_LEN, :n_exp], cap_c)
            gate = jnp.concatenate([gate, gate_c], axis=-1)
            idx = jnp.concatenate([idx, idx_c], axis=-1)
        idx, gate = lax.sort((idx, gate), dimension=-1, num_keys=1)
        idx_e = jnp.swapaxes(idx, 0, 1)
        gate_e = jnp.swapaxes(gate, 0, 1)
        slots = idx_e.shape[-1]
        rows_e = bsz * slots
        xe = h2[bidx, idx_e].reshape(n_exp, rows_e, d)
        ye = _experts(xe, gate_e.reshape(n_exp, rows_e, 1), exp_w_gate, exp_w_up, exp_w_down, l)
        nt_out = x1.shape[1]
        row = (jnp.arange(n_exp)[:, None, None] * rows_e + jnp.arange(bsz)[None, :, None] * slots
               + jnp.arange(slots)[None, None, :]).astype(jnp.int32)
        pos = jnp.full((bsz, nt_out, 128), -1, jnp.int32).at[bidx, idx_e, jnp.arange(n_exp)[:, None, None]].set(row)
        bounds = jnp.arange(nt_out // TOKEN_TILE + 1) * TOKEN_TILE
        cnt = jnp.sum(idx_e[..., None] < bounds, axis=2).astype(jnp.int32)
        base = row[:, :, 0:1]
        lo = jnp.transpose(base + cnt[:, :, :-1], (1, 2, 0)).reshape(-1)
        hi = jnp.transpose(base + cnt[:, :, 1:], (1, 2, 0)).reshape(-1)
        x_all = _combine(lo, hi, x1, pos, mods, ln2_g[l][None], ln2_b[l][None],
                         ye.reshape(n_exp * rows_e, d), with_ctx)
    return x_all
```

```python
import functools
import math

import jax
import jax.numpy as jnp
from jax import lax
from jax.experimental import pallas as pl
from jax.experimental.pallas import tpu as pltpu

F32 = jnp.float32
BF16 = jnp.bfloat16
HIGHEST = lax.Precision.HIGHEST

D_MODEL = 2048
DEPTH = 2
GRID_W = 64
CTX_LEN = 256
ATT_HEADS = 8
ATT_KV_HEADS = 2
HEAD_DIM = 128
ATT_BLOCK = 128
ATT_QBLOCKS = 2
ROPE_BASE = 10000.0
SSM_WIDTH = 512
SSM_GROUP = 16
SSM_GROUPS = 32
SSM_STATE = 64
GLA_HEADS = 4
GLA_DK = 64
GLA_DV = 128
GLA_RANK = 16
GLA_TAU = 16.0
GLA_CHUNK = 64
N_EXPERTS = 16
EXPERT_FF = 2048
CAPACITY_FACTOR = 2
N_MOD = 6
ALPHA = (2 * DEPTH) ** 0.25
LN_EPS = 1e-6
NEG_INF = -1e30

ATT_Q_W = ATT_HEADS * HEAD_DIM
ATT_KV_W = ATT_KV_HEADS * HEAD_DIM
GLA_QK_W = GLA_HEADS * GLA_DK
GLA_V_W = GLA_HEADS * GLA_DV
N_IN = 3616
N_IN_PAD = 3712
GZ_PAD = N_IN_PAD - 3584

TOKEN_TILE = 256
BATCH_TILE = 2
SSM_T = 32
SSM_NCTX = CTX_LEN // SSM_T
VMEM_LIMIT = 56 * 1024 * 1024


def _cparams(sem, vmem=VMEM_LIMIT):
    return pltpu.CompilerParams(dimension_semantics=sem, vmem_limit_bytes=vmem)


def _dot(a, b, precision=None):
    return jnp.dot(a, b, preferred_element_type=F32, precision=precision)


def _dot_nt(a, b, precision=None):
    return lax.dot_general(a, b, (((1,), (1,)), ((), ())), preferred_element_type=F32, precision=precision)


def _dot_tn(a, b, precision=None):
    return lax.dot_general(a, b, (((0,), (0,)), ((), ())), preferred_element_type=F32, precision=precision)


def _sigmoid(x):
    return 1.0 / (1.0 + jnp.exp(-x))


def _ln(x):
    mu = jnp.mean(x, axis=-1, keepdims=True)
    xc = x - mu
    var = jnp.mean(xc * xc, axis=-1, keepdims=True)
    return xc * lax.rsqrt(var + LN_EPS)


def _ada_kernel(c_ref, w_ref, b_ref, o_ref):
    c = c_ref[...]
    s = c * _sigmoid(c)
    o_ref[0] = _dot(s, w_ref[0], HIGHEST) + b_ref[0]


def _ada(cc, w_ada, b_ada):
    depth, d, n = w_ada.shape
    tn = 1024
    return pl.pallas_call(
        _ada_kernel,
        grid=(depth, n // tn),
        in_specs=[
            pl.BlockSpec((8, d), lambda l, j: (0, 0)),
            pl.BlockSpec((1, d, tn), lambda l, j: (l, 0, j)),
            pl.BlockSpec((1, 1, tn), lambda l, j: (l, 0, j)),
        ],
        out_specs=pl.BlockSpec((1, 8, tn), lambda l, j: (l, 0, j)),
        out_shape=jax.ShapeDtypeStruct((depth, 8, n), F32),
        compiler_params=_cparams(("arbitrary", "arbitrary")),
    )(cc, w_ada, b_ada.reshape(depth, 1, n))


def _inproj_kernel(x_ref, m_ref, cos_ref, sa_ref, sb_ref, w_ref,
                   q_ref, k_ref, v_ref, su_ref, gqk_ref, gv_ref, gr_ref, gz_ref):
    nb, tm, d = x_ref.shape
    rows = nb * tm
    m = m_ref[0]
    h = _ln(x_ref[...]) * (1.0 + m[:, 1:2]) + m[:, 0:1]
    h = h.reshape(rows, d).astype(BF16)
    tile = lambda ref: jnp.concatenate([ref[...]] * nb, axis=0)
    cos = tile(cos_ref)
    sa = tile(sa_ref)
    sb = tile(sb_ref)

    def rope(t):
        return t * cos + pltpu.roll(t, 96, 1) * sa + pltpu.roll(t, 32, 1) * sb

    def put(ref, val, sl=slice(None)):
        ref[:, :, sl] = val.reshape(nb, tm, val.shape[-1])

    scale = HEAD_DIM ** -0.5
    q = _dot(h, w_ref[:, 0:ATT_Q_W])
    for hd in range(ATT_HEADS):
        sl = slice(hd * HEAD_DIM, (hd + 1) * HEAD_DIM)
        put(q_ref, (rope(q[:, sl]) * scale).astype(BF16), sl)
    kk = _dot(h, w_ref[:, 1024:1280])
    for hd in range(ATT_KV_HEADS):
        sl = slice(hd * HEAD_DIM, (hd + 1) * HEAD_DIM)
        put(k_ref, rope(kk[:, sl]).astype(BF16), sl)
    put(v_ref, _dot(h, w_ref[:, 1280:1536]).astype(BF16))
    put(su_ref, _dot(h, w_ref[:, 1536:2048]))
    put(gqk_ref, _dot(h, w_ref[:, 2048:2560]))
    put(gv_ref, _dot(h, w_ref[:, 2560:3072]))
    put(gr_ref, _dot(h, w_ref[:, 3072:3584]))
    put(gz_ref, _dot(h, w_ref[:, 3584:N_IN_PAD]))


def _inproj(x_all, mods, cos_t, sa_t, sb_t, w_in_bf16):
    bsz, nt, d = x_all.shape
    tm = TOKEN_TILE
    nb = BATCH_TILE
    tiles = nt // tm
    tok = lambda width: pl.BlockSpec((nb, tm, width), lambda b, i: (b, i, 0))
    tab = pl.BlockSpec((tm, HEAD_DIM), lambda b, i: (i, 0))
    shp = lambda width, dt: jax.ShapeDtypeStruct((bsz, nt, width), dt)
    return pl.pallas_call(
        _inproj_kernel,
        grid=(bsz // nb, tiles),
        in_specs=[
            tok(d),
            pl.BlockSpec((1, nb, N_MOD, d), lambda b, i: (jnp.minimum(i, 1), b, 0, 0)),
            tab, tab, tab,
            pl.BlockSpec((d, N_IN_PAD), lambda b, i: (0, 0), pipeline_mode=pl.Buffered(1)),
        ],
        out_specs=[tok(ATT_Q_W), tok(ATT_KV_W), tok(ATT_KV_W), tok(SSM_WIDTH),
                   tok(2 * GLA_QK_W), tok(GLA_V_W), tok(GLA_V_W), tok(GZ_PAD)],
        out_shape=[shp(ATT_Q_W, BF16), shp(ATT_KV_W, BF16), shp(ATT_KV_W, BF16), shp(SSM_WIDTH, F32),
                   shp(2 * GLA_QK_W, F32), shp(GLA_V_W, F32), shp(GLA_V_W, F32), shp(GZ_PAD, F32)],
        compiler_params=_cparams(("arbitrary", "arbitrary")),
    )(x_all, mods, cos_t, sa_t, sb_t, w_in_bf16)


def _attn_kernel(sink_ref, q_ref, k_ref, v_ref, o_ref, *, blk0, nblk):
    for sub in range(ATT_QBLOCKS):
        _attn_block(sink_ref, q_ref, k_ref, v_ref, o_ref, sub, pl.program_id(1) * ATT_QBLOCKS + sub + blk0, nblk)


def _attn_block(sink_ref, q_ref, k_ref, v_ref, o_ref, sub, qb, nblk):
    w = ATT_BLOCK
    qrows = slice(sub * w, (sub + 1) * w)
    cb = CTX_LEN // w
    n = qb - cb
    last = nblk - 1
    prev_b = jnp.maximum(qb - 1, cb)
    cur_b = jnp.maximum(qb, cb)
    next_b = jnp.clip(qb + 1, cb, last)
    ok_prev = n >= 1
    ok_cur = n >= 0
    ok_next = jnp.logical_and(n >= 0, qb < last)
    g = ATT_HEADS // ATT_KV_HEADS
    rows = g * w
    ii = lax.broadcasted_iota(jnp.int32, (rows, w), 0) % w
    jj = lax.broadcasted_iota(jnp.int32, (rows, w), 1)
    m_prev = jnp.logical_and(jj >= ii, ok_prev)
    m_cur = jnp.logical_and(jj >= 0, ok_cur)
    m_next = jnp.logical_and(jj <= ii, ok_next)
    rowg = lax.broadcasted_iota(jnp.int32, (rows, 1), 0) // w

    def rows_of(ref, blk, sl):
        return ref[0, pl.ds(pl.multiple_of(blk * w, w), w), sl]

    for kvh in range(ATT_KV_HEADS):
        sl = slice(kvh * HEAD_DIM, (kvh + 1) * HEAD_DIM)
        qs = jnp.concatenate(
            [q_ref[0, qrows, (kvh * g + a) * HEAD_DIM:(kvh * g + a + 1) * HEAD_DIM] for a in range(g)], axis=0)
        sink = jnp.zeros((rows, 1), F32)
        for a in range(g):
            sink = jnp.where(rowg == a, sink_ref[kvh * g + a], sink)
        s_p = jnp.where(m_prev, _dot_nt(qs, rows_of(k_ref, prev_b, sl)), NEG_INF)
        s_c = jnp.where(m_cur, _dot_nt(qs, rows_of(k_ref, cur_b, sl)), NEG_INF)
        s_n = jnp.where(m_next, _dot_nt(qs, rows_of(k_ref, next_b, sl)), NEG_INF)
        s_x = _dot_nt(qs, k_ref[0, 0:CTX_LEN, sl])
        mx = jnp.maximum(jnp.maximum(jnp.max(s_p, axis=-1, keepdims=True), jnp.max(s_c, axis=-1, keepdims=True)),
                         jnp.maximum(jnp.max(s_n, axis=-1, keepdims=True), jnp.max(s_x, axis=-1, keepdims=True)))
        mx = jnp.maximum(mx, sink)
        p_p = jnp.exp(s_p - mx)
        p_c = jnp.exp(s_c - mx)
        p_n = jnp.exp(s_n - mx)
        p_x = jnp.exp(s_x - mx)
        den = (jnp.sum(p_p, axis=-1, keepdims=True) + jnp.sum(p_c, axis=-1, keepdims=True)
               + jnp.sum(p_n, axis=-1, keepdims=True) + jnp.sum(p_x, axis=-1, keepdims=True)
               + jnp.exp(sink - mx))
        o = (_dot(p_p.astype(BF16), rows_of(v_ref, prev_b, sl)) + _dot(p_c.astype(BF16), rows_of(v_ref, cur_b, sl))
             + _dot(p_n.astype(BF16), rows_of(v_ref, next_b, sl)) + _dot(p_x.astype(BF16), v_ref[0, 0:CTX_LEN, sl]))
        o = o / den
        for a in range(g):
            hs = slice((kvh * g + a) * HEAD_DIM, (kvh * g + a + 1) * HEAD_DIM)
            o_ref[0, qrows, hs] = o[a * w:(a + 1) * w].astype(BF16)


def _attention(sink, q_all, k_all, v_all, with_ctx):
    bsz, nt, _ = q_all.shape
    w = ATT_BLOCK
    nblk = nt // w
    blk0 = 0 if with_ctx else CTX_LEN // w
    wq = w * ATT_QBLOCKS
    step0 = blk0 // ATT_QBLOCKS
    return pl.pallas_call(
        functools.partial(_attn_kernel, blk0=blk0, nblk=nblk),
        grid=(bsz, (nblk - blk0) // ATT_QBLOCKS),
        in_specs=[
            pl.BlockSpec(memory_space=pltpu.SMEM),
            pl.BlockSpec((1, wq, ATT_Q_W), lambda b, j: (b, j + step0, 0)),
            pl.BlockSpec((1, nt, ATT_KV_W), lambda b, j: (b, 0, 0)),
            pl.BlockSpec((1, nt, ATT_KV_W), lambda b, j: (b, 0, 0)),
        ],
        out_specs=pl.BlockSpec((1, wq, ATT_Q_W), lambda b, j: (b, j, 0)),
        out_shape=jax.ShapeDtypeStruct((bsz, (nblk - blk0) * w, ATT_Q_W), BF16),
        compiler_params=_cparams(("arbitrary", "arbitrary")),
    )(sink, q_all, k_all, v_all)


def _ssm_param_kernel(lr_ref, li_ref, ldt_ref, bre_ref, bim_ref, cre_ref, cim_ref,
                      ktoe_ref, lr_out, nli_out, gr_out, gi_out, a1_ref, a2_ref):
    t = SSM_T
    hp = (SSM_GROUP, SSM_STATE)
    per_dir = []
    for d in range(2):
        lam_r = lr_ref[d, 0]
        lam_i = li_ref[d, 0]
        dt = jnp.exp(ldt_ref[d, 0])
        ldr = lam_r * dt
        ldi = lam_i * dt
        mag = jnp.exp(ldr)
        lbr = mag * jnp.cos(ldi)
        lbi = mag * jnp.sin(ldi)
        den = lam_r * lam_r + lam_i * lam_i
        nr = lbr - 1.0
        cf_r = (nr * lam_r + lbi * lam_i) / den
        cf_i = (lbi * lam_r - nr * lam_i) / den
        br = bre_ref[d, 0]
        bi = bim_ref[d, 0]
        bbr = cf_r * br - cf_i * bi
        bbi = cf_r * bi + cf_i * br
        cr = cre_ref[d, 0]
        ci = cim_ref[d, 0]
        lrs, lis = [], []
        kcol = lax.broadcasted_iota(jnp.int32, (t + _ROWS8, 1), 0).astype(F32)
        mk = jnp.exp(kcol * ldr)
        wr_all = mk * jnp.cos(kcol * ldi)
        wi_all = mk * jnp.sin(kcol * ldi)
        for k in range(t + 1):
            wr = wr_all[k:k + 1]
            wi = wi_all[k:k + 1]
            l_r = cr * wr - ci * wi
            l_i = cr * wi + ci * wr
            lr_out[d, 0, k] = l_r
            nli_out[d, 0, k] = -l_i
            gr_out[d, 0, k] = bbr * wr - bbi * wi
            gi_out[d, 0, k] = bbr * wi + bbi * wr
            if k < t:
                lrs.append(l_r)
                lis.append(l_i)
            else:
                a1_ref[d, 0] = wr
                a2_ref[d, 0] = wi
        taps = (_dot_nt(jnp.concatenate(lrs, axis=0), bbr, HIGHEST)
                - _dot_nt(jnp.concatenate(lis, axis=0), bbi, HIGHEST))
        per_dir.append([taps[k * SSM_GROUP:(k + 1) * SSM_GROUP] for k in range(t)])
    kf_, kb_ = per_dir
    for lag in range(-(t - 1), t):
        if lag > 0:
            val = kf_[lag]
        elif lag < 0:
            val = kb_[-lag]
        else:
            val = kf_[0] + kb_[0]
        ktoe_ref[0, lag + t - 1] = val


def _ssm_params(lam_re, lam_im, log_dt, b_re, b_im, c_re, c_im):
    g, p, h, t = SSM_GROUPS, SSM_STATE, SSM_GROUP, SSM_T
    row = lambda a: a.reshape(2, g, 1, p)
    ldt = jnp.broadcast_to(log_dt[:, :, None, None], (2, g, 1, p))
    bt = lambda a: jnp.swapaxes(a, 2, 3)
    vec_spec = pl.BlockSpec((2, 1, 1, p), lambda i: (0, i, 0, 0))
    mat_spec = pl.BlockSpec((2, 1, h, p), lambda i: (0, i, 0, 0))
    pw_spec = pl.BlockSpec((2, 1, t + 1, h, p), lambda i: (0, i, 0, 0, 0))
    pw_shape = jax.ShapeDtypeStruct((2, g, t + 1, h, p), F32)
    a_spec = pl.BlockSpec((2, 1, 1, p), lambda i: (0, i, 0, 0))
    a_shape = jax.ShapeDtypeStruct((2, g, 1, p), F32)
    return pl.pallas_call(
        _ssm_param_kernel,
        grid=(g,),
        in_specs=[vec_spec, vec_spec, vec_spec, mat_spec, mat_spec, mat_spec, mat_spec],
        out_specs=[pl.BlockSpec((1, 2 * t - 1, h, h), lambda i: (i, 0, 0, 0)),
                   pw_spec, pw_spec, pw_spec, pw_spec, a_spec, a_spec],
        out_shape=[jax.ShapeDtypeStruct((g, 2 * t - 1, h, h), F32),
                   pw_shape, pw_shape, pw_shape, pw_shape, a_shape, a_shape],
        compiler_params=_cparams(("arbitrary",)),
    )(row(lam_re), row(lam_im), ldt, bt(b_re), bt(b_im), c_re, c_im)


def _ssm_operators(ktoe, l_r, nl_i, g_r, g_i, a_re, a_im):
    g, p, h, t = SSM_GROUPS, SSM_STATE, SSM_GROUP, SSM_T
    s_idx = jnp.arange(t)
    lag = s_idx[None, :] - s_idx[:, None] + (t - 1)
    mt = ktoe[:, lag]
    mt = mt.transpose(0, 1, 4, 2, 3).reshape(g, t * h, t * h)
    f_f = jnp.concatenate([g_r[0][:, t - 1 - s_idx], g_i[0][:, t - 1 - s_idx]], axis=-1)
    f_b = jnp.concatenate([g_r[1][:, s_idx], g_i[1][:, s_idx]], axis=-1)
    ft = jnp.concatenate([f_f, f_b], axis=-1).reshape(g, t * h, 4 * p)
    r1 = jnp.concatenate([mt, ft], axis=-1).astype(BF16)
    e_f = jnp.concatenate([l_r[0][:, 1 + s_idx], nl_i[0][:, 1 + s_idx]], axis=-1)
    e_b = jnp.concatenate([l_r[1][:, t - s_idx], nl_i[1][:, t - s_idx]], axis=-1)
    r2 = jnp.concatenate([e_f, e_b], axis=-1).reshape(g, t * h, 4 * p)
    r2 = jnp.swapaxes(r2, 1, 2).astype(BF16)
    flat = lambda a: a.reshape(1, g * p)
    return r1, r2, flat(a_re[0]), flat(a_im[0]), flat(a_re[1]), flat(a_im[1])


_LANE_GROUPS = 128 // SSM_GROUP
_ROWS8 = 8


def _to_groups_kernel(su_ref, ug_ref, *, nc):
    t, h = SSM_T, SSM_GROUP
    lane_grp = lax.broadcasted_iota(jnp.int32, (_ROWS8, 128), 1) // h

    def body(rb, carry):
        base = pl.multiple_of(rb * (_ROWS8 * t), _ROWS8 * t)
        rows = pl.ds(pl.multiple_of(rb * _ROWS8, _ROWS8), _ROWS8)
        ps = [su_ref[0, pl.ds(base + s, _ROWS8, stride=t), :] for s in range(t)]
        for gm in range(_LANE_GROUPS):
            for j in range(t * h // 128):
                acc = jnp.zeros((_ROWS8, 128), F32)
                for sm in range(_LANE_GROUPS):
                    shift = ((sm - gm) * h) % 128
                    src = ps[_LANE_GROUPS * j + sm]
                    acc = jnp.where(lane_grp == sm, pltpu.roll(src, shift, 1) if shift else src, acc)
                ug_ref[gm, rows, j * 128:(j + 1) * 128] = acc
        return carry

    lax.fori_loop(0, nc // _ROWS8, body, 0)


def _to_groups(su):
    bsz, nt, w = su.shape
    nc = nt // SSM_T
    return pl.pallas_call(
        functools.partial(_to_groups_kernel, nc=nc),
        grid=(bsz, w // 128),
        in_specs=[pl.BlockSpec((1, nt, 128), lambda b, a: (b, 0, a))],
        out_specs=pl.BlockSpec((_LANE_GROUPS, nc, SSM_T * SSM_GROUP), lambda b, a: (a, b, 0)),
        out_shape=jax.ShapeDtypeStruct((SSM_GROUPS, bsz * nc, SSM_T * SSM_GROUP), F32),
        compiler_params=_cparams(("arbitrary", "arbitrary")),
    )(su)


def _from_groups_kernel(yg_ref, y_ref, *, nc):
    t, h = SSM_T, SSM_GROUP
    lane_grp = lax.broadcasted_iota(jnp.int32, (_ROWS8, 128), 1) // h

    def body(rb, carry):
        base = pl.multiple_of(rb * (_ROWS8 * t), _ROWS8 * t)
        rows = pl.ds(pl.multiple_of(rb * _ROWS8, _ROWS8), _ROWS8)
        for tt in range(t):
            j, sm = divmod(tt, _LANE_GROUPS)
            acc = jnp.zeros((_ROWS8, 128), F32)
            for gm in range(_LANE_GROUPS):
                shift = ((gm - sm) * h) % 128
                src = yg_ref[gm, rows, j * 128:(j + 1) * 128]
                acc = jnp.where(lane_grp == gm, pltpu.roll(src, shift, 1) if shift else src, acc)
            y_ref[0, pl.ds(base + tt, _ROWS8, stride=t), :] = acc
        return carry

    lax.fori_loop(0, nc // _ROWS8, body, 0)


def _from_groups(yg, bsz):
    g, n, th = yg.shape
    nc = n // bsz
    nt = nc * SSM_T
    return pl.pallas_call(
        functools.partial(_from_groups_kernel, nc=nc),
        grid=(bsz, g // _LANE_GROUPS),
        in_specs=[pl.BlockSpec((_LANE_GROUPS, nc, th), lambda b, a: (a, b, 0))],
        out_specs=pl.BlockSpec((1, nt, 128), lambda b, a: (b, 0, a)),
        out_shape=jax.ShapeDtypeStruct((bsz, nt, g * SSM_GROUP), F32),
        compiler_params=_cparams(("arbitrary", "arbitrary")),
    )(yg)


def _ssm_phase1_kernel(u_ref, r1_ref, y_ref, fr_ref, fi_ref, br_ref, bi_ref):
    th = SSM_T * SSM_GROUP
    p = SSM_STATE
    outs = [_dot(u_ref[k].astype(BF16), r1_ref[k]) for k in range(2)]
    for k in range(2):
        y_ref[k] = outs[k][:, 0:th]
    for idx, ref in enumerate((fr_ref, fi_ref, br_ref, bi_ref)):
        lo = th + idx * p
        ref[...] = jnp.concatenate([o[:, lo:lo + p] for o in outs], axis=1)


def _ssm_phase1(ug, r1):
    g, n, th = ug.shape
    p = SSM_STATE
    plane = pl.BlockSpec((n, 2 * p), lambda i: (0, i))
    plane_shape = jax.ShapeDtypeStruct((n, g * p), F32)
    return pl.pallas_call(
        _ssm_phase1_kernel,
        grid=(g // 2,),
        in_specs=[pl.BlockSpec((2, n, th), lambda i: (i, 0, 0)),
                  pl.BlockSpec((2, th, th + 4 * p), lambda i: (i, 0, 0))],
        out_specs=[pl.BlockSpec((2, n, th), lambda i: (i, 0, 0)), plane, plane, plane, plane],
        out_shape=[jax.ShapeDtypeStruct((g, n, th), F32), plane_shape, plane_shape, plane_shape, plane_shape],
        compiler_params=_cparams(("arbitrary",)),
    )(ug, r1)


def _ssm_scan_kernel(dfr_ref, dfi_ref, dbr_ref, dbi_ref, afr_ref, afi_ref, abr_ref, abi_ref,
                     sfr_ref, sfi_ref, sbr_ref, sbi_ref, *, nc, bsz):
    width = dfr_ref.shape[1]

    def run(dr_ref, di_ref, ar_ref, ai_ref, or_ref, oi_ref, chunk_of_step):
        ar = ar_ref[...]
        ai = ai_ref[...]

        def body(i, s):
            sr, si = s
            rows = pl.ds(chunk_of_step(i), bsz, stride=nc)
            or_ref[rows, :] = sr
            oi_ref[rows, :] = si
            return (ar * sr - ai * si + dr_ref[rows, :], ar * si + ai * sr + di_ref[rows, :])

        zero = jnp.zeros((bsz, width), F32)
        lax.fori_loop(0, nc, body, (zero, zero))

    run(dfr_ref, dfi_ref, afr_ref, afi_ref, sfr_ref, sfi_ref, lambda i: i)
    run(dbr_ref, dbi_ref, abr_ref, abi_ref, sbr_ref, sbi_ref,
        lambda i: jnp.where(i < SSM_NCTX, SSM_NCTX - 1 - i, nc + SSM_NCTX - 1 - i))


def _ssm_scan(planes, decays, bsz):
    n, width = planes[0].shape
    blk = pl.BlockSpec((n, 128), lambda i: (0, i))
    arow = pl.BlockSpec((1, 128), lambda i: (0, i))
    shape = jax.ShapeDtypeStruct((n, width), F32)
    return pl.pallas_call(
        functools.partial(_ssm_scan_kernel, nc=n // bsz, bsz=bsz),
        grid=(width // 128,),
        in_specs=[blk] * 4 + [arow] * 4,
        out_specs=[blk] * 4,
        out_shape=[shape] * 4,
        compiler_params=_cparams(("arbitrary",)),
    )(*planes, *decays)


def _ssm_phase2_kernel(y_ref, sfr_ref, sfi_ref, sbr_ref, sbi_ref, r2_ref, o_ref):
    p = SSM_STATE
    for k in range(2):
        acc = y_ref[k]
        for idx, ref in enumerate((sfr_ref, sfi_ref, sbr_ref, sbi_ref)):
            acc = acc + _dot(ref[:, k * p:(k + 1) * p].astype(BF16), r2_ref[k, idx * p:(idx + 1) * p, :])
        o_ref[k] = acc


def _ssm_phase2(yg, states, r2):
    g, n, th = yg.shape
    p = SSM_STATE
    plane = pl.BlockSpec((n, 2 * p), lambda i: (0, i))
    return pl.pallas_call(
        _ssm_phase2_kernel,
        grid=(g // 2,),
        in_specs=[pl.BlockSpec((2, n, th), lambda i: (i, 0, 0)), plane, plane, plane, plane,
                  pl.BlockSpec((2, 4 * p, th), lambda i: (i, 0, 0))],
        out_specs=pl.BlockSpec((2, n, th), lambda i: (i, 0, 0)),
        out_shape=jax.ShapeDtypeStruct((g, n, th), F32),
        compiler_params=_cparams(("arbitrary",)),
    )(yg, *states, r2)


def _ssm_mix(su_all, ops):
    r1, r2 = ops[0], ops[1]
    bsz = su_all.shape[0]
    ug = _to_groups(su_all)
    y1, *planes = _ssm_phase1(ug, r1)
    states = _ssm_scan(planes, ops[2:], bsz)
    return _from_groups(_ssm_phase2(y1, states, r2), bsz)


def _gla_direction(qk, v, z, wg, bg, s_ref, d, o_ref, reverse):
    tb = qk.shape[0]
    c_len = GLA_CHUNK
    x = _dot(z.astype(BF16), wg) + bg
    gate = (jnp.minimum(x, 0.0) - jnp.log(1.0 + jnp.exp(-jnp.abs(x)))) * (1.0 / GLA_TAU)
    ti = lax.broadcasted_iota(jnp.int32, (tb, tb), 0)
    si = lax.broadcasted_iota(jnp.int32, (tb, tb), 1)
    order = (si >= ti) if reverse else (si <= ti)
    tri = jnp.where(jnp.logical_and(order, ti // c_len == si // c_len), 1.0, 0.0).astype(F32)
    bcum_all = _dot(tri, gate, HIGHEST)
    keep = order[0:c_len, 0:c_len]
    q_all = (qk[:, 0:GLA_QK_W] * (GLA_DK ** -0.5) * jnp.exp(bcum_all)).astype(BF16)
    k_all = qk[:, GLA_QK_W:2 * GLA_QK_W]
    kin_all = (k_all * jnp.exp(-bcum_all)).astype(BF16)
    v_bf = v.astype(BF16)
    states = [s_ref[d, hd] for hd in range(GLA_HEADS)]
    chunks = range(tb // c_len)
    for c in (reversed(chunks) if reverse else chunks):
        rs = slice(c * c_len, (c + 1) * c_len)
        bcum = bcum_all[rs]
        blast = bcum[0:1] if reverse else bcum[c_len - 1:c_len]
        k_up = (k_all[rs] * jnp.exp(blast - bcum)).astype(BF16)
        dec = jnp.exp(blast)
        for hd in range(GLA_HEADS):
            ks = slice(hd * GLA_DK, (hd + 1) * GLA_DK)
            vs = slice(hd * GLA_DV, (hd + 1) * GLA_DV)
            vh = v_bf[rs, vs]
            q_in = q_all[rs, ks]
            att = jnp.where(keep, _dot_nt(q_in, kin_all[rs, ks]), 0.0)
            o_ref[0, rs, vs] = _dot(att.astype(BF16), vh) + _dot_nt(q_in, states[hd].astype(BF16))
            states[hd] = states[hd] * dec[:, ks] + _dot_tn(vh, k_up[:, ks])
    for hd in range(GLA_HEADS):
        s_ref[d, hd] = states[hd]


def _gla_kernel(qkf_ref, vf_ref, zf_ref, qkb_ref, vb_ref, zb_ref, wg_ref, bg_ref, of_ref, ob_ref, s_ref):
    @pl.when(pl.program_id(1) == 0)
    def _():
        s_ref[...] = jnp.zeros_like(s_ref)

    _gla_direction(qkf_ref[0], vf_ref[0], zf_ref[0], wg_ref[0], bg_ref[0], s_ref, 0, of_ref, False)
    _gla_direction(qkb_ref[0], vb_ref[0], zb_ref[0], wg_ref[1], bg_ref[1], s_ref, 1, ob_ref, True)


def _gla(gqk, gv, gz, wg_pad, bg):
    bsz, nt, _ = gqk.shape
    tb = TOKEN_TILE
    tiles = nt // tb
    fwd = lambda b, i: (b, i, 0)
    bwd = lambda b, i: (b, jnp.where(i == 0, 0, tiles - i), 0)
    spec = lambda width, im: pl.BlockSpec((1, tb, width), im)
    return pl.pallas_call(
        _gla_kernel,
        grid=(bsz, tiles),
        in_specs=[spec(2 * GLA_QK_W, fwd), spec(GLA_V_W, fwd), spec(GZ_PAD, fwd),
                  spec(2 * GLA_QK_W, bwd), spec(GLA_V_W, bwd), spec(GZ_PAD, bwd),
                  pl.BlockSpec((2, GZ_PAD, GLA_QK_W), lambda b, i: (0, 0, 0)),
                  pl.BlockSpec((2, 1, GLA_QK_W), lambda b, i: (0, 0, 0))],
        out_specs=[spec(GLA_V_W, fwd), spec(GLA_V_W, bwd)],
        out_shape=[jax.ShapeDtypeStruct((bsz, nt, GLA_V_W), F32), jax.ShapeDtypeStruct((bsz, nt, GLA_V_W), F32)],
        scratch_shapes=[pltpu.VMEM((2, GLA_HEADS, GLA_DV, GLA_DK), F32)],
        compiler_params=_cparams(("arbitrary", "arbitrary")),
    )(gqk, gv, gz, gqk, gv, gz, wg_pad, bg)


def _outproj_kernel(x_ref, m_ref, att_ref, y_ref, u_ref, of_ref, ob_ref, r_ref,
                    ssmd_ref, wglu_ref, bglu_ref, gng_ref, wo_ref, ln1g_ref, ln1b_ref, router_ref,
                    x1_ref, h2_ref, aff_ref):
    nb, tm, d = x_ref.shape
    rows = nb * tm
    flat = lambda ref: ref[...].reshape(rows, ref.shape[-1])
    m = m_ref[0]
    zin = flat(y_ref) + ssmd_ref[...] * flat(u_ref)
    z = 0.5 * zin * (1.0 + jnp.tanh(math.sqrt(2.0 / math.pi) * (zin + 0.044715 * (zin * zin * zin))))
    ssm = z * _sigmoid(_dot(z.astype(BF16), wglu_ref[...]) + bglu_ref[...])
    proj = _dot(flat(att_ref), wo_ref[0:ATT_Q_W, :]) + _dot(ssm.astype(BF16), wo_ref[ATT_Q_W:ATT_Q_W + SSM_WIDTH, :])
    o = flat(of_ref) + flat(ob_ref)
    r = flat(r_ref)
    gng = gng_ref[...]
    base = ATT_Q_W + SSM_WIDTH
    for hd in range(GLA_HEADS):
        vs = slice(hd * GLA_DV, (hd + 1) * GLA_DV)
        oh = o[:, vs]
        rh = r[:, vs]
        oh = oh * lax.rsqrt(jnp.mean(oh * oh, axis=-1, keepdims=True) + LN_EPS) * gng
        gla = oh * (rh * _sigmoid(rh))
        proj = proj + _dot(gla.astype(BF16), wo_ref[base + hd * GLA_DV:base + (hd + 1) * GLA_DV, :])
    proj = proj.reshape(nb, tm, d)
    x1 = _ln(ALPHA * x_ref[...] + m[:, 2:3] * proj) * ln1g_ref[...] + ln1b_ref[...]
    x1_ref[...] = x1
    h2 = (_ln(x1) * (1.0 + m[:, 4:5]) + m[:, 3:4]).reshape(rows, d)
    h_hi = h2.astype(BF16)
    h2_ref[...] = h_hi.reshape(nb, tm, d)
    h_lo = (h2 - h_hi.astype(F32)).astype(BF16)
    l_hi = _dot(h_hi, router_ref[...])
    logits = l_hi + pltpu.roll(l_hi, 128 - N_EXPERTS, 1) + _dot(h_lo, router_ref[...])
    col = lax.broadcasted_iota(jnp.int32, logits.shape, 1)
    logits = jnp.where(col < N_EXPERTS, logits, NEG_INF)
    e = jnp.exp(logits - jnp.max(logits, axis=-1, keepdims=True))
    aff_ref[...] = (e / jnp.sum(e, axis=-1, keepdims=True)).reshape(nb, tm, 128)


def _outproj(x_all, mods, att, y, su, o_f, o_b, gr, ssm_d, w_glu, b_glu, gn_g, w_out, ln1_g, ln1_b, router_split,
             with_ctx):
    bsz, nt, d = x_all.shape
    tm = TOKEN_TILE
    nb = BATCH_TILE
    t0 = 0 if with_ctx else CTX_LEN // tm
    tiles = nt // tm - t0
    att_t0 = 0 if with_ctx else -t0
    tok_in = lambda width: pl.BlockSpec((nb, tm, width), lambda b, i: (b, i + t0, 0))
    tok_out = lambda width: pl.BlockSpec((nb, tm, width), lambda b, i: (b, i, 0))
    const = lambda shape: pl.BlockSpec(shape, lambda b, i: tuple(0 for _ in shape))
    once = lambda shape: pl.BlockSpec(shape, lambda b, i: tuple(0 for _ in shape), pipeline_mode=pl.Buffered(1))
    shp = lambda width, dt: jax.ShapeDtypeStruct((bsz, tiles * tm, width), dt)
    return pl.pallas_call(
        _outproj_kernel,
        grid=(bsz // nb, tiles),
        in_specs=[
            tok_in(d),
            pl.BlockSpec((1, nb, N_MOD, d), lambda b, i: (jnp.minimum(i + t0, 1), b, 0, 0)),
            pl.BlockSpec((nb, tm, ATT_Q_W), lambda b, i: (b, i + t0 + att_t0, 0)),
            tok_in(SSM_WIDTH), tok_in(SSM_WIDTH), tok_in(GLA_V_W), tok_in(GLA_V_W), tok_in(GLA_V_W),
            const((1, SSM_WIDTH)), const((SSM_WIDTH, SSM_WIDTH)), const((1, SSM_WIDTH)), const((1, GLA_DV)),
            once((d, d)), const((1, d)), const((1, d)), const((d, 128)),
        ],
        out_specs=[tok_out(d), tok_out(d), tok_out(128)],
        out_shape=[shp(d, F32), shp(d, BF16), shp(128, F32)],
        compiler_params=_cparams(("arbitrary", "arbitrary")),
    )(x_all, mods, att, y, su, o_f, o_b, gr, ssm_d, w_glu, b_glu, gn_g, w_out, ln1_g, ln1_b, router_split)


def _expert_kernel(x_ref, g_ref, wg_ref, wu_ref, wd_ref, o_ref, hid_ref, *, nf, tf):
    s = pl.program_id(1)

    @pl.when(s < nf)
    def _():
        x = x_ref[0]
        a = _dot(x, wg_ref[0, 0].astype(BF16))
        u = _dot(x, wu_ref[0, 0].astype(BF16))
        col = pl.multiple_of(s * tf, tf)
        hid_ref[:, pl.ds(col, tf)] = ((a * _sigmoid(a)) * u).astype(BF16)

    @pl.when(s >= nf)
    def _():
        o_ref[0] = (_dot(hid_ref[...], wd_ref[0, 0].astype(BF16)) * g_ref[0]).astype(BF16)


def _experts(xe, ge, w_gate, w_up, w_down, layer):
    e, r, d = xe.shape
    ff = w_gate.shape[3]
    tf = 256
    nf = ff // tf
    nd = d // tf
    up = lambda i, s: (layer, i, 0, jnp.minimum(s, nf - 1))
    down = lambda i, s: (layer, i, 0, jnp.maximum(s - nf, 0))
    return pl.pallas_call(
        functools.partial(_expert_kernel, nf=nf, tf=tf),
        grid=(e, nf + nd),
        in_specs=[pl.BlockSpec((1, r, d), lambda i, s: (i, 0, 0)),
                  pl.BlockSpec((1, r, 1), lambda i, s: (i, 0, 0)),
                  pl.BlockSpec((1, 1, d, tf), up),
                  pl.BlockSpec((1, 1, d, tf), up),
                  pl.BlockSpec((1, 1, ff, tf), down)],
        out_specs=pl.BlockSpec((1, r, tf), lambda i, s: (i, 0, jnp.maximum(s - nf, 0))),
        out_shape=jax.ShapeDtypeStruct((e, r, d), BF16),
        scratch_shapes=[pltpu.VMEM((r, ff), BF16)],
        compiler_params=_cparams(("arbitrary", "arbitrary")),
    )(xe, ge, w_gate, w_up, w_down)


COMBINE_WIN = 64
COMBINE_ALIGN = 16


def _combine_kernel(lo_ref, hi_ref, x1_ref, m_ref, g_ref, b_ref, ye_hbm, tok_hbm, o_ref, buf, tokbuf, sem,
                    *, tiles, nsteps, rtot):
    n_exp = N_EXPERTS
    win = COMBINE_WIN
    tm = x1_ref.shape[1]
    step = pl.program_id(0) * tiles + pl.program_id(1)
    slot = step % 2
    extra = 2

    def first_row(st, e, k):
        u = (lo_ref[st * n_exp + e] // COMBINE_ALIGN) * COMBINE_ALIGN + k * win
        return u, pl.multiple_of(jnp.minimum(u, rtot - win), COMBINE_ALIGN)

    def copies(st, k, dst):
        out = []
        for e in range(n_exp):
            rows = pl.ds(first_row(st, e, k)[1], win)
            out.append(pltpu.make_async_copy(ye_hbm.at[rows, :], buf.at[dst, e], sem.at[dst]))
            out.append(pltpu.make_async_copy(tok_hbm.at[rows, :], tokbuf.at[dst, e], sem.at[dst]))
        return out

    @pl.when(step == 0)
    def _():
        for cp in copies(step, 0, slot):
            cp.start()

    @pl.when(step + 1 < nsteps)
    def _():
        for cp in copies(step + 1, 0, 1 - slot):
            cp.start()

    tile_tok = pl.program_id(1) * tm + lax.broadcasted_iota(jnp.int32, (win, tm), 1)
    jrow = lax.broadcasted_iota(jnp.int32, (win, 1), 0)

    def spread(k, src):
        parts = []
        for e in range(n_exp):
            u, s0 = first_row(step, e, k)
            row = s0 + jrow
            mine = jnp.logical_and(row >= jnp.maximum(u, lo_ref[step * n_exp + e]), row < hi_ref[step * n_exp + e])
            hit = jnp.logical_and(tokbuf[src, e][:, 0:1] == tile_tok, mine)
            parts.append(jnp.where(hit, 1.0, 0.0).astype(BF16))
        onehot_t = jnp.concatenate(parts, axis=0)
        return _dot_tn(onehot_t, buf[src].reshape(n_exp * win, buf.shape[-1]))

    for cp in copies(step, 0, slot):
        cp.wait()
    acc = spread(0, slot)

    rounds = jnp.int32(1)
    for e in range(n_exp):
        span = hi_ref[step * n_exp + e] - (lo_ref[step * n_exp + e] // COMBINE_ALIGN) * COMBINE_ALIGN
        rounds = jnp.maximum(rounds, (span + win - 1) // win)

    def more(k, acc):
        for cp in copies(step, k, extra):
            cp.start()
        for cp in copies(step, k, extra):
            cp.wait()
        return acc + spread(k, extra)

    acc = lax.fori_loop(1, rounds, more, acc)
    m = m_ref[0, 0]
    o_ref[0] = _ln(ALPHA * x1_ref[0] + m[5:6] * acc) * g_ref[...] + b_ref[...]


def _combine(lo, hi, x1, mods, ln2_g, ln2_b, ye, tok_of_row, with_ctx):
    bsz, nt, d = x1.shape
    tm = TOKEN_TILE
    tiles = nt // tm
    kind0 = 0 if with_ctx else 1
    tok = lambda width: pl.BlockSpec((1, tm, width), lambda b, i, lo, hi: (b, i, 0))
    vec = pl.BlockSpec((1, d), lambda b, i, lo, hi: (0, 0))
    return pl.pallas_call(
        functools.partial(_combine_kernel, tiles=tiles, nsteps=bsz * tiles, rtot=ye.shape[0]),
        grid_spec=pltpu.PrefetchScalarGridSpec(
            num_scalar_prefetch=2,
            grid=(bsz, tiles),
            in_specs=[tok(d),
                      pl.BlockSpec((1, 1, N_MOD, d), lambda b, i, lo, hi: (jnp.minimum(i + kind0, 1), b, 0, 0)),
                      vec, vec, pl.BlockSpec(memory_space=pl.ANY), pl.BlockSpec(memory_space=pl.ANY)],
            out_specs=tok(d),
            scratch_shapes=[pltpu.VMEM((3, N_EXPERTS, COMBINE_WIN, d), BF16),
                            pltpu.VMEM((3, N_EXPERTS, COMBINE_WIN, 128), jnp.int32),
                            pltpu.SemaphoreType.DMA((3,))],
        ),
        out_shape=jax.ShapeDtypeStruct((bsz, nt, d), F32),
        compiler_params=_cparams(("arbitrary", "arbitrary")),
    )(lo, hi, x1, mods, ln2_g, ln2_b, ye, tok_of_row)


def _route(aff, cap):
    gate, idx = lax.top_k(jnp.swapaxes(aff, 1, 2), cap)
    return gate, idx


def _rope_tables(seq):
    half = HEAD_DIM // 2
    nf = half // 2
    inv = ROPE_BASE ** (-jnp.arange(nf, dtype=F32) / nf)
    pos = jnp.arange(seq)
    ang_r = (pos // GRID_W).astype(F32)[:, None] * inv
    ang_c = (pos % GRID_W).astype(F32)[:, None] * inv
    zeros = jnp.zeros_like(ang_r)
    cos = jnp.concatenate([jnp.cos(ang_r), jnp.cos(ang_r), jnp.cos(ang_c), jnp.cos(ang_c)], axis=-1)
    sa = jnp.concatenate([-jnp.sin(ang_r), zeros, -jnp.sin(ang_c), zeros], axis=-1)
    sb = jnp.concatenate([zeros, jnp.sin(ang_r), zeros, jnp.sin(ang_c)], axis=-1)
    ctx = lambda fill: jnp.full((CTX_LEN, HEAD_DIM), fill, F32)
    return (jnp.concatenate([ctx(1.0), cos], axis=0), jnp.concatenate([ctx(0.0), sa], axis=0),
            jnp.concatenate([ctx(0.0), sb], axis=0))


def kernel(x, c, ctx, c_ctx, w_ada, b_ada, w_in, attn_sink, ssm_lam_re, ssm_lam_im, ssm_log_dt,
           ssm_b_re, ssm_b_im, ssm_c_re, ssm_c_im, ssm_d, ssm_w_glu, ssm_b_glu, gla_w_gate,
           gla_b_gate, gla_norm_g, w_out, ln1_g, ln1_b, ln2_g, ln2_b, router, exp_w_gate,
           exp_w_up, exp_w_down):
    bsz, seq, d = x.shape
    n_exp = router.shape[-1]
    cc = jnp.zeros((8, d), F32).at[:bsz].set(c).at[bsz].set(c_ctx)
    mod_all = _ada(cc, w_ada, b_ada)
    cos_t, sa_t, sb_t = _rope_tables(seq)
    x_all = jnp.concatenate([ctx, x], axis=1)
    bidx = jnp.arange(bsz)[None, :, None]

    for l in range(DEPTH):
        with_ctx = l < DEPTH - 1
        m = mod_all[l].reshape(8, N_MOD, d)
        mods = jnp.stack([jnp.broadcast_to(m[bsz], (bsz, N_MOD, d)), m[:bsz]], axis=0)
        w_in_p = jnp.pad(w_in[l], ((0, 0), (0, N_IN_PAD - N_IN))).astype(BF16)
        q, k, v, su, gqk, gv, gr, gz = _inproj(x_all, mods, cos_t, sa_t, sb_t, w_in_p)

        att = _attention(attn_sink[l], q, k, v, with_ctx)

        ops = _ssm_operators(*_ssm_params(ssm_lam_re[l], ssm_lam_im[l], ssm_log_dt[l], ssm_b_re[l],
                                          ssm_b_im[l], ssm_c_re[l], ssm_c_im[l]))
        y = _ssm_mix(su, ops)

        wg_pad = jnp.zeros((2, GZ_PAD, GLA_QK_W), F32)
        wg_pad = wg_pad.at[0, 0:GLA_RANK].set(gla_w_gate[l, 0]).at[1, GLA_RANK:2 * GLA_RANK].set(gla_w_gate[l, 1])
        o_f, o_b = _gla(gqk, gv, gz, wg_pad.astype(BF16), gla_b_gate[l][:, None, :])

        r_hi = router[l].astype(BF16)
        r_lo = (router[l] - r_hi.astype(F32)).astype(BF16)
        router_split = jnp.pad(jnp.concatenate([r_hi, r_lo], axis=1), ((0, 0), (0, 128 - 2 * n_exp)))
        x1, h2, aff = _outproj(x_all, mods, att, y, su, o_f, o_b, gr,
                               ssm_d[l][None], ssm_w_glu[l].astype(BF16), ssm_b_glu[l][None], gla_norm_g[l][None],
                               w_out[l].astype(BF16), ln1_g[l][None], ln1_b[l][None], router_split, with_ctx)

        t0 = CTX_LEN if with_ctx else 0
        cap = CAPACITY_FACTOR * seq // n_exp
        gate, idx = _route(aff[:, t0:, :n_exp], cap)
        idx = idx + t0
        if with_ctx:
            cap_c = CAPACITY_FACTOR * CTX_LEN // n_exp
            gate_c, idx_c = _route(aff[:, :CTX_LEN, :n_exp], cap_c)
            gate = jnp.concatenate([gate, gate_c], axis=-1)
            idx = jnp.concatenate([idx, idx_c], axis=-1)
        idx, gate = lax.sort((idx, gate), dimension=-1, num_keys=1)
        idx_e = jnp.swapaxes(idx, 0, 1)
        gate_e = jnp.swapaxes(gate, 0, 1)
        slots = idx_e.shape[-1]
        rows_e = bsz * slots
        xe = h2[bidx, idx_e].reshape(n_exp, rows_e, d)
        ye = _experts(xe, gate_e.reshape(n_exp, rows_e, 1), exp_w_gate, exp_w_up, exp_w_down, l)
        nt_out = x1.shape[1]
        bounds = jnp.arange(nt_out // TOKEN_TILE + 1) * TOKEN_TILE
        cnt = jnp.sum(idx_e[..., None] < bounds, axis=2).astype(jnp.int32)
        base = (jnp.arange(n_exp)[:, None, None] * rows_e + jnp.arange(bsz)[None, :, None] * slots).astype(jnp.int32)
        lo = jnp.transpose(base + cnt[:, :, :-1], (1, 2, 0)).reshape(-1)
        hi = jnp.transpose(base + cnt[:, :, 1:], (1, 2, 0)).reshape(-1)
        tok_of_row = jnp.broadcast_to(idx_e.reshape(n_exp * rows_e, 1).astype(jnp.int32), (n_exp * rows_e, 128))
        x_all = _combine(lo, hi, x1, mods, ln2_g[l][None], ln2_b[l][None],
                         ye.reshape(n_exp * rows_e, d), tok_of_row, with_ctx)
    return x_all
```

```python
import functools
import math

import jax
import jax.numpy as jnp
from jax import lax
from jax.experimental import pallas as pl
from jax.experimental.pallas import tpu as pltpu

F32 = jnp.float32
BF16 = jnp.bfloat16
HIGHEST = lax.Precision.HIGHEST

D_MODEL = 2048
DEPTH = 2
GRID_W = 64
CTX_LEN = 256
ATT_HEADS = 8
ATT_KV_HEADS = 2
HEAD_DIM = 128
ATT_BLOCK = 128
ATT_QBLOCKS = 2
ROPE_BASE = 10000.0
SSM_WIDTH = 512
SSM_GROUP = 16
SSM_GROUPS = 32
SSM_STATE = 64
GLA_HEADS = 4
GLA_DK = 64
GLA_DV = 128
GLA_RANK = 16
GLA_TAU = 16.0
GLA_CHUNK = 64
N_EXPERTS = 16
EXPERT_FF = 2048
CAPACITY_FACTOR = 2
N_MOD = 6
ALPHA = (2 * DEPTH) ** 0.25
LN_EPS = 1e-6
NEG_INF = -1e30

ATT_Q_W = ATT_HEADS * HEAD_DIM
ATT_KV_W = ATT_KV_HEADS * HEAD_DIM
GLA_QK_W = GLA_HEADS * GLA_DK
GLA_V_W = GLA_HEADS * GLA_DV
N_IN = 3616
N_IN_PAD = 3712
GZ_PAD = N_IN_PAD - 3584

TOKEN_TILE = 256
BATCH_TILE = 2
EXPERT_GROUPS = 2
SSM_T = 32
SSM_NCTX = CTX_LEN // SSM_T
VMEM_LIMIT = 56 * 1024 * 1024


def _cparams(sem, vmem=VMEM_LIMIT):
    return pltpu.CompilerParams(dimension_semantics=sem, vmem_limit_bytes=vmem)


def _dot(a, b, precision=None):
    return jnp.dot(a, b, preferred_element_type=F32, precision=precision)


def _dot_nt(a, b, precision=None):
    return lax.dot_general(a, b, (((1,), (1,)), ((), ())), preferred_element_type=F32, precision=precision)


def _dot_tn(a, b, precision=None):
    return lax.dot_general(a, b, (((0,), (0,)), ((), ())), preferred_element_type=F32, precision=precision)


def _sigmoid(x):
    return 1.0 / (1.0 + jnp.exp(-x))


def _ln(x):
    mu = jnp.mean(x, axis=-1, keepdims=True)
    xc = x - mu
    var = jnp.mean(xc * xc, axis=-1, keepdims=True)
    return xc * lax.rsqrt(var + LN_EPS)


def _ada_kernel(c_ref, w_ref, b_ref, o_ref):
    c = c_ref[...]
    s = c * _sigmoid(c)
    w = w_ref[0]
    w_hi = w.astype(BF16)
    w_lo = (w - w_hi.astype(F32)).astype(BF16)
    s_hi = s.astype(BF16)
    s_lo = (s - s_hi.astype(F32)).astype(BF16)
    both = _dot(jnp.concatenate([s_hi, s_lo], axis=0), w_hi)
    rows = s.shape[0]
    o_ref[0] = both[0:rows] + both[rows:2 * rows] + _dot(s_hi, w_lo) + b_ref[0]


def _ada(cc, w_ada, b_ada):
    depth, d, n = w_ada.shape
    tn = 1024
    return pl.pallas_call(
        _ada_kernel,
        grid=(depth, n // tn),
        in_specs=[
            pl.BlockSpec((8, d), lambda l, j: (0, 0)),
            pl.BlockSpec((1, d, tn), lambda l, j: (l, 0, j)),
            pl.BlockSpec((1, 1, tn), lambda l, j: (l, 0, j)),
        ],
        out_specs=pl.BlockSpec((1, 8, tn), lambda l, j: (l, 0, j)),
        out_shape=jax.ShapeDtypeStruct((depth, 8, n), F32),
        compiler_params=_cparams(("arbitrary", "arbitrary")),
    )(cc, w_ada, b_ada.reshape(depth, 1, n))


def _inproj_kernel(x_ref, m_ref, cos_ref, sa_ref, sb_ref, w_ref,
                   q_ref, k_ref, v_ref, su_ref, gqk_ref, gv_ref, gr_ref, gz_ref):
    nb, tm, d = x_ref.shape
    rows = nb * tm
    m = m_ref[0]
    h = _ln(x_ref[...]) * (1.0 + m[:, 1:2]) + m[:, 0:1]
    h = h.reshape(rows, d).astype(BF16)
    tile = lambda ref: jnp.concatenate([ref[...]] * nb, axis=0)
    cos = tile(cos_ref)
    sa = tile(sa_ref)
    sb = tile(sb_ref)

    def rope(t):
        return t * cos + pltpu.roll(t, 96, 1) * sa + pltpu.roll(t, 32, 1) * sb

    def put(ref, val, sl=slice(None)):
        ref[:, :, sl] = val.reshape(nb, tm, val.shape[-1])

    scale = HEAD_DIM ** -0.5
    q = _dot(h, w_ref[:, 0:ATT_Q_W])
    for hd in range(ATT_HEADS):
        sl = slice(hd * HEAD_DIM, (hd + 1) * HEAD_DIM)
        put(q_ref, (rope(q[:, sl]) * scale).astype(BF16), sl)
    kk = _dot(h, w_ref[:, 1024:1280])
    for hd in range(ATT_KV_HEADS):
        sl = slice(hd * HEAD_DIM, (hd + 1) * HEAD_DIM)
        put(k_ref, rope(kk[:, sl]).astype(BF16), sl)
    put(v_ref, _dot(h, w_ref[:, 1280:1536]).astype(BF16))
    put(su_ref, _dot(h, w_ref[:, 1536:2048]))
    put(gqk_ref, _dot(h, w_ref[:, 2048:2560]))
    put(gv_ref, _dot(h, w_ref[:, 2560:3072]))
    put(gr_ref, _dot(h, w_ref[:, 3072:3584]))
    put(gz_ref, _dot(h, w_ref[:, 3584:N_IN_PAD]))


def _inproj(x_all, mods, cos_t, sa_t, sb_t, w_in_bf16):
    bsz, nt, d = x_all.shape
    tm = TOKEN_TILE
    nb = BATCH_TILE
    tiles = nt // tm
    tok = lambda width: pl.BlockSpec((nb, tm, width), lambda b, i: (b, i, 0))
    tab = pl.BlockSpec((tm, HEAD_DIM), lambda b, i: (i, 0))
    shp = lambda width, dt: jax.ShapeDtypeStruct((bsz, nt, width), dt)
    return pl.pallas_call(
        _inproj_kernel,
        grid=(bsz // nb, tiles),
        in_specs=[
            tok(d),
            pl.BlockSpec((1, nb, N_MOD, d), lambda b, i: (jnp.minimum(i, 1), b, 0, 0)),
            tab, tab, tab,
            pl.BlockSpec((d, N_IN_PAD), lambda b, i: (0, 0), pipeline_mode=pl.Buffered(1)),
        ],
        out_specs=[tok(ATT_Q_W), tok(ATT_KV_W), tok(ATT_KV_W), tok(SSM_WIDTH),
                   tok(2 * GLA_QK_W), tok(GLA_V_W), tok(GLA_V_W), tok(GZ_PAD)],
        out_shape=[shp(ATT_Q_W, BF16), shp(ATT_KV_W, BF16), shp(ATT_KV_W, BF16), shp(SSM_WIDTH, F32),
                   shp(2 * GLA_QK_W, F32), shp(GLA_V_W, F32), shp(GLA_V_W, F32), shp(GZ_PAD, F32)],
        compiler_params=_cparams(("arbitrary", "arbitrary")),
    )(x_all, mods, cos_t, sa_t, sb_t, w_in_bf16)


def _attn_kernel(sink_ref, q_ref, k_ref, v_ref, o_ref, *, blk0, nblk):
    for sub in range(ATT_QBLOCKS):
        _attn_block(sink_ref, q_ref, k_ref, v_ref, o_ref, sub, pl.program_id(1) * ATT_QBLOCKS + sub + blk0, nblk)


def _attn_block(sink_ref, q_ref, k_ref, v_ref, o_ref, sub, qb, nblk):
    w = ATT_BLOCK
    qrows = slice(sub * w, (sub + 1) * w)
    cb = CTX_LEN // w
    n = qb - cb
    last = nblk - 1
    prev_b = jnp.maximum(qb - 1, cb)
    cur_b = jnp.maximum(qb, cb)
    next_b = jnp.clip(qb + 1, cb, last)
    ok_prev = n >= 1
    ok_cur = n >= 0
    ok_next = jnp.logical_and(n >= 0, qb < last)
    g = ATT_HEADS // ATT_KV_HEADS
    rows = g * w
    ii = lax.broadcasted_iota(jnp.int32, (rows, w), 0) % w
    jj = lax.broadcasted_iota(jnp.int32, (rows, w), 1)
    m_prev = jnp.logical_and(jj >= ii, ok_prev)
    m_cur = jnp.logical_and(jj >= 0, ok_cur)
    m_next = jnp.logical_and(jj <= ii, ok_next)
    rowg = lax.broadcasted_iota(jnp.int32, (rows, 1), 0) // w

    def rows_of(ref, blk, sl):
        return ref[0, pl.ds(pl.multiple_of(blk * w, w), w), sl]

    for kvh in range(ATT_KV_HEADS):
        sl = slice(kvh * HEAD_DIM, (kvh + 1) * HEAD_DIM)
        qs = jnp.concatenate(
            [q_ref[0, qrows, (kvh * g + a) * HEAD_DIM:(kvh * g + a + 1) * HEAD_DIM] for a in range(g)], axis=0)
        sink = jnp.zeros((rows, 1), F32)
        for a in range(g):
            sink = jnp.where(rowg == a, sink_ref[kvh * g + a], sink)
        s_p = jnp.where(m_prev, _dot_nt(qs, rows_of(k_ref, prev_b, sl)), NEG_INF)
        s_c = jnp.where(m_cur, _dot_nt(qs, rows_of(k_ref, cur_b, sl)), NEG_INF)
        s_n = jnp.where(m_next, _dot_nt(qs, rows_of(k_ref, next_b, sl)), NEG_INF)
        s_x = _dot_nt(qs, k_ref[0, 0:CTX_LEN, sl])
        mx = jnp.maximum(jnp.maximum(jnp.max(s_p, axis=-1, keepdims=True), jnp.max(s_c, axis=-1, keepdims=True)),
                         jnp.maximum(jnp.max(s_n, axis=-1, keepdims=True), jnp.max(s_x, axis=-1, keepdims=True)))
        mx = jnp.maximum(mx, sink)
        p_p = jnp.exp(s_p - mx)
        p_c = jnp.exp(s_c - mx)
        p_n = jnp.exp(s_n - mx)
        p_x = jnp.exp(s_x - mx)
        den = (jnp.sum(p_p, axis=-1, keepdims=True) + jnp.sum(p_c, axis=-1, keepdims=True)
               + jnp.sum(p_n, axis=-1, keepdims=True) + jnp.sum(p_x, axis=-1, keepdims=True)
               + jnp.exp(sink - mx))
        o = (_dot(p_p.astype(BF16), rows_of(v_ref, prev_b, sl)) + _dot(p_c.astype(BF16), rows_of(v_ref, cur_b, sl))
             + _dot(p_n.astype(BF16), rows_of(v_ref, next_b, sl)) + _dot(p_x.astype(BF16), v_ref[0, 0:CTX_LEN, sl]))
        o = o / den
        for a in range(g):
            hs = slice((kvh * g + a) * HEAD_DIM, (kvh * g + a + 1) * HEAD_DIM)
            o_ref[0, qrows, hs] = o[a * w:(a + 1) * w].astype(BF16)


def _attention(sink, q_all, k_all, v_all, with_ctx):
    bsz, nt, _ = q_all.shape
    w = ATT_BLOCK
    nblk = nt // w
    blk0 = 0 if with_ctx else CTX_LEN // w
    wq = w * ATT_QBLOCKS
    step0 = blk0 // ATT_QBLOCKS
    return pl.pallas_call(
        functools.partial(_attn_kernel, blk0=blk0, nblk=nblk),
        grid=(bsz, (nblk - blk0) // ATT_QBLOCKS),
        in_specs=[
            pl.BlockSpec(memory_space=pltpu.SMEM),
            pl.BlockSpec((1, wq, ATT_Q_W), lambda b, j: (b, j + step0, 0)),
            pl.BlockSpec((1, nt, ATT_KV_W), lambda b, j: (b, 0, 0)),
            pl.BlockSpec((1, nt, ATT_KV_W), lambda b, j: (b, 0, 0)),
        ],
        out_specs=pl.BlockSpec((1, wq, ATT_Q_W), lambda b, j: (b, j, 0)),
        out_shape=jax.ShapeDtypeStruct((bsz, (nblk - blk0) * w, ATT_Q_W), BF16),
        compiler_params=_cparams(("arbitrary", "arbitrary")),
    )(sink, q_all, k_all, v_all)


def _ssm_ops_kernel(lr_ref, li_ref, ldt_ref, lrc_ref, lic_ref, ldtc_ref, bre_ref, bim_ref, cret_ref, cimt_ref,
                    r1_ref, r2_ref, are_ref, aim_ref):
    t, h, p = SSM_T, SSM_GROUP, SSM_STATE
    th = t * h
    kpad = t + _ROWS8
    lane_lag = lax.broadcasted_iota(jnp.int32, (kpad, th), 1) // h
    krow = lax.broadcasted_iota(jnp.int32, (kpad, th), 0)
    pick = lambda lag_of_lane: jnp.where(krow == lag_of_lane, 1.0, 0.0).astype(F32)
    tile_h = jnp.where(lax.broadcasted_iota(jnp.int32, (h, th), 0)
                       == lax.broadcasted_iota(jnp.int32, (h, th), 1) % h, 1.0, 0.0).astype(F32)
    lane = lax.broadcasted_iota(jnp.int32, (h, th), 1)
    z, gs = [], []
    for d in range(2):
        lam_r, lam_i = lr_ref[d, 0], li_ref[d, 0]
        dt = jnp.exp(ldt_ref[d, 0])
        ldr, ldi = lam_r * dt, lam_i * dt
        mag = jnp.exp(ldr)
        lbr, lbi = mag * jnp.cos(ldi), mag * jnp.sin(ldi)
        den = lam_r * lam_r + lam_i * lam_i
        nr = lbr - 1.0
        cf_r = (nr * lam_r + lbi * lam_i) / den
        cf_i = (lbi * lam_r - nr * lam_i) / den
        br, bi = bre_ref[d, 0], bim_ref[d, 0]
        bbr = cf_r * br - cf_i * bi
        bbi = cf_r * bi + cf_i * br
        kcol = jnp.minimum(lax.broadcasted_iota(jnp.int32, (kpad, 1), 0), t).astype(F32)
        mk = jnp.exp(kcol * ldr)
        wr_all, wi_all = mk * jnp.cos(kcol * ldi), mk * jnp.sin(kcol * ldi)
        are_ref[d, 0] = wr_all[t:t + 1]
        aim_ref[d, 0] = wi_all[t:t + 1]
        gs.append([(bbr * wr_all[k:k + 1] - bbi * wi_all[k:k + 1], bbr * wi_all[k:k + 1] + bbi * wr_all[k:k + 1])
                   for k in range(t)])
        dtc = jnp.exp(ldtc_ref[d, 0])
        ldrc, ldic = lrc_ref[d, 0] * dtc, lic_ref[d, 0] * dtc
        krow_f = jnp.minimum(lax.broadcasted_iota(jnp.int32, (1, kpad), 1), t).astype(F32)
        mkc = jnp.exp(ldrc * krow_f)
        wrc, wic = mkc * jnp.cos(ldic * krow_f), mkc * jnp.sin(ldic * krow_f)
        ct_r = _dot(cret_ref[d, 0], tile_h, HIGHEST)
        ct_i = _dot(cimt_ref[d, 0], tile_h, HIGHEST)

        def c_lam(sel):
            er, ei = _dot(wrc, sel, HIGHEST), _dot(wic, sel, HIGHEST)
            return ct_r * er - ct_i * ei, ct_r * ei + ct_i * er

        lr_, li_ = c_lam(pick(lane_lag if d == 0 else t - 1 - lane_lag))
        z.append(_dot(bbr, lr_, HIGHEST) - _dot(bbi, li_, HIGHEST))
        er_, ei_ = c_lam(pick(lane_lag + 1 if d == 0 else t - lane_lag))
        r2_ref[0, (2 * d) * p:(2 * d + 1) * p, :] = er_.astype(BF16)
        r2_ref[0, (2 * d + 1) * p:(2 * d + 2) * p, :] = (-ei_).astype(BF16)
    zf, zb = z
    for s in range(t):
        fwd = zf if s == 0 else pltpu.roll(zf, s * h, 1)
        bwd = zb if s == t - 1 else pltpu.roll(zb, th - (t - 1 - s) * h, 1)
        taps = jnp.where(lane >= s * h, fwd, 0.0) + jnp.where(lane < (s + 1) * h, bwd, 0.0)
        inc = jnp.concatenate([gs[0][t - 1 - s][0], gs[0][t - 1 - s][1], gs[1][s][0], gs[1][s][1]], axis=1)
        r1_ref[0, s * h:(s + 1) * h, :] = jnp.concatenate([taps, inc], axis=1).astype(BF16)


def _ssm_ops(lam_re, lam_im, log_dt, b_re, b_im, c_re, c_im):
    g, p, h, t = SSM_GROUPS, SSM_STATE, SSM_GROUP, SSM_T
    row = lambda a: a.reshape(2, g, 1, p)
    col = lambda a: a.reshape(2, g, p, 1)
    ldt = jnp.broadcast_to(log_dt[:, :, None], (2, g, p))
    tr = lambda a: jnp.swapaxes(a, 2, 3)
    row_spec = pl.BlockSpec((2, 1, 1, p), lambda i: (0, i, 0, 0))
    col_spec = pl.BlockSpec((2, 1, p, 1), lambda i: (0, i, 0, 0))
    hp_spec = pl.BlockSpec((2, 1, h, p), lambda i: (0, i, 0, 0))
    ph_spec = pl.BlockSpec((2, 1, p, h), lambda i: (0, i, 0, 0))
    a_shape = jax.ShapeDtypeStruct((2, g, 1, p), F32)
    r1, r2, a_re, a_im = pl.pallas_call(
        _ssm_ops_kernel,
        grid=(g,),
        in_specs=[row_spec, row_spec, row_spec, col_spec, col_spec, col_spec, hp_spec, hp_spec, ph_spec, ph_spec],
        out_specs=[pl.BlockSpec((1, t * h, t * h + 4 * p), lambda i: (i, 0, 0)),
                   pl.BlockSpec((1, 4 * p, t * h), lambda i: (i, 0, 0)), row_spec, row_spec],
        out_shape=[jax.ShapeDtypeStruct((g, t * h, t * h + 4 * p), BF16),
                   jax.ShapeDtypeStruct((g, 4 * p, t * h), BF16), a_shape, a_shape],
        compiler_params=_cparams(("arbitrary",)),
    )(row(lam_re), row(lam_im), row(ldt), col(lam_re), col(lam_im), col(ldt), tr(b_re), tr(b_im), tr(c_re), tr(c_im))
    flat = lambda a: a.reshape(1, g * p)
    return r1, r2, flat(a_re[0]), flat(a_im[0]), flat(a_re[1]), flat(a_im[1])


_LANE_GROUPS = 128 // SSM_GROUP
_ROWS8 = 8


def _to_groups_kernel(su_ref, ug_ref, *, nc):
    t, h = SSM_T, SSM_GROUP
    lane_grp = lax.broadcasted_iota(jnp.int32, (_ROWS8, 128), 1) // h

    def body(rb, carry):
        base = pl.multiple_of(rb * (_ROWS8 * t), _ROWS8 * t)
        rows = pl.ds(pl.multiple_of(rb * _ROWS8, _ROWS8), _ROWS8)
        ps = [su_ref[0, pl.ds(base + s, _ROWS8, stride=t), :] for s in range(t)]
        for gm in range(_LANE_GROUPS):
            for j in range(t * h // 128):
                acc = jnp.zeros((_ROWS8, 128), F32)
                for sm in range(_LANE_GROUPS):
                    shift = ((sm - gm) * h) % 128
                    src = ps[_LANE_GROUPS * j + sm]
                    acc = jnp.where(lane_grp == sm, pltpu.roll(src, shift, 1) if shift else src, acc)
                ug_ref[gm, rows, j * 128:(j + 1) * 128] = acc
        return carry

    lax.fori_loop(0, nc // _ROWS8, body, 0)


def _to_groups(su):
    bsz, nt, w = su.shape
    nc = nt // SSM_T
    return pl.pallas_call(
        functools.partial(_to_groups_kernel, nc=nc),
        grid=(bsz, w // 128),
        in_specs=[pl.BlockSpec((1, nt, 128), lambda b, a: (b, 0, a))],
        out_specs=pl.BlockSpec((_LANE_GROUPS, nc, SSM_T * SSM_GROUP), lambda b, a: (a, b, 0)),
        out_shape=jax.ShapeDtypeStruct((SSM_GROUPS, bsz * nc, SSM_T * SSM_GROUP), F32),
        compiler_params=_cparams(("arbitrary", "arbitrary")),
    )(su)


def _from_groups_kernel(yg_ref, y_ref, *, nc):
    t, h = SSM_T, SSM_GROUP
    lane_grp = lax.broadcasted_iota(jnp.int32, (_ROWS8, 128), 1) // h

    def body(rb, carry):
        base = pl.multiple_of(rb * (_ROWS8 * t), _ROWS8 * t)
        rows = pl.ds(pl.multiple_of(rb * _ROWS8, _ROWS8), _ROWS8)
        for tt in range(t):
            j, sm = divmod(tt, _LANE_GROUPS)
            acc = jnp.zeros((_ROWS8, 128), F32)
            for gm in range(_LANE_GROUPS):
                shift = ((gm - sm) * h) % 128
                src = yg_ref[gm, rows, j * 128:(j + 1) * 128]
                acc = jnp.where(lane_grp == gm, pltpu.roll(src, shift, 1) if shift else src, acc)
            y_ref[0, pl.ds(base + tt, _ROWS8, stride=t), :] = acc
        return carry

    lax.fori_loop(0, nc // _ROWS8, body, 0)


def _from_groups(yg, bsz):
    g, n, th = yg.shape
    nc = n // bsz
    nt = nc * SSM_T
    return pl.pallas_call(
        functools.partial(_from_groups_kernel, nc=nc),
        grid=(bsz, g // _LANE_GROUPS),
        in_specs=[pl.BlockSpec((_LANE_GROUPS, nc, th), lambda b, a: (a, b, 0))],
        out_specs=pl.BlockSpec((1, nt, 128), lambda b, a: (b, 0, a)),
        out_shape=jax.ShapeDtypeStruct((bsz, nt, g * SSM_GROUP), F32),
        compiler_params=_cparams(("arbitrary", "arbitrary")),
    )(yg)


def _ssm_phase1_kernel(u_ref, r1_ref, y_ref, fr_ref, fi_ref, br_ref, bi_ref):
    th = SSM_T * SSM_GROUP
    p = SSM_STATE
    outs = [_dot(u_ref[k].astype(BF16), r1_ref[k]) for k in range(2)]
    for k in range(2):
        y_ref[k] = outs[k][:, 0:th]
    for idx, ref in enumerate((fr_ref, fi_ref, br_ref, bi_ref)):
        lo = th + idx * p
        ref[...] = jnp.concatenate([o[:, lo:lo + p] for o in outs], axis=1)


def _ssm_phase1(ug, r1):
    g, n, th = ug.shape
    p = SSM_STATE
    plane = pl.BlockSpec((n, 2 * p), lambda i: (0, i))
    plane_shape = jax.ShapeDtypeStruct((n, g * p), F32)
    return pl.pallas_call(
        _ssm_phase1_kernel,
        grid=(g // 2,),
        in_specs=[pl.BlockSpec((2, n, th), lambda i: (i, 0, 0)),
                  pl.BlockSpec((2, th, th + 4 * p), lambda i: (i, 0, 0))],
        out_specs=[pl.BlockSpec((2, n, th), lambda i: (i, 0, 0)), plane, plane, plane, plane],
        out_shape=[jax.ShapeDtypeStruct((g, n, th), F32), plane_shape, plane_shape, plane_shape, plane_shape],
        compiler_params=_cparams(("arbitrary",)),
    )(ug, r1)


def _ssm_scan_kernel(dfr_ref, dfi_ref, dbr_ref, dbi_ref, afr_ref, afi_ref, abr_ref, abi_ref,
                     sfr_ref, sfi_ref, sbr_ref, sbi_ref, *, nc, bsz):
    width = dfr_ref.shape[1]

    def run(dr_ref, di_ref, ar_ref, ai_ref, or_ref, oi_ref, chunk_of_step):
        ar = ar_ref[...]
        ai = ai_ref[...]

        def body(i, s):
            sr, si = s
            rows = pl.ds(chunk_of_step(i), bsz, stride=nc)
            or_ref[rows, :] = sr
            oi_ref[rows, :] = si
            return (ar * sr - ai * si + dr_ref[rows, :], ar * si + ai * sr + di_ref[rows, :])

        zero = jnp.zeros((bsz, width), F32)
        lax.fori_loop(0, nc, body, (zero, zero))

    run(dfr_ref, dfi_ref, afr_ref, afi_ref, sfr_ref, sfi_ref, lambda i: i)
    run(dbr_ref, dbi_ref, abr_ref, abi_ref, sbr_ref, sbi_ref,
        lambda i: jnp.where(i < SSM_NCTX, SSM_NCTX - 1 - i, nc + SSM_NCTX - 1 - i))


def _ssm_scan(planes, decays, bsz):
    n, width = planes[0].shape
    blk = pl.BlockSpec((n, 128), lambda i: (0, i))
    arow = pl.BlockSpec((1, 128), lambda i: (0, i))
    shape = jax.ShapeDtypeStruct((n, width), F32)
    return pl.pallas_call(
        functools.partial(_ssm_scan_kernel, nc=n // bsz, bsz=bsz),
        grid=(width // 128,),
        in_specs=[blk] * 4 + [arow] * 4,
        out_specs=[blk] * 4,
        out_shape=[shape] * 4,
        compiler_params=_cparams(("arbitrary",)),
    )(*planes, *decays)


def _ssm_phase2_kernel(y_ref, sfr_ref, sfi_ref, sbr_ref, sbi_ref, r2_ref, o_ref):
    p = SSM_STATE
    for k in range(2):
        acc = y_ref[k]
        for idx, ref in enumerate((sfr_ref, sfi_ref, sbr_ref, sbi_ref)):
            acc = acc + _dot(ref[:, k * p:(k + 1) * p].astype(BF16), r2_ref[k, idx * p:(idx + 1) * p, :])
        o_ref[k] = acc


def _ssm_phase2(yg, states, r2):
    g, n, th = yg.shape
    p = SSM_STATE
    plane = pl.BlockSpec((n, 2 * p), lambda i: (0, i))
    return pl.pallas_call(
        _ssm_phase2_kernel,
        grid=(g // 2,),
        in_specs=[pl.BlockSpec((2, n, th), lambda i: (i, 0, 0)), plane, plane, plane, plane,
                  pl.BlockSpec((2, 4 * p, th), lambda i: (i, 0, 0))],
        out_specs=pl.BlockSpec((2, n, th), lambda i: (i, 0, 0)),
        out_shape=jax.ShapeDtypeStruct((g, n, th), F32),
        compiler_params=_cparams(("arbitrary",)),
    )(yg, *states, r2)


def _ssm_mix(su_all, ops):
    r1, r2 = ops[0], ops[1]
    bsz = su_all.shape[0]
    ug = _to_groups(su_all)
    y1, *planes = _ssm_phase1(ug, r1)
    states = _ssm_scan(planes, ops[2:], bsz)
    return _from_groups(_ssm_phase2(y1, states, r2), bsz)


def _gla_direction(qk, v, z, wg, bg, s_ref, d, o_ref, reverse):
    tb = qk.shape[0]
    c_len = GLA_CHUNK
    x = _dot(z.astype(BF16), wg) + bg
    gate = (jnp.minimum(x, 0.0) - jnp.log(1.0 + jnp.exp(-jnp.abs(x)))) * (1.0 / GLA_TAU)
    ti = lax.broadcasted_iota(jnp.int32, (tb, tb), 0)
    si = lax.broadcasted_iota(jnp.int32, (tb, tb), 1)
    order = (si >= ti) if reverse else (si <= ti)
    tri = jnp.where(jnp.logical_and(order, ti // c_len == si // c_len), 1.0, 0.0).astype(F32)
    bcum_all = _dot(tri, gate, HIGHEST)
    q_all = (qk[:, 0:GLA_QK_W] * (GLA_DK ** -0.5) * jnp.exp(bcum_all)).astype(BF16)
    k_all = qk[:, GLA_QK_W:2 * GLA_QK_W]
    kin_all = k_all * jnp.exp(-bcum_all)
    v_bf = v.astype(BF16)
    nh = GLA_HEADS
    head_of = lambda shape, axis, size: lax.broadcasted_iota(jnp.int32, shape, axis) // size
    kmask = head_of((nh * c_len, GLA_QK_W), 0, c_len) == head_of((nh * c_len, GLA_QK_W), 1, GLA_DK)
    vmask = head_of((nh * c_len, GLA_V_W), 0, c_len) == head_of((nh * c_len, GLA_V_W), 1, GLA_DV)
    smask = head_of((GLA_V_W, GLA_QK_W), 0, GLA_DV) == head_of((GLA_V_W, GLA_QK_W), 1, GLA_DK)
    tq = lax.broadcasted_iota(jnp.int32, (c_len, nh * c_len), 0)
    sk = lax.broadcasted_iota(jnp.int32, (c_len, nh * c_len), 1) % c_len
    keep = (sk >= tq) if reverse else (sk <= tq)
    chunks = range(tb // c_len)
    for c in (reversed(chunks) if reverse else chunks):
        rs = slice(c * c_len, (c + 1) * c_len)
        bcum = bcum_all[rs]
        blast = bcum[0:1] if reverse else bcum[c_len - 1:c_len]
        k_up = (k_all[rs] * jnp.exp(blast - bcum)).astype(BF16)
        q_in = q_all[rs]
        k_exp = jnp.where(kmask, jnp.concatenate([kin_all[rs]] * nh, axis=0), 0.0).astype(BF16)
        v_exp = jnp.where(vmask, jnp.concatenate([v[rs]] * nh, axis=0), 0.0).astype(BF16)
        att = jnp.where(keep, _dot_nt(q_in, k_exp), 0.0)
        st = s_ref[d]
        o_ref[0, rs, :] = _dot(att.astype(BF16), v_exp) + _dot_nt(q_in, st.astype(BF16))
        s_ref[d] = st * jnp.exp(blast) + jnp.where(smask, _dot_tn(v_bf[rs], k_up), 0.0)


def _gla_kernel(qkf_ref, vf_ref, zf_ref, qkb_ref, vb_ref, zb_ref, wg_ref, bg_ref, of_ref, ob_ref, s_ref):
    @pl.when(pl.program_id(1) == 0)
    def _():
        s_ref[...] = jnp.zeros_like(s_ref)

    _gla_direction(qkf_ref[0], vf_ref[0], zf_ref[0], wg_ref[0], bg_ref[0], s_ref, 0, of_ref, False)
    _gla_direction(qkb_ref[0], vb_ref[0], zb_ref[0], wg_ref[1], bg_ref[1], s_ref, 1, ob_ref, True)


def _gla(gqk, gv, gz, wg_pad, bg):
    bsz, nt, _ = gqk.shape
    tb = TOKEN_TILE
    tiles = nt // tb
    fwd = lambda b, i: (b, i, 0)
    bwd = lambda b, i: (b, jnp.where(i == 0, 0, tiles - i), 0)
    spec = lambda width, im: pl.BlockSpec((1, tb, width), im)
    return pl.pallas_call(
        _gla_kernel,
        grid=(bsz, tiles),
        in_specs=[spec(2 * GLA_QK_W, fwd), spec(GLA_V_W, fwd), spec(GZ_PAD, fwd),
                  spec(2 * GLA_QK_W, bwd), spec(GLA_V_W, bwd), spec(GZ_PAD, bwd),
                  pl.BlockSpec((2, GZ_PAD, GLA_QK_W), lambda b, i: (0, 0, 0)),
                  pl.BlockSpec((2, 1, GLA_QK_W), lambda b, i: (0, 0, 0))],
        out_specs=[spec(GLA_V_W, fwd), spec(GLA_V_W, bwd)],
        out_shape=[jax.ShapeDtypeStruct((bsz, nt, GLA_V_W), F32), jax.ShapeDtypeStruct((bsz, nt, GLA_V_W), F32)],
        scratch_shapes=[pltpu.VMEM((2, GLA_V_W, GLA_QK_W), F32)],
        compiler_params=_cparams(("arbitrary", "arbitrary")),
    )(gqk, gv, gz, gqk, gv, gz, wg_pad, bg)


def _outproj_kernel(x_ref, m_ref, att_ref, y_ref, u_ref, of_ref, ob_ref, r_ref,
                    ssmd_ref, wglu_ref, bglu_ref, gng_ref, wo_ref, ln1g_ref, ln1b_ref, router_ref,
                    x1_ref, h2_ref, aff_ref):
    nb, tm, d = x_ref.shape
    rows = nb * tm
    flat = lambda ref: ref[...].reshape(rows, ref.shape[-1])
    m = m_ref[0]
    zin = flat(y_ref) + ssmd_ref[...] * flat(u_ref)
    z = 0.5 * zin * (1.0 + jnp.tanh(math.sqrt(2.0 / math.pi) * (zin + 0.044715 * (zin * zin * zin))))
    ssm = z * _sigmoid(_dot(z.astype(BF16), wglu_ref[...]) + bglu_ref[...])
    proj = _dot(flat(att_ref), wo_ref[0:ATT_Q_W, :]) + _dot(ssm.astype(BF16), wo_ref[ATT_Q_W:ATT_Q_W + SSM_WIDTH, :])
    o = flat(of_ref) + flat(ob_ref)
    r = flat(r_ref)
    gng = gng_ref[...]
    base = ATT_Q_W + SSM_WIDTH
    for hd in range(GLA_HEADS):
        vs = slice(hd * GLA_DV, (hd + 1) * GLA_DV)
        oh = o[:, vs]
        rh = r[:, vs]
        oh = oh * lax.rsqrt(jnp.mean(oh * oh, axis=-1, keepdims=True) + LN_EPS) * gng
        gla = oh * (rh * _sigmoid(rh))
        proj = proj + _dot(gla.astype(BF16), wo_ref[base + hd * GLA_DV:base + (hd + 1) * GLA_DV, :])
    proj = proj.reshape(nb, tm, d)
    x1 = _ln(ALPHA * x_ref[...] + m[:, 2:3] * proj) * ln1g_ref[...] + ln1b_ref[...]
    x1_ref[...] = x1
    h2 = (_ln(x1) * (1.0 + m[:, 4:5]) + m[:, 3:4]).reshape(rows, d)
    h_hi = h2.astype(BF16)
    h2_ref[...] = h_hi.reshape(nb, tm, d)
    h_lo = (h2 - h_hi.astype(F32)).astype(BF16)
    l_hi = _dot(h_hi, router_ref[...])
    logits = l_hi + pltpu.roll(l_hi, 128 - N_EXPERTS, 1) + _dot(h_lo, router_ref[...])
    col = lax.broadcasted_iota(jnp.int32, logits.shape, 1)
    logits = jnp.where(col < N_EXPERTS, logits, NEG_INF)
    e = jnp.exp(logits - jnp.max(logits, axis=-1, keepdims=True))
    aff_ref[...] = (e / jnp.sum(e, axis=-1, keepdims=True)).reshape(nb, tm, 128)


def _outproj(x_all, mods, att, y, su, o_f, o_b, gr, ssm_d, w_glu, b_glu, gn_g, w_out, ln1_g, ln1_b, router_split,
             with_ctx):
    bsz, nt, d = x_all.shape
    tm = TOKEN_TILE
    nb = BATCH_TILE
    t0 = 0 if with_ctx else CTX_LEN // tm
    tiles = nt // tm - t0
    att_t0 = 0 if with_ctx else -t0
    tok_in = lambda width: pl.BlockSpec((nb, tm, width), lambda b, i: (b, i + t0, 0))
    tok_out = lambda width: pl.BlockSpec((nb, tm, width), lambda b, i: (b, i, 0))
    const = lambda shape: pl.BlockSpec(shape, lambda b, i: tuple(0 for _ in shape))
    once = lambda shape: pl.BlockSpec(shape, lambda b, i: tuple(0 for _ in shape), pipeline_mode=pl.Buffered(1))
    shp = lambda width, dt: jax.ShapeDtypeStruct((bsz, tiles * tm, width), dt)
    return pl.pallas_call(
        _outproj_kernel,
        grid=(bsz // nb, tiles),
        in_specs=[
            tok_in(d),
            pl.BlockSpec((1, nb, N_MOD, d), lambda b, i: (jnp.minimum(i + t0, 1), b, 0, 0)),
            pl.BlockSpec((nb, tm, ATT_Q_W), lambda b, i: (b, i + t0 + att_t0, 0)),
            tok_in(SSM_WIDTH), tok_in(SSM_WIDTH), tok_in(GLA_V_W), tok_in(GLA_V_W), tok_in(GLA_V_W),
            const((1, SSM_WIDTH)), const((SSM_WIDTH, SSM_WIDTH)), const((1, SSM_WIDTH)), const((1, GLA_DV)),
            once((d, d)), const((1, d)), const((1, d)), const((d, 128)),
        ],
        out_specs=[tok_out(d), tok_out(d), tok_out(128)],
        out_shape=[shp(d, F32), shp(d, BF16), shp(128, F32)],
        compiler_params=_cparams(("arbitrary", "arbitrary")),
    )(x_all, mods, att, y, su, o_f, o_b, gr, ssm_d, w_glu, b_glu, gn_g, w_out, ln1_g, ln1_b, router_split)


def _expert_kernel(x_ref, g_ref, wg_ref, wu_ref, wd_ref, o_ref, hid_ref, *, nf, tf):
    s = pl.program_id(1)

    @pl.when(s < nf)
    def _():
        x = x_ref[0]
        a = _dot(x, wg_ref[0, 0].astype(BF16))
        u = _dot(x, wu_ref[0, 0].astype(BF16))
        col = pl.multiple_of(s * tf, tf)
        hid_ref[:, pl.ds(col, tf)] = ((a * _sigmoid(a)) * u).astype(BF16)

    @pl.when(s >= nf)
    def _():
        o_ref[0] = (_dot(hid_ref[...], wd_ref[0, 0].astype(BF16)) * g_ref[0]).astype(BF16)


def _experts(xe, ge, w_gate, w_up, w_down, layer, e0):
    e, r, d = xe.shape
    ff = w_gate.shape[3]
    tf = 256
    nf = ff // tf
    nd = d // tf
    up = lambda i, s: (layer, e0 + i, 0, jnp.minimum(s, nf - 1))
    down = lambda i, s: (layer, e0 + i, 0, jnp.maximum(s - nf, 0))
    return pl.pallas_call(
        functools.partial(_expert_kernel, nf=nf, tf=tf),
        grid=(e, nf + nd),
        in_specs=[pl.BlockSpec((1, r, d), lambda i, s: (i, 0, 0)),
                  pl.BlockSpec((1, r, 1), lambda i, s: (i, 0, 0)),
                  pl.BlockSpec((1, 1, d, tf), up),
                  pl.BlockSpec((1, 1, d, tf), up),
                  pl.BlockSpec((1, 1, ff, tf), down)],
        out_specs=pl.BlockSpec((1, r, tf), lambda i, s: (i, 0, jnp.maximum(s - nf, 0))),
        out_shape=jax.ShapeDtypeStruct((e, r, d), BF16),
        scratch_shapes=[pltpu.VMEM((r, ff), BF16)],
        compiler_params=_cparams(("arbitrary", "arbitrary")),
    )(xe, ge, w_gate, w_up, w_down)


COMBINE_WIN = 64
COMBINE_ALIGN = 16


def _combine_kernel(lo_ref, hi_ref, x1_ref, m_ref, g_ref, b_ref, tok_hbm, *rest, tiles, nsteps, group_rows):
    ye_groups, (o_ref, buf, tokbuf, sem) = rest[:-4], rest[-4:]
    _combine_body(lo_ref, hi_ref, x1_ref, m_ref, g_ref, b_ref, tok_hbm, ye_groups, o_ref, buf, tokbuf, sem,
                  tiles, nsteps, group_rows)


def _combine_body(lo_ref, hi_ref, x1_ref, m_ref, g_ref, b_ref, tok_hbm, ye_groups, o_ref, buf, tokbuf, sem,
                  tiles, nsteps, group_rows):
    n_exp = N_EXPERTS
    win = COMBINE_WIN
    tm = x1_ref.shape[1]
    step = pl.program_id(0) * tiles + pl.program_id(1)
    slot = step % 2
    extra = 2

    per_group = n_exp // len(ye_groups)

    def first_row(st, e, k):
        u = (lo_ref[st * n_exp + e] // COMBINE_ALIGN) * COMBINE_ALIGN + k * win
        group_end = (e // per_group + 1) * group_rows
        return u, pl.multiple_of(jnp.minimum(u, group_end - win), COMBINE_ALIGN)

    def copies(st, k, dst):
        out = []
        for e in range(n_exp):
            s0 = first_row(st, e, k)[1]
            local = pl.ds(pl.multiple_of(s0 - (e // per_group) * group_rows, COMBINE_ALIGN), win)
            out.append(pltpu.make_async_copy(ye_groups[e // per_group].at[local, :], buf.at[dst, e], sem.at[dst]))
            out.append(pltpu.make_async_copy(tok_hbm.at[pl.ds(s0, win), :], tokbuf.at[dst, e], sem.at[dst]))
        return out

    @pl.when(step == 0)
    def _():
        for cp in copies(step, 0, slot):
            cp.start()

    @pl.when(step + 1 < nsteps)
    def _():
        for cp in copies(step + 1, 0, 1 - slot):
            cp.start()

    tile_tok = pl.program_id(1) * tm + lax.broadcasted_iota(jnp.int32, (win, tm), 1)
    jrow = lax.broadcasted_iota(jnp.int32, (win, 1), 0)

    def spread(k, src):
        parts = []
        for e in range(n_exp):
            u, s0 = first_row(step, e, k)
            row = s0 + jrow
            mine = jnp.logical_and(row >= jnp.maximum(u, lo_ref[step * n_exp + e]), row < hi_ref[step * n_exp + e])
            hit = jnp.logical_and(tokbuf[src, e][:, 0:1] == tile_tok, mine)
            parts.append(jnp.where(hit, 1.0, 0.0).astype(BF16))
        onehot_t = jnp.concatenate(parts, axis=0)
        return _dot_tn(onehot_t, buf[src].reshape(n_exp * win, buf.shape[-1]))

    for cp in copies(step, 0, slot):
        cp.wait()
    acc = spread(0, slot)

    rounds = jnp.int32(1)
    for e in range(n_exp):
        span = hi_ref[step * n_exp + e] - (lo_ref[step * n_exp + e] // COMBINE_ALIGN) * COMBINE_ALIGN
        rounds = jnp.maximum(rounds, (span + win - 1) // win)

    def more(k, acc):
        for cp in copies(step, k, extra):
            cp.start()
        for cp in copies(step, k, extra):
            cp.wait()
        return acc + spread(k, extra)

    acc = lax.fori_loop(1, rounds, more, acc)
    m = m_ref[0, 0]
    o_ref[0] = _ln(ALPHA * x1_ref[0] + m[5:6] * acc) * g_ref[...] + b_ref[...]


def _combine(lo, hi, x1, mods, ln2_g, ln2_b, tok_of_row, ye_groups, with_ctx):
    bsz, nt, d = x1.shape
    tm = TOKEN_TILE
    tiles = nt // tm
    kind0 = 0 if with_ctx else 1
    tok = lambda width: pl.BlockSpec((1, tm, width), lambda b, i, lo, hi: (b, i, 0))
    vec = pl.BlockSpec((1, d), lambda b, i, lo, hi: (0, 0))
    hbm = pl.BlockSpec(memory_space=pl.ANY)
    return pl.pallas_call(
        functools.partial(_combine_kernel, tiles=tiles, nsteps=bsz * tiles, group_rows=ye_groups[0].shape[0]),
        grid_spec=pltpu.PrefetchScalarGridSpec(
            num_scalar_prefetch=2,
            grid=(bsz, tiles),
            in_specs=[tok(d),
                      pl.BlockSpec((1, 1, N_MOD, d), lambda b, i, lo, hi: (jnp.minimum(i + kind0, 1), b, 0, 0)),
                      vec, vec, hbm] + [hbm] * len(ye_groups),
            out_specs=tok(d),
            scratch_shapes=[pltpu.VMEM((3, N_EXPERTS, COMBINE_WIN, d), BF16),
                            pltpu.VMEM((3, N_EXPERTS, COMBINE_WIN, 128), jnp.int32),
                            pltpu.SemaphoreType.DMA((3,))],
        ),
        out_shape=jax.ShapeDtypeStruct((bsz, nt, d), F32),
        compiler_params=_cparams(("arbitrary", "arbitrary")),
    )(lo, hi, x1, mods, ln2_g, ln2_b, tok_of_row, *ye_groups)


def _route(aff, cap):
    gate, idx = lax.top_k(jnp.swapaxes(aff, 1, 2), cap)
    return gate, idx


def _rope_tables(seq):
    half = HEAD_DIM // 2
    nf = half // 2
    inv = ROPE_BASE ** (-jnp.arange(nf, dtype=F32) / nf)
    pos = jnp.arange(seq)
    ang_r = (pos // GRID_W).astype(F32)[:, None] * inv
    ang_c = (pos % GRID_W).astype(F32)[:, None] * inv
    zeros = jnp.zeros_like(ang_r)
    cos = jnp.concatenate([jnp.cos(ang_r), jnp.cos(ang_r), jnp.cos(ang_c), jnp.cos(ang_c)], axis=-1)
    sa = jnp.concatenate([-jnp.sin(ang_r), zeros, -jnp.sin(ang_c), zeros], axis=-1)
    sb = jnp.concatenate([zeros, jnp.sin(ang_r), zeros, jnp.sin(ang_c)], axis=-1)
    ctx = lambda fill: jnp.full((CTX_LEN, HEAD_DIM), fill, F32)
    return (jnp.concatenate([ctx(1.0), cos], axis=0), jnp.concatenate([ctx(0.0), sa], axis=0),
            jnp.concatenate([ctx(0.0), sb], axis=0))


def kernel(x, c, ctx, c_ctx, w_ada, b_ada, w_in, attn_sink, ssm_lam_re, ssm_lam_im, ssm_log_dt,
           ssm_b_re, ssm_b_im, ssm_c_re, ssm_c_im, ssm_d, ssm_w_glu, ssm_b_glu, gla_w_gate,
           gla_b_gate, gla_norm_g, w_out, ln1_g, ln1_b, ln2_g, ln2_b, router, exp_w_gate,
           exp_w_up, exp_w_down):
    bsz, seq, d = x.shape
    n_exp = router.shape[-1]
    cc = jnp.zeros((8, d), F32).at[:bsz].set(c).at[bsz].set(c_ctx)
    mod_all = _ada(cc, w_ada, b_ada)
    cos_t, sa_t, sb_t = _rope_tables(seq)
    x_all = jnp.concatenate([ctx, x], axis=1)
    bidx = jnp.arange(bsz)[None, :, None]

    for l in range(DEPTH):
        with_ctx = l < DEPTH - 1
        m = mod_all[l].reshape(8, N_MOD, d)
        mods = jnp.stack([jnp.broadcast_to(m[bsz], (bsz, N_MOD, d)), m[:bsz]], axis=0)
        w_in_p = jnp.pad(w_in[l], ((0, 0), (0, N_IN_PAD - N_IN))).astype(BF16)
        q, k, v, su, gqk, gv, gr, gz = _inproj(x_all, mods, cos_t, sa_t, sb_t, w_in_p)

        att = _attention(attn_sink[l], q, k, v, with_ctx)

        ops = _ssm_ops(ssm_lam_re[l], ssm_lam_im[l], ssm_log_dt[l], ssm_b_re[l], ssm_b_im[l], ssm_c_re[l], ssm_c_im[l])
        y = _ssm_mix(su, ops)

        wg_pad = jnp.zeros((2, GZ_PAD, GLA_QK_W), F32)
        wg_pad = wg_pad.at[0, 0:GLA_RANK].set(gla_w_gate[l, 0]).at[1, GLA_RANK:2 * GLA_RANK].set(gla_w_gate[l, 1])
        o_f, o_b = _gla(gqk, gv, gz, wg_pad.astype(BF16), gla_b_gate[l][:, None, :])

        r_hi = router[l].astype(BF16)
        r_lo = (router[l] - r_hi.astype(F32)).astype(BF16)
        router_split = jnp.pad(jnp.concatenate([r_hi, r_lo], axis=1), ((0, 0), (0, 128 - 2 * n_exp)))
        x1, h2, aff = _outproj(x_all, mods, att, y, su, o_f, o_b, gr,
                               ssm_d[l][None], ssm_w_glu[l].astype(BF16), ssm_b_glu[l][None], gla_norm_g[l][None],
                               w_out[l].astype(BF16), ln1_g[l][None], ln1_b[l][None], router_split, with_ctx)

        t0 = CTX_LEN if with_ctx else 0
        cap = CAPACITY_FACTOR * seq // n_exp
        gate, idx = _route(aff[:, t0:, :n_exp], cap)
        idx = idx + t0
        if with_ctx:
            cap_c = CAPACITY_FACTOR * CTX_LEN // n_exp
            gate_c, idx_c = _route(aff[:, :CTX_LEN, :n_exp], cap_c)
            gate = jnp.concatenate([gate, gate_c], axis=-1)
            idx = jnp.concatenate([idx, idx_c], axis=-1)
        idx, gate = lax.sort((idx, gate), dimension=-1, num_keys=1)
        idx_e = jnp.swapaxes(idx, 0, 1)
        gate_e = jnp.swapaxes(gate, 0, 1)
        slots = idx_e.shape[-1]
        rows_e = bsz * slots
        ye_groups = []
        eg = n_exp // EXPERT_GROUPS
        for grp in range(EXPERT_GROUPS):
            sl = slice(grp * eg, (grp + 1) * eg)
            xe = h2[bidx, idx_e[sl]].reshape(eg, rows_e, d)
            ye = _experts(xe, gate_e[sl].reshape(eg, rows_e, 1), exp_w_gate, exp_w_up, exp_w_down, l, grp * eg)
            ye_groups.append(ye.reshape(eg * rows_e, d))
        nt_out = x1.shape[1]
        bounds = jnp.arange(nt_out // TOKEN_TILE + 1) * TOKEN_TILE
        cnt = jnp.sum(idx_e[..., None] < bounds, axis=2).astype(jnp.int32)
        base = (jnp.arange(n_exp)[:, None, None] * rows_e + jnp.arange(bsz)[None, :, None] * slots).astype(jnp.int32)
        lo = jnp.transpose(base + cnt[:, :, :-1], (1, 2, 0)).reshape(-1)
        hi = jnp.transpose(base + cnt[:, :, 1:], (1, 2, 0)).reshape(-1)
        tok_of_row = jnp.broadcast_to(idx_e.reshape(n_exp * rows_e, 1).astype(jnp.int32), (n_exp * rows_e, 128))
        x_all = _combine(lo, hi, x1, mods, ln2_g[l][None], ln2_b[l][None], tok_of_row, ye_groups, with_ctx)
    return x_all
```

```python
import functools
import math

import jax
import jax.numpy as jnp
from jax import lax
from jax.experimental import pallas as pl
from jax.experimental.pallas import tpu as pltpu

F32 = jnp.float32
BF16 = jnp.bfloat16
HIGHEST = lax.Precision.HIGHEST

D_MODEL = 2048
DEPTH = 2
GRID_W = 64
CTX_LEN = 256
ATT_HEADS = 8
ATT_KV_HEADS = 2
HEAD_DIM = 128
ATT_BLOCK = 128
ATT_QBLOCKS = 2
ROPE_BASE = 10000.0
SSM_WIDTH = 512
SSM_GROUP = 16
SSM_GROUPS = 32
SSM_STATE = 64
GLA_HEADS = 4
GLA_DK = 64
GLA_DV = 128
GLA_RANK = 16
GLA_TAU = 16.0
GLA_CHUNK = 64
N_EXPERTS = 16
EXPERT_FF = 2048
CAPACITY_FACTOR = 2
N_MOD = 6
ALPHA = (2 * DEPTH) ** 0.25
LN_EPS = 1e-6
NEG_INF = -1e30

ATT_Q_W = ATT_HEADS * HEAD_DIM
ATT_KV_W = ATT_KV_HEADS * HEAD_DIM
GLA_QK_W = GLA_HEADS * GLA_DK
GLA_V_W = GLA_HEADS * GLA_DV
N_IN = 3616
N_IN_PAD = 3712
GZ_PAD = N_IN_PAD - 3584

TOKEN_TILE = 256
BATCH_TILE = 2
EXPERT_GROUPS = 2
SSM_T = 32
SSM_NCTX = CTX_LEN // SSM_T
VMEM_LIMIT = 56 * 1024 * 1024


def _cparams(sem, vmem=VMEM_LIMIT):
    return pltpu.CompilerParams(dimension_semantics=sem, vmem_limit_bytes=vmem)


def _dot(a, b, precision=None):
    return jnp.dot(a, b, preferred_element_type=F32, precision=precision)


def _dot_nt(a, b, precision=None):
    return lax.dot_general(a, b, (((1,), (1,)), ((), ())), preferred_element_type=F32, precision=precision)


def _dot_tn(a, b, precision=None):
    return lax.dot_general(a, b, (((0,), (0,)), ((), ())), preferred_element_type=F32, precision=precision)


def _sigmoid(x):
    return 1.0 / (1.0 + jnp.exp(-x))


def _ln(x):
    mu = jnp.mean(x, axis=-1, keepdims=True)
    xc = x - mu
    var = jnp.mean(xc * xc, axis=-1, keepdims=True)
    return xc * lax.rsqrt(var + LN_EPS)


def _ada_kernel(c_ref, w_ref, b_ref, o_ref):
    c = c_ref[...]
    s = c * _sigmoid(c)
    w = w_ref[0]
    w_hi = w.astype(BF16)
    w_lo = (w - w_hi.astype(F32)).astype(BF16)
    s_hi = s.astype(BF16)
    s_lo = (s - s_hi.astype(F32)).astype(BF16)
    both = _dot(jnp.concatenate([s_hi, s_lo], axis=0), w_hi)
    rows = s.shape[0]
    o_ref[0] = both[0:rows] + both[rows:2 * rows] + _dot(s_hi, w_lo) + b_ref[0]


def _ada(cc, w_ada, b_ada):
    depth, d, n = w_ada.shape
    tn = 1024
    return pl.pallas_call(
        _ada_kernel,
        grid=(depth, n // tn),
        in_specs=[
            pl.BlockSpec((8, d), lambda l, j: (0, 0)),
            pl.BlockSpec((1, d, tn), lambda l, j: (l, 0, j)),
            pl.BlockSpec((1, 1, tn), lambda l, j: (l, 0, j)),
        ],
        out_specs=pl.BlockSpec((1, 8, tn), lambda l, j: (l, 0, j)),
        out_shape=jax.ShapeDtypeStruct((depth, 8, n), F32),
        compiler_params=_cparams(("arbitrary", "arbitrary")),
    )(cc, w_ada, b_ada.reshape(depth, 1, n))


def _inproj_kernel(ctx_ref, x_ref, m_ref, cos_ref, sa_ref, sb_ref, w_ref,
                   q_ref, k_ref, v_ref, su_ref, gqk_ref, gv_ref, gr_ref, gz_ref):
    nb, tm, d = x_ref.shape
    rows = nb * tm
    m = m_ref[0]
    x = jnp.where(pl.program_id(1) == 0, ctx_ref[...], x_ref[...])
    h = _ln(x) * (1.0 + m[:, 1:2]) + m[:, 0:1]
    h = h.reshape(rows, d).astype(BF16)
    tile = lambda ref: jnp.concatenate([ref[...]] * nb, axis=0)
    cos = tile(cos_ref)
    sa = tile(sa_ref)
    sb = tile(sb_ref)

    def rope(t):
        return t * cos + pltpu.roll(t, 96, 1) * sa + pltpu.roll(t, 32, 1) * sb

    def put(ref, val, sl=slice(None)):
        ref[:, :, sl] = val.reshape(nb, tm, val.shape[-1])

    scale = HEAD_DIM ** -0.5
    q = _dot(h, w_ref[:, 0:ATT_Q_W])
    for hd in range(ATT_HEADS):
        sl = slice(hd * HEAD_DIM, (hd + 1) * HEAD_DIM)
        put(q_ref, (rope(q[:, sl]) * scale).astype(BF16), sl)
    kk = _dot(h, w_ref[:, 1024:1280])
    for hd in range(ATT_KV_HEADS):
        sl = slice(hd * HEAD_DIM, (hd + 1) * HEAD_DIM)
        put(k_ref, rope(kk[:, sl]).astype(BF16), sl)
    put(v_ref, _dot(h, w_ref[:, 1280:1536]).astype(BF16))
    put(su_ref, _dot(h, w_ref[:, 1536:2048]))
    put(gqk_ref, _dot(h, w_ref[:, 2048:2560]))
    put(gv_ref, _dot(h, w_ref[:, 2560:3072]))
    put(gr_ref, _dot(h, w_ref[:, 3072:3584]))
    put(gz_ref, _dot(h, w_ref[:, 3584:N_IN_PAD]))


def _inproj(ctx_src, x_src, x_off, nt, mods, cos_t, sa_t, sb_t, w_in_bf16):
    bsz, _, d = x_src.shape
    tm = TOKEN_TILE
    nb = BATCH_TILE
    tiles = nt // tm
    tok = lambda width: pl.BlockSpec((nb, tm, width), lambda b, i: (b, i, 0))
    tab = pl.BlockSpec((tm, HEAD_DIM), lambda b, i: (i, 0))
    shp = lambda width, dt: jax.ShapeDtypeStruct((bsz, nt, width), dt)
    return pl.pallas_call(
        _inproj_kernel,
        grid=(bsz // nb, tiles),
        in_specs=[
            pl.BlockSpec((nb, tm, d), lambda b, i: (b, 0, 0)),
            pl.BlockSpec((nb, tm, d), lambda b, i: (b, jnp.maximum(i - x_off, 0), 0)),
            pl.BlockSpec((1, nb, N_MOD, d), lambda b, i: (jnp.minimum(i, 1), b, 0, 0)),
            tab, tab, tab,
            pl.BlockSpec((d, N_IN_PAD), lambda b, i: (0, 0), pipeline_mode=pl.Buffered(1)),
        ],
        out_specs=[tok(ATT_Q_W), tok(ATT_KV_W), tok(ATT_KV_W), tok(SSM_WIDTH),
                   tok(2 * GLA_QK_W), tok(GLA_V_W), tok(GLA_V_W), tok(GZ_PAD)],
        out_shape=[shp(ATT_Q_W, BF16), shp(ATT_KV_W, BF16), shp(ATT_KV_W, BF16), shp(SSM_WIDTH, F32),
                   shp(2 * GLA_QK_W, F32), shp(GLA_V_W, F32), shp(GLA_V_W, F32), shp(GZ_PAD, F32)],
        compiler_params=_cparams(("arbitrary", "arbitrary")),
    )(ctx_src, x_src, mods, cos_t, sa_t, sb_t, w_in_bf16)


def _attn_kernel(sink_ref, q_ref, k_ref, v_ref, o_ref, *, blk0, nblk):
    for sub in range(ATT_QBLOCKS):
        _attn_block(sink_ref, q_ref, k_ref, v_ref, o_ref, sub, pl.program_id(1) * ATT_QBLOCKS + sub + blk0, nblk)


def _attn_block(sink_ref, q_ref, k_ref, v_ref, o_ref, sub, qb, nblk):
    w = ATT_BLOCK
    qrows = slice(sub * w, (sub + 1) * w)
    cb = CTX_LEN // w
    n = qb - cb
    span = 3 * w
    start = jnp.clip((n - 1) * w, 0, (nblk - cb) * w - span)
    band = pl.ds(pl.multiple_of(CTX_LEN + start, w), span)
    g = ATT_HEADS // ATT_KV_HEADS
    rows = g * w
    q_pos = n * w + lax.broadcasted_iota(jnp.int32, (rows, span), 0) % w
    rel = q_pos - (start + lax.broadcasted_iota(jnp.int32, (rows, span), 1))
    in_window = jnp.logical_and(jnp.logical_and(rel <= w, rel >= -w), n >= 0)
    rowg = lax.broadcasted_iota(jnp.int32, (rows, 1), 0) // w

    for kvh in range(ATT_KV_HEADS):
        sl = slice(kvh * HEAD_DIM, (kvh + 1) * HEAD_DIM)
        qs = jnp.concatenate(
            [q_ref[0, qrows, (kvh * g + a) * HEAD_DIM:(kvh * g + a + 1) * HEAD_DIM] for a in range(g)], axis=0)
        sink = jnp.zeros((rows, 1), F32)
        for a in range(g):
            sink = jnp.where(rowg == a, sink_ref[kvh * g + a], sink)
        s_b = jnp.where(in_window, _dot_nt(qs, k_ref[0, band, sl]), NEG_INF)
        s_x = _dot_nt(qs, k_ref[0, 0:CTX_LEN, sl])
        mx = jnp.maximum(jnp.maximum(jnp.max(s_b, axis=-1, keepdims=True), jnp.max(s_x, axis=-1, keepdims=True)), sink)
        p_b = jnp.exp(s_b - mx)
        p_x = jnp.exp(s_x - mx)
        den = jnp.sum(p_b, axis=-1, keepdims=True) + jnp.sum(p_x, axis=-1, keepdims=True) + jnp.exp(sink - mx)
        o = _dot(p_b.astype(BF16), v_ref[0, band, sl]) + _dot(p_x.astype(BF16), v_ref[0, 0:CTX_LEN, sl])
        o = o / den
        for a in range(g):
            hs = slice((kvh * g + a) * HEAD_DIM, (kvh * g + a + 1) * HEAD_DIM)
            o_ref[0, qrows, hs] = o[a * w:(a + 1) * w].astype(BF16)


def _attention(sink, q_all, k_all, v_all, with_ctx):
    bsz, nt, _ = q_all.shape
    w = ATT_BLOCK
    nblk = nt // w
    blk0 = 0 if with_ctx else CTX_LEN // w
    wq = w * ATT_QBLOCKS
    step0 = blk0 // ATT_QBLOCKS
    return pl.pallas_call(
        functools.partial(_attn_kernel, blk0=blk0, nblk=nblk),
        grid=(bsz, (nblk - blk0) // ATT_QBLOCKS),
        in_specs=[
            pl.BlockSpec(memory_space=pltpu.SMEM),
            pl.BlockSpec((1, wq, ATT_Q_W), lambda b, j: (b, j + step0, 0)),
            pl.BlockSpec((1, nt, ATT_KV_W), lambda b, j: (b, 0, 0)),
            pl.BlockSpec((1, nt, ATT_KV_W), lambda b, j: (b, 0, 0)),
        ],
        out_specs=pl.BlockSpec((1, wq, ATT_Q_W), lambda b, j: (b, j, 0)),
        out_shape=jax.ShapeDtypeStruct((bsz, (nblk - blk0) * w, ATT_Q_W), BF16),
        compiler_params=_cparams(("arbitrary", "arbitrary")),
    )(sink, q_all, k_all, v_all)


def _ssm_ops_kernel(lr_ref, li_ref, ldt_ref, lrc_ref, lic_ref, ldtc_ref, bre_ref, bim_ref, cret_ref, cimt_ref,
                    r1_ref, r2_ref, are_ref, aim_ref):
    t, h, p = SSM_T, SSM_GROUP, SSM_STATE
    th = t * h
    kpad = t + _ROWS8
    lane_lag = lax.broadcasted_iota(jnp.int32, (kpad, th), 1) // h
    krow = lax.broadcasted_iota(jnp.int32, (kpad, th), 0)
    pick = lambda lag_of_lane: jnp.where(krow == lag_of_lane, 1.0, 0.0).astype(F32)
    tile_h = jnp.where(lax.broadcasted_iota(jnp.int32, (h, th), 0)
                       == lax.broadcasted_iota(jnp.int32, (h, th), 1) % h, 1.0, 0.0).astype(F32)
    lane = lax.broadcasted_iota(jnp.int32, (h, th), 1)
    z, gs = [], []
    for d in range(2):
        lam_r, lam_i = lr_ref[d, 0], li_ref[d, 0]
        dt = jnp.exp(ldt_ref[d, 0])
        ldr, ldi = lam_r * dt, lam_i * dt
        mag = jnp.exp(ldr)
        lbr, lbi = mag * jnp.cos(ldi), mag * jnp.sin(ldi)
        den = lam_r * lam_r + lam_i * lam_i
        nr = lbr - 1.0
        cf_r = (nr * lam_r + lbi * lam_i) / den
        cf_i = (lbi * lam_r - nr * lam_i) / den
        br, bi = bre_ref[d, 0], bim_ref[d, 0]
        bbr = cf_r * br - cf_i * bi
        bbi = cf_r * bi + cf_i * br
        kcol = jnp.minimum(lax.broadcasted_iota(jnp.int32, (kpad, 1), 0), t).astype(F32)
        mk = jnp.exp(kcol * ldr)
        wr_all, wi_all = mk * jnp.cos(kcol * ldi), mk * jnp.sin(kcol * ldi)
        are_ref[d, 0] = wr_all[t:t + 1]
        aim_ref[d, 0] = wi_all[t:t + 1]
        gs.append([(bbr * wr_all[k:k + 1] - bbi * wi_all[k:k + 1], bbr * wi_all[k:k + 1] + bbi * wr_all[k:k + 1])
                   for k in range(t)])
        dtc = jnp.exp(ldtc_ref[d, 0])
        ldrc, ldic = lrc_ref[d, 0] * dtc, lic_ref[d, 0] * dtc
        krow_f = jnp.minimum(lax.broadcasted_iota(jnp.int32, (1, kpad), 1), t).astype(F32)
        mkc = jnp.exp(ldrc * krow_f)
        wrc, wic = mkc * jnp.cos(ldic * krow_f), mkc * jnp.sin(ldic * krow_f)
        ct_r = _dot(cret_ref[d, 0], tile_h, HIGHEST)
        ct_i = _dot(cimt_ref[d, 0], tile_h, HIGHEST)

        def c_lam(sel):
            er, ei = _dot(wrc, sel, HIGHEST), _dot(wic, sel, HIGHEST)
            return ct_r * er - ct_i * ei, ct_r * ei + ct_i * er

        lr_, li_ = c_lam(pick(lane_lag if d == 0 else t - 1 - lane_lag))
        z.append(_dot(bbr, lr_, HIGHEST) - _dot(bbi, li_, HIGHEST))
        er_, ei_ = c_lam(pick(lane_lag + 1 if d == 0 else t - lane_lag))
        r2_ref[0, (2 * d) * p:(2 * d + 1) * p, :] = er_.astype(BF16)
        r2_ref[0, (2 * d + 1) * p:(2 * d + 2) * p, :] = (-ei_).astype(BF16)
    zf, zb = z
    for s in range(t):
        fwd = zf if s == 0 else pltpu.roll(zf, s * h, 1)
        bwd = zb if s == t - 1 else pltpu.roll(zb, th - (t - 1 - s) * h, 1)
        taps = jnp.where(lane >= s * h, fwd, 0.0) + jnp.where(lane < (s + 1) * h, bwd, 0.0)
        inc = jnp.concatenate([gs[0][t - 1 - s][0], gs[0][t - 1 - s][1], gs[1][s][0], gs[1][s][1]], axis=1)
        r1_ref[0, s * h:(s + 1) * h, :] = jnp.concatenate([taps, inc], axis=1).astype(BF16)


def _ssm_ops(lam_re, lam_im, log_dt, b_re, b_im, c_re, c_im):
    g, p, h, t = SSM_GROUPS, SSM_STATE, SSM_GROUP, SSM_T
    row = lambda a: a.reshape(2, g, 1, p)
    col = lambda a: a.reshape(2, g, p, 1)
    ldt = jnp.broadcast_to(log_dt[:, :, None], (2, g, p))
    tr = lambda a: jnp.swapaxes(a, 2, 3)
    row_spec = pl.BlockSpec((2, 1, 1, p), lambda i: (0, i, 0, 0))
    col_spec = pl.BlockSpec((2, 1, p, 1), lambda i: (0, i, 0, 0))
    hp_spec = pl.BlockSpec((2, 1, h, p), lambda i: (0, i, 0, 0))
    ph_spec = pl.BlockSpec((2, 1, p, h), lambda i: (0, i, 0, 0))
    a_shape = jax.ShapeDtypeStruct((2, g, 1, p), F32)
    r1, r2, a_re, a_im = pl.pallas_call(
        _ssm_ops_kernel,
        grid=(g,),
        in_specs=[row_spec, row_spec, row_spec, col_spec, col_spec, col_spec, hp_spec, hp_spec, ph_spec, ph_spec],
        out_specs=[pl.BlockSpec((1, t * h, t * h + 4 * p), lambda i: (i, 0, 0)),
                   pl.BlockSpec((1, 4 * p, t * h), lambda i: (i, 0, 0)), row_spec, row_spec],
        out_shape=[jax.ShapeDtypeStruct((g, t * h, t * h + 4 * p), BF16),
                   jax.ShapeDtypeStruct((g, 4 * p, t * h), BF16), a_shape, a_shape],
        compiler_params=_cparams(("arbitrary",)),
    )(row(lam_re), row(lam_im), row(ldt), col(lam_re), col(lam_im), col(ldt), tr(b_re), tr(b_im), tr(c_re), tr(c_im))
    flat = lambda a: a.reshape(1, g * p)
    return r1, r2, flat(a_re[0]), flat(a_im[0]), flat(a_re[1]), flat(a_im[1])


_LANE_GROUPS = 128 // SSM_GROUP
_ROWS8 = 8


def _block_transpose(xs):
    h = SSM_GROUP
    blk = lax.broadcasted_iota(jnp.int32, xs[0].shape, 1) // h
    xs = list(xs)
    for d in (4, 2, 1):
        upper = (blk & d) != 0
        for i in range(_LANE_GROUPS):
            if i & d:
                continue
            a, b = xs[i], xs[i + d]
            xs[i] = jnp.where(upper, pltpu.roll(b, d * h, 1), a)
            xs[i + d] = jnp.where(upper, b, pltpu.roll(a, 128 - d * h, 1))
    return xs


def _to_groups_kernel(su_ref, ug_ref, *, nc):
    t, h = SSM_T, SSM_GROUP

    def body(rb, carry):
        base = pl.multiple_of(rb * (_ROWS8 * t), _ROWS8 * t)
        rows = pl.ds(pl.multiple_of(rb * _ROWS8, _ROWS8), _ROWS8)
        for j in range(t * h // 128):
            ps = [su_ref[0, pl.ds(base + _LANE_GROUPS * j + sm, _ROWS8, stride=t), :] for sm in range(_LANE_GROUPS)]
            for gm, tile in enumerate(_block_transpose(ps)):
                ug_ref[gm, rows, j * 128:(j + 1) * 128] = tile
        return carry

    lax.fori_loop(0, nc // _ROWS8, body, 0)


def _to_groups(su):
    bsz, nt, w = su.shape
    nc = nt // SSM_T
    return pl.pallas_call(
        functools.partial(_to_groups_kernel, nc=nc),
        grid=(bsz, w // 128),
        in_specs=[pl.BlockSpec((1, nt, 128), lambda b, a: (b, 0, a))],
        out_specs=pl.BlockSpec((_LANE_GROUPS, nc, SSM_T * SSM_GROUP), lambda b, a: (a, b, 0)),
        out_shape=jax.ShapeDtypeStruct((SSM_GROUPS, bsz * nc, SSM_T * SSM_GROUP), F32),
        compiler_params=_cparams(("arbitrary", "arbitrary")),
    )(su)


def _from_groups_kernel(yg_ref, y_ref, *, nc):
    t, h = SSM_T, SSM_GROUP

    def body(rb, carry):
        base = pl.multiple_of(rb * (_ROWS8 * t), _ROWS8 * t)
        rows = pl.ds(pl.multiple_of(rb * _ROWS8, _ROWS8), _ROWS8)
        for j in range(t * h // 128):
            tiles = [yg_ref[gm, rows, j * 128:(j + 1) * 128] for gm in range(_LANE_GROUPS)]
            for sm, tile in enumerate(_block_transpose(tiles)):
                y_ref[0, pl.ds(base + _LANE_GROUPS * j + sm, _ROWS8, stride=t), :] = tile
        return carry

    lax.fori_loop(0, nc // _ROWS8, body, 0)


def _from_groups(yg, bsz):
    g, n, th = yg.shape
    nc = n // bsz
    nt = nc * SSM_T
    return pl.pallas_call(
        functools.partial(_from_groups_kernel, nc=nc),
        grid=(bsz, g // _LANE_GROUPS),
        in_specs=[pl.BlockSpec((_LANE_GROUPS, nc, th), lambda b, a: (a, b, 0))],
        out_specs=pl.BlockSpec((1, nt, 128), lambda b, a: (b, 0, a)),
        out_shape=jax.ShapeDtypeStruct((bsz, nt, g * SSM_GROUP), F32),
        compiler_params=_cparams(("arbitrary", "arbitrary")),
    )(yg)


def _ssm_phase1_kernel(u_ref, r1_ref, y_ref, fr_ref, fi_ref, br_ref, bi_ref):
    th = SSM_T * SSM_GROUP
    p = SSM_STATE
    outs = [_dot(u_ref[k].astype(BF16), r1_ref[k]) for k in range(2)]
    for k in range(2):
        y_ref[k] = outs[k][:, 0:th]
    for idx, ref in enumerate((fr_ref, fi_ref, br_ref, bi_ref)):
        lo = th + idx * p
        ref[...] = jnp.concatenate([o[:, lo:lo + p] for o in outs], axis=1)


def _ssm_phase1(ug, r1):
    g, n, th = ug.shape
    p = SSM_STATE
    plane = pl.BlockSpec((n, 2 * p), lambda i: (0, i))
    plane_shape = jax.ShapeDtypeStruct((n, g * p), F32)
    return pl.pallas_call(
        _ssm_phase1_kernel,
        grid=(g // 2,),
        in_specs=[pl.BlockSpec((2, n, th), lambda i: (i, 0, 0)),
                  pl.BlockSpec((2, th, th + 4 * p), lambda i: (i, 0, 0))],
        out_specs=[pl.BlockSpec((2, n, th), lambda i: (i, 0, 0)), plane, plane, plane, plane],
        out_shape=[jax.ShapeDtypeStruct((g, n, th), F32), plane_shape, plane_shape, plane_shape, plane_shape],
        compiler_params=_cparams(("arbitrary",)),
    )(ug, r1)


def _ssm_scan_kernel(dfr_ref, dfi_ref, dbr_ref, dbi_ref, afr_ref, afi_ref, abr_ref, abi_ref,
                     sfr_ref, sfi_ref, sbr_ref, sbi_ref, *, nc, bsz):
    width = dfr_ref.shape[1]

    def run(dr_ref, di_ref, ar_ref, ai_ref, or_ref, oi_ref, chunk_of_step):
        ar = ar_ref[...]
        ai = ai_ref[...]

        def body(i, s):
            sr, si = s
            rows = pl.ds(chunk_of_step(i), bsz, stride=nc)
            or_ref[rows, :] = sr
            oi_ref[rows, :] = si
            return (ar * sr - ai * si + dr_ref[rows, :], ar * si + ai * sr + di_ref[rows, :])

        zero = jnp.zeros((bsz, width), F32)
        lax.fori_loop(0, nc, body, (zero, zero))

    run(dfr_ref, dfi_ref, afr_ref, afi_ref, sfr_ref, sfi_ref, lambda i: i)
    run(dbr_ref, dbi_ref, abr_ref, abi_ref, sbr_ref, sbi_ref,
        lambda i: jnp.where(i < SSM_NCTX, SSM_NCTX - 1 - i, nc + SSM_NCTX - 1 - i))


def _ssm_scan(planes, decays, bsz):
    n, width = planes[0].shape
    blk = pl.BlockSpec((n, 128), lambda i: (0, i))
    arow = pl.BlockSpec((1, 128), lambda i: (0, i))
    shape = jax.ShapeDtypeStruct((n, width), F32)
    return pl.pallas_call(
        functools.partial(_ssm_scan_kernel, nc=n // bsz, bsz=bsz),
        grid=(width // 128,),
        in_specs=[blk] * 4 + [arow] * 4,
        out_specs=[blk] * 4,
        out_shape=[shape] * 4,
        compiler_params=_cparams(("arbitrary",)),
    )(*planes, *decays)


def _ssm_phase2_kernel(y_ref, sfr_ref, sfi_ref, sbr_ref, sbi_ref, r2_ref, o_ref):
    p = SSM_STATE
    for k in range(2):
        acc = y_ref[k]
        for idx, ref in enumerate((sfr_ref, sfi_ref, sbr_ref, sbi_ref)):
            acc = acc + _dot(ref[:, k * p:(k + 1) * p].astype(BF16), r2_ref[k, idx * p:(idx + 1) * p, :])
        o_ref[k] = acc


def _ssm_phase2(yg, states, r2):
    g, n, th = yg.shape
    p = SSM_STATE
    plane = pl.BlockSpec((n, 2 * p), lambda i: (0, i))
    return pl.pallas_call(
        _ssm_phase2_kernel,
        grid=(g // 2,),
        in_specs=[pl.BlockSpec((2, n, th), lambda i: (i, 0, 0)), plane, plane, plane, plane,
                  pl.BlockSpec((2, 4 * p, th), lambda i: (i, 0, 0))],
        out_specs=pl.BlockSpec((2, n, th), lambda i: (i, 0, 0)),
        out_shape=jax.ShapeDtypeStruct((g, n, th), F32),
        compiler_params=_cparams(("arbitrary",)),
    )(yg, *states, r2)


def _ssm_mix(su_all, ops):
    r1, r2 = ops[0], ops[1]
    bsz = su_all.shape[0]
    ug = _to_groups(su_all)
    y1, *planes = _ssm_phase1(ug, r1)
    states = _ssm_scan(planes, ops[2:], bsz)
    return _from_groups(_ssm_phase2(y1, states, r2), bsz)


def _gla_direction(qk, v, z, wg, bg, s_ref, d, o_ref, reverse):
    tb = qk.shape[0]
    c_len = GLA_CHUNK
    x = _dot(z.astype(BF16), wg) + bg
    gate = (jnp.minimum(x, 0.0) - jnp.log(1.0 + jnp.exp(-jnp.abs(x)))) * (1.0 / GLA_TAU)
    ti = lax.broadcasted_iota(jnp.int32, (tb, tb), 0)
    si = lax.broadcasted_iota(jnp.int32, (tb, tb), 1)
    order = (si >= ti) if reverse else (si <= ti)
    tri = jnp.where(jnp.logical_and(order, ti // c_len == si // c_len), 1.0, 0.0).astype(F32)
    bcum_all = _dot(tri, gate, HIGHEST)
    q_all = (qk[:, 0:GLA_QK_W] * (GLA_DK ** -0.5) * jnp.exp(bcum_all)).astype(BF16)
    k_all = qk[:, GLA_QK_W:2 * GLA_QK_W]
    kin_all = k_all * jnp.exp(-bcum_all)
    v_bf = v.astype(BF16)
    nh = GLA_HEADS
    head_of = lambda shape, axis, size: lax.broadcasted_iota(jnp.int32, shape, axis) // size
    kmask = head_of((nh * c_len, GLA_QK_W), 0, c_len) == head_of((nh * c_len, GLA_QK_W), 1, GLA_DK)
    vmask = head_of((nh * c_len, GLA_V_W), 0, c_len) == head_of((nh * c_len, GLA_V_W), 1, GLA_DV)
    smask = head_of((GLA_V_W, GLA_QK_W), 0, GLA_DV) == head_of((GLA_V_W, GLA_QK_W), 1, GLA_DK)
    tq = lax.broadcasted_iota(jnp.int32, (c_len, nh * c_len), 0)
    sk = lax.broadcasted_iota(jnp.int32, (c_len, nh * c_len), 1) % c_len
    keep = (sk >= tq) if reverse else (sk <= tq)
    chunks = range(tb // c_len)
    for c in (reversed(chunks) if reverse else chunks):
        rs = slice(c * c_len, (c + 1) * c_len)
        bcum = bcum_all[rs]
        blast = bcum[0:1] if reverse else bcum[c_len - 1:c_len]
        k_up = (k_all[rs] * jnp.exp(blast - bcum)).astype(BF16)
        q_in = q_all[rs]
        k_exp = jnp.where(kmask, jnp.concatenate([kin_all[rs]] * nh, axis=0), 0.0).astype(BF16)
        v_exp = jnp.where(vmask, jnp.concatenate([v[rs]] * nh, axis=0), 0.0).astype(BF16)
        att = jnp.where(keep, _dot_nt(q_in, k_exp), 0.0)
        st = s_ref[d]
        o_ref[0, rs, :] = _dot(att.astype(BF16), v_exp) + _dot_nt(q_in, st.astype(BF16))
        s_ref[d] = st * jnp.exp(blast) + jnp.where(smask, _dot_tn(v_bf[rs], k_up), 0.0)


def _gla_kernel(qkf_ref, vf_ref, zf_ref, qkb_ref, vb_ref, zb_ref, wg_ref, bg_ref, of_ref, ob_ref, s_ref):
    @pl.when(pl.program_id(1) == 0)
    def _():
        s_ref[...] = jnp.zeros_like(s_ref)

    _gla_direction(qkf_ref[0], vf_ref[0], zf_ref[0], wg_ref[0], bg_ref[0], s_ref, 0, of_ref, False)
    _gla_direction(qkb_ref[0], vb_ref[0], zb_ref[0], wg_ref[1], bg_ref[1], s_ref, 1, ob_ref, True)


def _gla(gqk, gv, gz, wg_pad, bg):
    bsz, nt, _ = gqk.shape
    tb = TOKEN_TILE
    tiles = nt // tb
    fwd = lambda b, i: (b, i, 0)
    bwd = lambda b, i: (b, jnp.where(i == 0, 0, tiles - i), 0)
    spec = lambda width, im: pl.BlockSpec((1, tb, width), im)
    return pl.pallas_call(
        _gla_kernel,
        grid=(bsz, tiles),
        in_specs=[spec(2 * GLA_QK_W, fwd), spec(GLA_V_W, fwd), spec(GZ_PAD, fwd),
                  spec(2 * GLA_QK_W, bwd), spec(GLA_V_W, bwd), spec(GZ_PAD, bwd),
                  pl.BlockSpec((2, GZ_PAD, GLA_QK_W), lambda b, i: (0, 0, 0)),
                  pl.BlockSpec((2, 1, GLA_QK_W), lambda b, i: (0, 0, 0))],
        out_specs=[spec(GLA_V_W, fwd), spec(GLA_V_W, bwd)],
        out_shape=[jax.ShapeDtypeStruct((bsz, nt, GLA_V_W), F32), jax.ShapeDtypeStruct((bsz, nt, GLA_V_W), F32)],
        scratch_shapes=[pltpu.VMEM((2, GLA_V_W, GLA_QK_W), F32)],
        compiler_params=_cparams(("arbitrary", "arbitrary")),
    )(gqk, gv, gz, gqk, gv, gz, wg_pad, bg)


def _outproj_kernel(ctx_ref, x_ref, m_ref, att_ref, y_ref, u_ref, of_ref, ob_ref, r_ref,
                    ssmd_ref, wglu_ref, bglu_ref, gng_ref, wo_ref, ln1g_ref, ln1b_ref, router_ref,
                    x1_ref, h2_ref, aff_ref, *, t0):
    nb, tm, d = x_ref.shape
    x_in = jnp.where(pl.program_id(1) + t0 == 0, ctx_ref[...], x_ref[...])
    rows = nb * tm
    flat = lambda ref: ref[...].reshape(rows, ref.shape[-1])
    m = m_ref[0]
    zin = flat(y_ref) + ssmd_ref[...] * flat(u_ref)
    z = 0.5 * zin * (1.0 + jnp.tanh(math.sqrt(2.0 / math.pi) * (zin + 0.044715 * (zin * zin * zin))))
    ssm = z * _sigmoid(_dot(z.astype(BF16), wglu_ref[...]) + bglu_ref[...])
    o = flat(of_ref) + flat(ob_ref)
    r = flat(r_ref)
    gng = gng_ref[...]
    mix = [flat(att_ref), ssm.astype(BF16)]
    for hd in range(GLA_HEADS):
        vs = slice(hd * GLA_DV, (hd + 1) * GLA_DV)
        oh = o[:, vs]
        rh = r[:, vs]
        oh = oh * lax.rsqrt(jnp.mean(oh * oh, axis=-1, keepdims=True) + LN_EPS) * gng
        mix.append((oh * (rh * _sigmoid(rh))).astype(BF16))
    proj = _dot(jnp.concatenate(mix, axis=1), wo_ref[...]).reshape(nb, tm, d)
    x1 = _ln(ALPHA * x_in + m[:, 2:3] * proj) * ln1g_ref[...] + ln1b_ref[...]
    x1_ref[...] = x1
    h2 = (_ln(x1) * (1.0 + m[:, 4:5]) + m[:, 3:4]).reshape(rows, d)
    h_hi = h2.astype(BF16)
    h2_ref[...] = h_hi.reshape(nb, tm, d)
    h_lo = (h2 - h_hi.astype(F32)).astype(BF16)
    l_hi = _dot(h_hi, router_ref[...])
    logits = l_hi + pltpu.roll(l_hi, 128 - N_EXPERTS, 1) + _dot(h_lo, router_ref[...])
    col = lax.broadcasted_iota(jnp.int32, logits.shape, 1)
    logits = jnp.where(col < N_EXPERTS, logits, NEG_INF)
    e = jnp.exp(logits - jnp.max(logits, axis=-1, keepdims=True))
    aff_ref[...] = (e / jnp.sum(e, axis=-1, keepdims=True)).reshape(nb, tm, 128)


def _outproj(ctx_src, x_src, x_off, mods, att, y, su, o_f, o_b, gr, ssm_d, w_glu, b_glu, gn_g, w_out, ln1_g, ln1_b,
             router_split, with_ctx):
    bsz, nt, _ = y.shape
    d = x_src.shape[-1]
    tm = TOKEN_TILE
    nb = BATCH_TILE
    t0 = 0 if with_ctx else CTX_LEN // tm
    tiles = nt // tm - t0
    att_t0 = 0 if with_ctx else -t0
    tok_in = lambda width: pl.BlockSpec((nb, tm, width), lambda b, i: (b, i + t0, 0))
    tok_out = lambda width: pl.BlockSpec((nb, tm, width), lambda b, i: (b, i, 0))
    const = lambda shape: pl.BlockSpec(shape, lambda b, i: tuple(0 for _ in shape))
    once = lambda shape: pl.BlockSpec(shape, lambda b, i: tuple(0 for _ in shape), pipeline_mode=pl.Buffered(1))
    shp = lambda width, dt: jax.ShapeDtypeStruct((bsz, tiles * tm, width), dt)
    return pl.pallas_call(
        functools.partial(_outproj_kernel, t0=t0),
        grid=(bsz // nb, tiles),
        in_specs=[
            pl.BlockSpec((nb, tm, d), lambda b, i: (b, 0, 0)),
            pl.BlockSpec((nb, tm, d), lambda b, i: (b, jnp.maximum(i + t0 - x_off, 0), 0)),
            pl.BlockSpec((1, nb, N_MOD, d), lambda b, i: (jnp.minimum(i + t0, 1), b, 0, 0)),
            pl.BlockSpec((nb, tm, ATT_Q_W), lambda b, i: (b, i + t0 + att_t0, 0)),
            tok_in(SSM_WIDTH), tok_in(SSM_WIDTH), tok_in(GLA_V_W), tok_in(GLA_V_W), tok_in(GLA_V_W),
            const((1, SSM_WIDTH)), const((SSM_WIDTH, SSM_WIDTH)), const((1, SSM_WIDTH)), const((1, GLA_DV)),
            once((d, d)), const((1, d)), const((1, d)), const((d, 128)),
        ],
        out_specs=[tok_out(d), tok_out(d), tok_out(128)],
        out_shape=[shp(d, F32), shp(d, BF16), shp(128, F32)],
        compiler_params=_cparams(("arbitrary", "arbitrary")),
    )(ctx_src, x_src, mods, att, y, su, o_f, o_b, gr, ssm_d, w_glu, b_glu, gn_g, w_out, ln1_g, ln1_b, router_split)


def _expert_kernel(x_ref, g_ref, wg_ref, wu_ref, wd_ref, o_ref, hid_ref, *, nf, tf):
    s = pl.program_id(1)

    @pl.when(s < nf)
    def _():
        x = x_ref[0]
        a = _dot(x, wg_ref[0, 0].astype(BF16))
        u = _dot(x, wu_ref[0, 0].astype(BF16))
        col = pl.multiple_of(s * tf, tf)
        hid_ref[:, pl.ds(col, tf)] = ((a * _sigmoid(a)) * u).astype(BF16)

    @pl.when(s >= nf)
    def _():
        o_ref[0] = (_dot(hid_ref[...], wd_ref[0, 0].astype(BF16)) * g_ref[0]).astype(BF16)


def _experts(xe, ge, w_gate, w_up, w_down, layer, e0):
    e, r, d = xe.shape
    ff = w_gate.shape[3]
    tf = 256
    nf = ff // tf
    nd = d // tf
    up = lambda i, s: (layer, e0 + i, 0, jnp.minimum(s, nf - 1))
    down = lambda i, s: (layer, e0 + i, 0, jnp.maximum(s - nf, 0))
    return pl.pallas_call(
        functools.partial(_expert_kernel, nf=nf, tf=tf),
        grid=(e, nf + nd),
        in_specs=[pl.BlockSpec((1, r, d), lambda i, s: (i, 0, 0)),
                  pl.BlockSpec((1, r, 1), lambda i, s: (i, 0, 0)),
                  pl.BlockSpec((1, 1, d, tf), up),
                  pl.BlockSpec((1, 1, d, tf), up),
                  pl.BlockSpec((1, 1, ff, tf), down)],
        out_specs=pl.BlockSpec((1, r, tf), lambda i, s: (i, 0, jnp.maximum(s - nf, 0))),
        out_shape=jax.ShapeDtypeStruct((e, r, d), BF16),
        scratch_shapes=[pltpu.VMEM((r, ff), BF16)],
        compiler_params=_cparams(("arbitrary", "arbitrary")),
    )(xe, ge, w_gate, w_up, w_down)


COMBINE_WIN = 64
COMBINE_ALIGN = 16


def _combine_kernel(lo_ref, hi_ref, x1_ref, m_ref, g_ref, b_ref, tok_hbm, *rest, tiles, nsteps, group_rows):
    ye_groups, (o_ref, buf, tokbuf, sem) = rest[:-4], rest[-4:]
    _combine_body(lo_ref, hi_ref, x1_ref, m_ref, g_ref, b_ref, tok_hbm, ye_groups, o_ref, buf, tokbuf, sem,
                  tiles, nsteps, group_rows)


def _combine_body(lo_ref, hi_ref, x1_ref, m_ref, g_ref, b_ref, tok_hbm, ye_groups, o_ref, buf, tokbuf, sem,
                  tiles, nsteps, group_rows):
    n_exp = N_EXPERTS
    win = COMBINE_WIN
    tm = x1_ref.shape[1]
    step = pl.program_id(0) * tiles + pl.program_id(1)
    slot = step % 2
    extra = 2

    per_group = n_exp // len(ye_groups)

    def first_row(st, e, k):
        u = (lo_ref[st * n_exp + e] // COMBINE_ALIGN) * COMBINE_ALIGN + k * win
        group_end = (e // per_group + 1) * group_rows
        return u, pl.multiple_of(jnp.minimum(u, group_end - win), COMBINE_ALIGN)

    def copies(st, k, dst):
        out = []
        for e in range(n_exp):
            s0 = first_row(st, e, k)[1]
            local = pl.ds(pl.multiple_of(s0 - (e // per_group) * group_rows, COMBINE_ALIGN), win)
            out.append(pltpu.make_async_copy(ye_groups[e // per_group].at[local, :], buf.at[dst, e], sem.at[dst]))
            out.append(pltpu.make_async_copy(tok_hbm.at[pl.ds(s0, win), :], tokbuf.at[dst, e], sem.at[dst]))
        return out

    @pl.when(step == 0)
    def _():
        for cp in copies(step, 0, slot):
            cp.start()

    @pl.when(step + 1 < nsteps)
    def _():
        for cp in copies(step + 1, 0, 1 - slot):
            cp.start()

    tile_tok = pl.program_id(1) * tm + lax.broadcasted_iota(jnp.int32, (win, tm), 1)
    jrow = lax.broadcasted_iota(jnp.int32, (win, 1), 0)

    def spread(k, src):
        parts = []
        for e in range(n_exp):
            u, s0 = first_row(step, e, k)
            row = s0 + jrow
            mine = jnp.logical_and(row >= jnp.maximum(u, lo_ref[step * n_exp + e]), row < hi_ref[step * n_exp + e])
            hit = jnp.logical_and(tokbuf[src, e][:, 0:1] == tile_tok, mine)
            parts.append(jnp.where(hit, 1.0, 0.0).astype(BF16))
        onehot_t = jnp.concatenate(parts, axis=0)
        return _dot_tn(onehot_t, buf[src].reshape(n_exp * win, buf.shape[-1]))

    for cp in copies(step, 0, slot):
        cp.wait()
    acc = spread(0, slot)

    rounds = jnp.int32(1)
    for e in range(n_exp):
        span = hi_ref[step * n_exp + e] - (lo_ref[step * n_exp + e] // COMBINE_ALIGN) * COMBINE_ALIGN
        rounds = jnp.maximum(rounds, (span + win - 1) // win)

    def more(k, acc):
        for cp in copies(step, k, extra):
            cp.start()
        for cp in copies(step, k, extra):
            cp.wait()
        return acc + spread(k, extra)

    acc = lax.fori_loop(1, rounds, more, acc)
    m = m_ref[0, 0]
    o_ref[0] = _ln(ALPHA * x1_ref[0] + m[5:6] * acc) * g_ref[...] + b_ref[...]


def _combine(lo, hi, x1, mods, ln2_g, ln2_b, tok_of_row, ye_groups, with_ctx):
    bsz, nt, d = x1.shape
    tm = TOKEN_TILE
    tiles = nt // tm
    kind0 = 0 if with_ctx else 1
    tok = lambda width: pl.BlockSpec((1, tm, width), lambda b, i, lo, hi: (b, i, 0))
    vec = pl.BlockSpec((1, d), lambda b, i, lo, hi: (0, 0))
    hbm = pl.BlockSpec(memory_space=pl.ANY)
    return pl.pallas_call(
        functools.partial(_combine_kernel, tiles=tiles, nsteps=bsz * tiles, group_rows=ye_groups[0].shape[0]),
        grid_spec=pltpu.PrefetchScalarGridSpec(
            num_scalar_prefetch=2,
            grid=(bsz, tiles),
            in_specs=[tok(d),
                      pl.BlockSpec((1, 1, N_MOD, d), lambda b, i, lo, hi: (jnp.minimum(i + kind0, 1), b, 0, 0)),
                      vec, vec, hbm] + [hbm] * len(ye_groups),
            out_specs=tok(d),
            scratch_shapes=[pltpu.VMEM((3, N_EXPERTS, COMBINE_WIN, d), BF16),
                            pltpu.VMEM((3, N_EXPERTS, COMBINE_WIN, 128), jnp.int32),
                            pltpu.SemaphoreType.DMA((3,))],
        ),
        out_shape=jax.ShapeDtypeStruct((bsz, nt, d), F32),
        compiler_params=_cparams(("arbitrary", "arbitrary")),
    )(lo, hi, x1, mods, ln2_g, ln2_b, tok_of_row, *ye_groups)


def _route(aff, cap):
    gate, idx = lax.top_k(jnp.swapaxes(aff, 1, 2), cap)
    return gate, idx


def _rope_tables(seq):
    half = HEAD_DIM // 2
    nf = half // 2
    inv = ROPE_BASE ** (-jnp.arange(nf, dtype=F32) / nf)
    pos = jnp.arange(seq)
    ang_r = (pos // GRID_W).astype(F32)[:, None] * inv
    ang_c = (pos % GRID_W).astype(F32)[:, None] * inv
    zeros = jnp.zeros_like(ang_r)
    cos = jnp.concatenate([jnp.cos(ang_r), jnp.cos(ang_r), jnp.cos(ang_c), jnp.cos(ang_c)], axis=-1)
    sa = jnp.concatenate([-jnp.sin(ang_r), zeros, -jnp.sin(ang_c), zeros], axis=-1)
    sb = jnp.concatenate([zeros, jnp.sin(ang_r), zeros, jnp.sin(ang_c)], axis=-1)
    ctx = lambda fill: jnp.full((CTX_LEN, HEAD_DIM), fill, F32)
    return (jnp.concatenate([ctx(1.0), cos], axis=0), jnp.concatenate([ctx(0.0), sa], axis=0),
            jnp.concatenate([ctx(0.0), sb], axis=0))


def kernel(x, c, ctx, c_ctx, w_ada, b_ada, w_in, attn_sink, ssm_lam_re, ssm_lam_im, ssm_log_dt,
           ssm_b_re, ssm_b_im, ssm_c_re, ssm_c_im, ssm_d, ssm_w_glu, ssm_b_glu, gla_w_gate,
           gla_b_gate, gla_norm_g, w_out, ln1_g, ln1_b, ln2_g, ln2_b, router, exp_w_gate,
           exp_w_up, exp_w_down):
    bsz, seq, d = x.shape
    n_exp = router.shape[-1]
    cc = jnp.zeros((8, d), F32).at[:bsz].set(c).at[bsz].set(c_ctx)
    mod_all = _ada(cc, w_ada, b_ada)
    cos_t, sa_t, sb_t = _rope_tables(seq)
    ctx_src, x_src, x_off = ctx, x, CTX_LEN // TOKEN_TILE
    nt = CTX_LEN + seq
    bidx = jnp.arange(bsz)[None, :, None]

    for l in range(DEPTH):
        with_ctx = l < DEPTH - 1
        m = mod_all[l].reshape(8, N_MOD, d)
        mods = jnp.stack([jnp.broadcast_to(m[bsz], (bsz, N_MOD, d)), m[:bsz]], axis=0)
        w_in_p = jnp.pad(w_in[l], ((0, 0), (0, N_IN_PAD - N_IN))).astype(BF16)
        q, k, v, su, gqk, gv, gr, gz = _inproj(ctx_src, x_src, x_off, nt, mods, cos_t, sa_t, sb_t, w_in_p)

        att = _attention(attn_sink[l], q, k, v, with_ctx)

        ops = _ssm_ops(ssm_lam_re[l], ssm_lam_im[l], ssm_log_dt[l], ssm_b_re[l], ssm_b_im[l], ssm_c_re[l], ssm_c_im[l])
        y = _ssm_mix(su, ops)

        wg_pad = jnp.zeros((2, GZ_PAD, GLA_QK_W), F32)
        wg_pad = wg_pad.at[0, 0:GLA_RANK].set(gla_w_gate[l, 0]).at[1, GLA_RANK:2 * GLA_RANK].set(gla_w_gate[l, 1])
        o_f, o_b = _gla(gqk, gv, gz, wg_pad.astype(BF16), gla_b_gate[l][:, None, :])

        r_hi = router[l].astype(BF16)
        r_lo = (router[l] - r_hi.astype(F32)).astype(BF16)
        router_split = jnp.pad(jnp.concatenate([r_hi, r_lo], axis=1), ((0, 0), (0, 128 - 2 * n_exp)))
        x1, h2, aff = _outproj(ctx_src, x_src, x_off, mods, att, y, su, o_f, o_b, gr,
                               ssm_d[l][None], ssm_w_glu[l].astype(BF16), ssm_b_glu[l][None], gla_norm_g[l][None],
                               w_out[l].astype(BF16), ln1_g[l][None], ln1_b[l][None], router_split, with_ctx)

        t0 = CTX_LEN if with_ctx else 0
        cap = CAPACITY_FACTOR * seq // n_exp
        gate, idx = _route(aff[:, t0:, :n_exp], cap)
        idx = idx + t0
        if with_ctx:
            cap_c = CAPACITY_FACTOR * CTX_LEN // n_exp
            gate_c, idx_c = _route(aff[:, :CTX_LEN, :n_exp], cap_c)
            gate = jnp.concatenate([gate, gate_c], axis=-1)
            idx = jnp.concatenate([idx, idx_c], axis=-1)
        idx, gate = lax.sort((idx, gate), dimension=-1, num_keys=1)
        idx_e = jnp.swapaxes(idx, 0, 1)
        gate_e = jnp.swapaxes(gate, 0, 1)
        slots = idx_e.shape[-1]
        rows_e = bsz * slots
        ye_groups = []
        eg = n_exp // EXPERT_GROUPS
        for grp in range(EXPERT_GROUPS):
            sl = slice(grp * eg, (grp + 1) * eg)
            xe = h2[bidx, idx_e[sl]].reshape(eg, rows_e, d)
            ye = _experts(xe, gate_e[sl].reshape(eg, rows_e, 1), exp_w_gate, exp_w_up, exp_w_down, l, grp * eg)
            ye_groups.append(ye.reshape(eg * rows_e, d))
        nt_out = x1.shape[1]
        bounds = jnp.arange(nt_out // TOKEN_TILE + 1) * TOKEN_TILE
        cnt = jnp.sum(idx_e[..., None] < bounds, axis=2).astype(jnp.int32)
        base = (jnp.arange(n_exp)[:, None, None] * rows_e + jnp.arange(bsz)[None, :, None] * slots).astype(jnp.int32)
        lo = jnp.transpose(base + cnt[:, :, :-1], (1, 2, 0)).reshape(-1)
        hi = jnp.transpose(base + cnt[:, :, 1:], (1, 2, 0)).reshape(-1)
        tok_of_row = jnp.broadcast_to(idx_e.reshape(n_exp * rows_e, 1).astype(jnp.int32), (n_exp * rows_e, 128))
        x_new = _combine(lo, hi, x1, mods, ln2_g[l][None], ln2_b[l][None], tok_of_row, ye_groups, with_ctx)
        ctx_src, x_src, x_off = x_new, x_new, 0
    return x_new
```

```python
import functools
import math

import jax
import jax.numpy as jnp
from jax import lax
from jax.experimental import pallas as pl
from jax.experimental.pallas import tpu as pltpu

F32 = jnp.float32
BF16 = jnp.bfloat16
HIGHEST = lax.Precision.HIGHEST

D_MODEL = 2048
DEPTH = 2
GRID_W = 64
CTX_LEN = 256
ATT_HEADS = 8
ATT_KV_HEADS = 2
HEAD_DIM = 128
ATT_BLOCK = 128
ATT_QBLOCKS = 2
ROPE_BASE = 10000.0
SSM_WIDTH = 512
SSM_GROUP = 16
SSM_GROUPS = 32
SSM_STATE = 64
GLA_HEADS = 4
GLA_DK = 64
GLA_DV = 128
GLA_RANK = 16
GLA_TAU = 16.0
GLA_CHUNK = 64
N_EXPERTS = 16
EXPERT_FF = 2048
CAPACITY_FACTOR = 2
N_MOD = 6
ALPHA = (2 * DEPTH) ** 0.25
LN_EPS = 1e-6
NEG_INF = -1e30

ATT_Q_W = ATT_HEADS * HEAD_DIM
ATT_KV_W = ATT_KV_HEADS * HEAD_DIM
GLA_QK_W = GLA_HEADS * GLA_DK
GLA_V_W = GLA_HEADS * GLA_DV
N_IN = 3616
N_IN_PAD = 3712
GZ_PAD = N_IN_PAD - 3584

TOKEN_TILE = 256
BATCH_TILE = 2
EXPERT_GROUPS = 2
SSM_T = 32
SSM_NCTX = CTX_LEN // SSM_T
VMEM_LIMIT = 56 * 1024 * 1024


def _cparams(sem, vmem=VMEM_LIMIT):
    return pltpu.CompilerParams(dimension_semantics=sem, vmem_limit_bytes=vmem)


def _dot(a, b, precision=None):
    return jnp.dot(a, b, preferred_element_type=F32, precision=precision)


def _dot_nt(a, b, precision=None):
    return lax.dot_general(a, b, (((1,), (1,)), ((), ())), preferred_element_type=F32, precision=precision)


def _dot_tn(a, b, precision=None):
    return lax.dot_general(a, b, (((0,), (0,)), ((), ())), preferred_element_type=F32, precision=precision)


def _sigmoid(x):
    return 1.0 / (1.0 + jnp.exp(-x))


def _ln(x):
    mu = jnp.mean(x, axis=-1, keepdims=True)
    xc = x - mu
    var = jnp.mean(xc * xc, axis=-1, keepdims=True)
    return xc * lax.rsqrt(var + LN_EPS)


def _ada_kernel(c_ref, w_ref, b_ref, o_ref):
    c = c_ref[...]
    s = c * _sigmoid(c)
    w = w_ref[0]
    w_hi = w.astype(BF16)
    w_lo = (w - w_hi.astype(F32)).astype(BF16)
    s_hi = s.astype(BF16)
    s_lo = (s - s_hi.astype(F32)).astype(BF16)
    both = _dot(jnp.concatenate([s_hi, s_lo], axis=0), w_hi)
    rows = s.shape[0]
    o_ref[0] = both[0:rows] + both[rows:2 * rows] + _dot(s_hi, w_lo) + b_ref[0]


def _ada(cc, w_ada, b_ada):
    depth, d, n = w_ada.shape
    tn = 1024
    return pl.pallas_call(
        _ada_kernel,
        grid=(depth, n // tn),
        in_specs=[
            pl.BlockSpec((8, d), lambda l, j: (0, 0)),
            pl.BlockSpec((1, d, tn), lambda l, j: (l, 0, j)),
            pl.BlockSpec((1, 1, tn), lambda l, j: (l, 0, j)),
        ],
        out_specs=pl.BlockSpec((1, 8, tn), lambda l, j: (l, 0, j)),
        out_shape=jax.ShapeDtypeStruct((depth, 8, n), F32),
        compiler_params=_cparams(("arbitrary", "arbitrary")),
    )(cc, w_ada, b_ada.reshape(depth, 1, n))


def _inproj_kernel(ctx_ref, x_ref, m_ref, cos_ref, sa_ref, sb_ref, w_ref,
                   q_ref, k_ref, v_ref, su_ref, gqk_ref, gv_ref, gr_ref, gz_ref):
    nb, tm, d = x_ref.shape
    rows = nb * tm
    m = m_ref[0]
    x = jnp.where(pl.program_id(1) == 0, ctx_ref[...], x_ref[...])
    h = _ln(x) * (1.0 + m[:, 1:2]) + m[:, 0:1]
    h = h.reshape(rows, d).astype(BF16)
    tile = lambda ref: jnp.concatenate([ref[...]] * nb, axis=0)
    cos = tile(cos_ref)
    sa = tile(sa_ref)
    sb = tile(sb_ref)

    def rope(t):
        return t * cos + pltpu.roll(t, 96, 1) * sa + pltpu.roll(t, 32, 1) * sb

    def put(ref, val, sl=slice(None)):
        ref[:, :, sl] = val.reshape(nb, tm, val.shape[-1])

    scale = HEAD_DIM ** -0.5
    q = _dot(h, w_ref[:, 0:ATT_Q_W])
    for hd in range(ATT_HEADS):
        sl = slice(hd * HEAD_DIM, (hd + 1) * HEAD_DIM)
        put(q_ref, (rope(q[:, sl]) * scale).astype(BF16), sl)
    kk = _dot(h, w_ref[:, 1024:1280])
    for hd in range(ATT_KV_HEADS):
        sl = slice(hd * HEAD_DIM, (hd + 1) * HEAD_DIM)
        put(k_ref, rope(kk[:, sl]).astype(BF16), sl)
    put(v_ref, _dot(h, w_ref[:, 1280:1536]).astype(BF16))
    put(su_ref, _dot(h, w_ref[:, 1536:2048]))
    put(gqk_ref, _dot(h, w_ref[:, 2048:2560]))
    put(gv_ref, _dot(h, w_ref[:, 2560:3072]))
    put(gr_ref, _dot(h, w_ref[:, 3072:3584]))
    put(gz_ref, _dot(h, w_ref[:, 3584:N_IN_PAD]))


def _inproj(ctx_src, x_src, x_off, nt, mods, cos_t, sa_t, sb_t, w_in_bf16):
    bsz, _, d = x_src.shape
    tm = TOKEN_TILE
    nb = BATCH_TILE
    tiles = nt // tm
    tok = lambda width: pl.BlockSpec((nb, tm, width), lambda b, i: (b, i, 0))
    tab = pl.BlockSpec((tm, HEAD_DIM), lambda b, i: (i, 0))
    shp = lambda width, dt: jax.ShapeDtypeStruct((bsz, nt, width), dt)
    return pl.pallas_call(
        _inproj_kernel,
        grid=(bsz // nb, tiles),
        in_specs=[
            pl.BlockSpec((nb, tm, d), lambda b, i: (b, 0, 0)),
            pl.BlockSpec((nb, tm, d), lambda b, i: (b, jnp.maximum(i - x_off, 0), 0)),
            pl.BlockSpec((1, nb, N_MOD, d), lambda b, i: (jnp.minimum(i, 1), b, 0, 0)),
            tab, tab, tab,
            pl.BlockSpec((d, N_IN_PAD), lambda b, i: (0, 0), pipeline_mode=pl.Buffered(1)),
        ],
        out_specs=[tok(ATT_Q_W), tok(ATT_KV_W), tok(ATT_KV_W), tok(SSM_WIDTH),
                   tok(2 * GLA_QK_W), tok(GLA_V_W), tok(GLA_V_W), tok(GZ_PAD)],
        out_shape=[shp(ATT_Q_W, BF16), shp(ATT_KV_W, BF16), shp(ATT_KV_W, BF16), shp(SSM_WIDTH, F32),
                   shp(2 * GLA_QK_W, F32), shp(GLA_V_W, F32), shp(GLA_V_W, F32), shp(GZ_PAD, F32)],
        compiler_params=_cparams(("arbitrary", "arbitrary")),
    )(ctx_src, x_src, mods, cos_t, sa_t, sb_t, w_in_bf16)


def _attn_kernel(sink_ref, q_ref, k_ref, v_ref, o_ref, *, blk0, nblk):
    for sub in range(ATT_QBLOCKS):
        _attn_block(sink_ref, q_ref, k_ref, v_ref, o_ref, sub, pl.program_id(1) * ATT_QBLOCKS + sub + blk0, nblk)


def _attn_block(sink_ref, q_ref, k_ref, v_ref, o_ref, sub, qb, nblk):
    w = ATT_BLOCK
    qrows = slice(sub * w, (sub + 1) * w)
    cb = CTX_LEN // w
    n = qb - cb
    span = 3 * w
    start = jnp.clip((n - 1) * w, 0, (nblk - cb) * w - span)
    band = pl.ds(pl.multiple_of(CTX_LEN + start, w), span)
    g = ATT_HEADS // ATT_KV_HEADS
    rows = g * w
    q_pos = n * w + lax.broadcasted_iota(jnp.int32, (rows, span), 0) % w
    rel = q_pos - (start + lax.broadcasted_iota(jnp.int32, (rows, span), 1))
    in_window = jnp.logical_and(jnp.logical_and(rel <= w, rel >= -w), n >= 0)
    rowg = lax.broadcasted_iota(jnp.int32, (rows, 1), 0) // w

    for kvh in range(ATT_KV_HEADS):
        sl = slice(kvh * HEAD_DIM, (kvh + 1) * HEAD_DIM)
        qs = jnp.concatenate(
            [q_ref[0, qrows, (kvh * g + a) * HEAD_DIM:(kvh * g + a + 1) * HEAD_DIM] for a in range(g)], axis=0)
        sink = jnp.zeros((rows, 1), F32)
        for a in range(g):
            sink = jnp.where(rowg == a, sink_ref[kvh * g + a], sink)
        s_b = jnp.where(in_window, _dot_nt(qs, k_ref[0, band, sl]), NEG_INF)
        s_x = _dot_nt(qs, k_ref[0, 0:CTX_LEN, sl])
        mx = jnp.maximum(jnp.maximum(jnp.max(s_b, axis=-1, keepdims=True), jnp.max(s_x, axis=-1, keepdims=True)), sink)
        p_b = jnp.exp(s_b - mx)
        p_x = jnp.exp(s_x - mx)
        den = jnp.sum(p_b, axis=-1, keepdims=True) + jnp.sum(p_x, axis=-1, keepdims=True) + jnp.exp(sink - mx)
        o = _dot(p_b.astype(BF16), v_ref[0, band, sl]) + _dot(p_x.astype(BF16), v_ref[0, 0:CTX_LEN, sl])
        o = o / den
        for a in range(g):
            hs = slice((kvh * g + a) * HEAD_DIM, (kvh * g + a + 1) * HEAD_DIM)
            o_ref[0, qrows, hs] = o[a * w:(a + 1) * w].astype(BF16)


def _attention(sink, q_all, k_all, v_all, with_ctx):
    bsz, nt, _ = q_all.shape
    w = ATT_BLOCK
    nblk = nt // w
    blk0 = 0 if with_ctx else CTX_LEN // w
    wq = w * ATT_QBLOCKS
    step0 = blk0 // ATT_QBLOCKS
    return pl.pallas_call(
        functools.partial(_attn_kernel, blk0=blk0, nblk=nblk),
        grid=(bsz, (nblk - blk0) // ATT_QBLOCKS),
        in_specs=[
            pl.BlockSpec(memory_space=pltpu.SMEM),
            pl.BlockSpec((1, wq, ATT_Q_W), lambda b, j: (b, j + step0, 0)),
            pl.BlockSpec((1, nt, ATT_KV_W), lambda b, j: (b, 0, 0)),
            pl.BlockSpec((1, nt, ATT_KV_W), lambda b, j: (b, 0, 0)),
        ],
        out_specs=pl.BlockSpec((1, wq, ATT_Q_W), lambda b, j: (b, j, 0)),
        out_shape=jax.ShapeDtypeStruct((bsz, (nblk - blk0) * w, ATT_Q_W), BF16),
        compiler_params=_cparams(("arbitrary", "arbitrary")),
    )(sink, q_all, k_all, v_all)


def _ssm_ops_kernel(lr_ref, li_ref, ldt_ref, lrc_ref, lic_ref, ldtc_ref, bre_ref, bim_ref, cret_ref, cimt_ref,
                    r1_ref, r2_ref, are_ref, aim_ref):
    t, h, p = SSM_T, SSM_GROUP, SSM_STATE
    th = t * h
    kpad = t + _ROWS8
    lane_lag = lax.broadcasted_iota(jnp.int32, (kpad, th), 1) // h
    krow = lax.broadcasted_iota(jnp.int32, (kpad, th), 0)
    pick = lambda lag_of_lane: jnp.where(krow == lag_of_lane, 1.0, 0.0).astype(F32)
    tile_h = jnp.where(lax.broadcasted_iota(jnp.int32, (h, th), 0)
                       == lax.broadcasted_iota(jnp.int32, (h, th), 1) % h, 1.0, 0.0).astype(F32)
    lane = lax.broadcasted_iota(jnp.int32, (h, th), 1)
    z, gs = [], []
    for d in range(2):
        lam_r, lam_i = lr_ref[d, 0], li_ref[d, 0]
        dt = jnp.exp(ldt_ref[d, 0])
        ldr, ldi = lam_r * dt, lam_i * dt
        mag = jnp.exp(ldr)
        lbr, lbi = mag * jnp.cos(ldi), mag * jnp.sin(ldi)
        den = lam_r * lam_r + lam_i * lam_i
        nr = lbr - 1.0
        cf_r = (nr * lam_r + lbi * lam_i) / den
        cf_i = (lbi * lam_r - nr * lam_i) / den
        br, bi = bre_ref[d, 0], bim_ref[d, 0]
        bbr = cf_r * br - cf_i * bi
        bbi = cf_r * bi + cf_i * br
        kcol = jnp.minimum(lax.broadcasted_iota(jnp.int32, (kpad, 1), 0), t).astype(F32)
        mk = jnp.exp(kcol * ldr)
        wr_all, wi_all = mk * jnp.cos(kcol * ldi), mk * jnp.sin(kcol * ldi)
        are_ref[d, 0] = wr_all[t:t + 1]
        aim_ref[d, 0] = wi_all[t:t + 1]
        gs.append([(bbr * wr_all[k:k + 1] - bbi * wi_all[k:k + 1], bbr * wi_all[k:k + 1] + bbi * wr_all[k:k + 1])
                   for k in range(t)])
        dtc = jnp.exp(ldtc_ref[d, 0])
        ldrc, ldic = lrc_ref[d, 0] * dtc, lic_ref[d, 0] * dtc
        krow_f = jnp.minimum(lax.broadcasted_iota(jnp.int32, (1, kpad), 1), t).astype(F32)
        mkc = jnp.exp(ldrc * krow_f)
        wrc, wic = mkc * jnp.cos(ldic * krow_f), mkc * jnp.sin(ldic * krow_f)
        ct_r = _dot(cret_ref[d, 0], tile_h, HIGHEST)
        ct_i = _dot(cimt_ref[d, 0], tile_h, HIGHEST)

        def c_lam(sel):
            er, ei = _dot(wrc, sel, HIGHEST), _dot(wic, sel, HIGHEST)
            return ct_r * er - ct_i * ei, ct_r * ei + ct_i * er

        lr_, li_ = c_lam(pick(lane_lag if d == 0 else t - 1 - lane_lag))
        z.append(_dot(bbr, lr_, HIGHEST) - _dot(bbi, li_, HIGHEST))
        er_, ei_ = c_lam(pick(lane_lag + 1 if d == 0 else t - lane_lag))
        r2_ref[0, (2 * d) * p:(2 * d + 1) * p, :] = er_.astype(BF16)
        r2_ref[0, (2 * d + 1) * p:(2 * d + 2) * p, :] = (-ei_).astype(BF16)
    zf, zb = z
    for s in range(t):
        fwd = zf if s == 0 else pltpu.roll(zf, s * h, 1)
        bwd = zb if s == t - 1 else pltpu.roll(zb, th - (t - 1 - s) * h, 1)
        taps = jnp.where(lane >= s * h, fwd, 0.0) + jnp.where(lane < (s + 1) * h, bwd, 0.0)
        inc = jnp.concatenate([gs[0][t - 1 - s][0], gs[0][t - 1 - s][1], gs[1][s][0], gs[1][s][1]], axis=1)
        r1_ref[0, s * h:(s + 1) * h, :] = jnp.concatenate([taps, inc], axis=1).astype(BF16)


def _ssm_ops(lam_re, lam_im, log_dt, b_re, b_im, c_re, c_im):
    g, p, h, t = SSM_GROUPS, SSM_STATE, SSM_GROUP, SSM_T
    row = lambda a: a.reshape(2, g, 1, p)
    col = lambda a: a.reshape(2, g, p, 1)
    ldt = jnp.broadcast_to(log_dt[:, :, None], (2, g, p))
    tr = lambda a: jnp.swapaxes(a, 2, 3)
    row_spec = pl.BlockSpec((2, 1, 1, p), lambda i: (0, i, 0, 0))
    col_spec = pl.BlockSpec((2, 1, p, 1), lambda i: (0, i, 0, 0))
    hp_spec = pl.BlockSpec((2, 1, h, p), lambda i: (0, i, 0, 0))
    ph_spec = pl.BlockSpec((2, 1, p, h), lambda i: (0, i, 0, 0))
    a_shape = jax.ShapeDtypeStruct((2, g, 1, p), F32)
    r1, r2, a_re, a_im = pl.pallas_call(
        _ssm_ops_kernel,
        grid=(g,),
        in_specs=[row_spec, row_spec, row_spec, col_spec, col_spec, col_spec, hp_spec, hp_spec, ph_spec, ph_spec],
        out_specs=[pl.BlockSpec((1, t * h, t * h + 4 * p), lambda i: (i, 0, 0)),
                   pl.BlockSpec((1, 4 * p, t * h), lambda i: (i, 0, 0)), row_spec, row_spec],
        out_shape=[jax.ShapeDtypeStruct((g, t * h, t * h + 4 * p), BF16),
                   jax.ShapeDtypeStruct((g, 4 * p, t * h), BF16), a_shape, a_shape],
        compiler_params=_cparams(("arbitrary",)),
    )(row(lam_re), row(lam_im), row(ldt), col(lam_re), col(lam_im), col(ldt), tr(b_re), tr(b_im), tr(c_re), tr(c_im))
    flat = lambda a: a.reshape(1, g * p)
    return r1, r2, flat(a_re[0]), flat(a_im[0]), flat(a_re[1]), flat(a_im[1])


_LANE_GROUPS = 128 // SSM_GROUP
_ROWS8 = 8


def _block_transpose(xs):
    h = SSM_GROUP
    blk = lax.broadcasted_iota(jnp.int32, xs[0].shape, 1) // h
    xs = list(xs)
    for d in (4, 2, 1):
        upper = (blk & d) != 0
        for i in range(_LANE_GROUPS):
            if i & d:
                continue
            a, b = xs[i], xs[i + d]
            xs[i] = jnp.where(upper, pltpu.roll(b, d * h, 1), a)
            xs[i + d] = jnp.where(upper, b, pltpu.roll(a, 128 - d * h, 1))
    return xs


def _sublane_transpose(xs):
    sub = lax.broadcasted_iota(jnp.int32, xs[0].shape, 0)
    xs = list(xs)
    for d in (4, 2, 1):
        upper = (sub & d) != 0
        for i in range(_ROWS8):
            if i & d:
                continue
            a, b = xs[i], xs[i + d]
            xs[i] = jnp.where(upper, pltpu.roll(b, d, 0), a)
            xs[i + d] = jnp.where(upper, b, pltpu.roll(a, _ROWS8 - d, 0))
    return xs


def _to_groups_kernel(su_ref, ug_ref, *, nc):
    t, h = SSM_T, SSM_GROUP

    def body(rb, carry):
        base = pl.multiple_of(rb * (_ROWS8 * t), _ROWS8 * t)
        rows = pl.ds(pl.multiple_of(rb * _ROWS8, _ROWS8), _ROWS8)
        for j in range(t * h // 128):
            toks = [su_ref[0, pl.ds(pl.multiple_of(base + c * t + _ROWS8 * j, _ROWS8), _ROWS8), :]
                    for c in range(_ROWS8)]
            for gm, tile in enumerate(_block_transpose(_sublane_transpose(toks))):
                ug_ref[gm, rows, j * 128:(j + 1) * 128] = tile
        return carry

    lax.fori_loop(0, nc // _ROWS8, body, 0)


def _to_groups(su):
    bsz, nt, w = su.shape
    nc = nt // SSM_T
    return pl.pallas_call(
        functools.partial(_to_groups_kernel, nc=nc),
        grid=(bsz, w // 128),
        in_specs=[pl.BlockSpec((1, nt, 128), lambda b, a: (b, 0, a))],
        out_specs=pl.BlockSpec((_LANE_GROUPS, nc, SSM_T * SSM_GROUP), lambda b, a: (a, b, 0)),
        out_shape=jax.ShapeDtypeStruct((SSM_GROUPS, bsz * nc, SSM_T * SSM_GROUP), F32),
        compiler_params=_cparams(("arbitrary", "arbitrary")),
    )(su)


def _from_groups_kernel(yg_ref, y_ref, *, nc):
    t, h = SSM_T, SSM_GROUP

    def body(rb, carry):
        base = pl.multiple_of(rb * (_ROWS8 * t), _ROWS8 * t)
        rows = pl.ds(pl.multiple_of(rb * _ROWS8, _ROWS8), _ROWS8)
        for j in range(t * h // 128):
            tiles = [yg_ref[gm, rows, j * 128:(j + 1) * 128] for gm in range(_LANE_GROUPS)]
            for c, tile in enumerate(_sublane_transpose(_block_transpose(tiles))):
                y_ref[0, pl.ds(pl.multiple_of(base + c * t + _ROWS8 * j, _ROWS8), _ROWS8), :] = tile
        return carry

    lax.fori_loop(0, nc // _ROWS8, body, 0)


def _from_groups(yg, bsz):
    g, n, th = yg.shape
    nc = n // bsz
    nt = nc * SSM_T
    return pl.pallas_call(
        functools.partial(_from_groups_kernel, nc=nc),
        grid=(bsz, g // _LANE_GROUPS),
        in_specs=[pl.BlockSpec((_LANE_GROUPS, nc, th), lambda b, a: (a, b, 0))],
        out_specs=pl.BlockSpec((1, nt, 128), lambda b, a: (b, 0, a)),
        out_shape=jax.ShapeDtypeStruct((bsz, nt, g * SSM_GROUP), F32),
        compiler_params=_cparams(("arbitrary", "arbitrary")),
    )(yg)


def _ssm_phase1_kernel(u_ref, r1_ref, y_ref, fr_ref, fi_ref, br_ref, bi_ref):
    th = SSM_T * SSM_GROUP
    p = SSM_STATE
    outs = [_dot(u_ref[k].astype(BF16), r1_ref[k]) for k in range(2)]
    for k in range(2):
        y_ref[k] = outs[k][:, 0:th]
    for idx, ref in enumerate((fr_ref, fi_ref, br_ref, bi_ref)):
        lo = th + idx * p
        ref[...] = jnp.concatenate([o[:, lo:lo + p] for o in outs], axis=1)


def _ssm_phase1(ug, r1):
    g, n, th = ug.shape
    p = SSM_STATE
    plane = pl.BlockSpec((n, 2 * p), lambda i: (0, i))
    plane_shape = jax.ShapeDtypeStruct((n, g * p), F32)
    return pl.pallas_call(
        _ssm_phase1_kernel,
        grid=(g // 2,),
        in_specs=[pl.BlockSpec((2, n, th), lambda i: (i, 0, 0)),
                  pl.BlockSpec((2, th, th + 4 * p), lambda i: (i, 0, 0))],
        out_specs=[pl.BlockSpec((2, n, th), lambda i: (i, 0, 0)), plane, plane, plane, plane],
        out_shape=[jax.ShapeDtypeStruct((g, n, th), F32), plane_shape, plane_shape, plane_shape, plane_shape],
        compiler_params=_cparams(("arbitrary",)),
    )(ug, r1)


def _ssm_scan_kernel(dfr_ref, dfi_ref, dbr_ref, dbi_ref, afr_ref, afi_ref, abr_ref, abi_ref,
                     sfr_ref, sfi_ref, sbr_ref, sbi_ref, *, nc, bsz):
    width = dfr_ref.shape[1]

    def run(dr_ref, di_ref, ar_ref, ai_ref, or_ref, oi_ref, chunk_of_step):
        ar = ar_ref[...]
        ai = ai_ref[...]

        def body(i, s):
            sr, si = s
            rows = pl.ds(chunk_of_step(i), bsz, stride=nc)
            or_ref[rows, :] = sr
            oi_ref[rows, :] = si
            return (ar * sr - ai * si + dr_ref[rows, :], ar * si + ai * sr + di_ref[rows, :])

        zero = jnp.zeros((bsz, width), F32)
        lax.fori_loop(0, nc, body, (zero, zero))

    run(dfr_ref, dfi_ref, afr_ref, afi_ref, sfr_ref, sfi_ref, lambda i: i)
    run(dbr_ref, dbi_ref, abr_ref, abi_ref, sbr_ref, sbi_ref,
        lambda i: jnp.where(i < SSM_NCTX, SSM_NCTX - 1 - i, nc + SSM_NCTX - 1 - i))


def _ssm_scan(planes, decays, bsz):
    n, width = planes[0].shape
    blk = pl.BlockSpec((n, 128), lambda i: (0, i))
    arow = pl.BlockSpec((1, 128), lambda i: (0, i))
    shape = jax.ShapeDtypeStruct((n, width), F32)
    return pl.pallas_call(
        functools.partial(_ssm_scan_kernel, nc=n // bsz, bsz=bsz),
        grid=(width // 128,),
        in_specs=[blk] * 4 + [arow] * 4,
        out_specs=[blk] * 4,
        out_shape=[shape] * 4,
        compiler_params=_cparams(("arbitrary",)),
    )(*planes, *decays)


def _ssm_phase2_kernel(y_ref, sfr_ref, sfi_ref, sbr_ref, sbi_ref, r2_ref, o_ref):
    p = SSM_STATE
    for k in range(2):
        acc = y_ref[k]
        for idx, ref in enumerate((sfr_ref, sfi_ref, sbr_ref, sbi_ref)):
            acc = acc + _dot(ref[:, k * p:(k + 1) * p].astype(BF16), r2_ref[k, idx * p:(idx + 1) * p, :])
        o_ref[k] = acc


def _ssm_phase2(yg, states, r2):
    g, n, th = yg.shape
    p = SSM_STATE
    plane = pl.BlockSpec((n, 2 * p), lambda i: (0, i))
    return pl.pallas_call(
        _ssm_phase2_kernel,
        grid=(g // 2,),
        in_specs=[pl.BlockSpec((2, n, th), lambda i: (i, 0, 0)), plane, plane, plane, plane,
                  pl.BlockSpec((2, 4 * p, th), lambda i: (i, 0, 0))],
        out_specs=pl.BlockSpec((2, n, th), lambda i: (i, 0, 0)),
        out_shape=jax.ShapeDtypeStruct((g, n, th), F32),
        compiler_params=_cparams(("arbitrary",)),
    )(yg, *states, r2)


def _ssm_mix(su_all, ops):
    r1, r2 = ops[0], ops[1]
    bsz = su_all.shape[0]
    ug = _to_groups(su_all)
    y1, *planes = _ssm_phase1(ug, r1)
    states = _ssm_scan(planes, ops[2:], bsz)
    return _from_groups(_ssm_phase2(y1, states, r2), bsz)


def _gla_direction(qk, v, z, wg, bg, s_ref, d, o_ref, reverse):
    tb = qk.shape[0]
    c_len = GLA_CHUNK
    x = _dot(z.astype(BF16), wg) + bg
    gate = (jnp.minimum(x, 0.0) - jnp.log(1.0 + jnp.exp(-jnp.abs(x)))) * (1.0 / GLA_TAU)
    ti = lax.broadcasted_iota(jnp.int32, (tb, tb), 0)
    si = lax.broadcasted_iota(jnp.int32, (tb, tb), 1)
    order = (si >= ti) if reverse else (si <= ti)
    tri = jnp.where(jnp.logical_and(order, ti // c_len == si // c_len), 1.0, 0.0).astype(F32)
    bcum_all = _dot(tri, gate, HIGHEST)
    q_all = (qk[:, 0:GLA_QK_W] * (GLA_DK ** -0.5) * jnp.exp(bcum_all)).astype(BF16)
    k_all = qk[:, GLA_QK_W:2 * GLA_QK_W]
    kin_all = k_all * jnp.exp(-bcum_all)
    v_bf = v.astype(BF16)
    nh = GLA_HEADS
    head_of = lambda shape, axis, size: lax.broadcasted_iota(jnp.int32, shape, axis) // size
    kmask = head_of((nh * c_len, GLA_QK_W), 0, c_len) == head_of((nh * c_len, GLA_QK_W), 1, GLA_DK)
    vmask = head_of((nh * c_len, GLA_V_W), 0, c_len) == head_of((nh * c_len, GLA_V_W), 1, GLA_DV)
    smask = head_of((GLA_V_W, GLA_QK_W), 0, GLA_DV) == head_of((GLA_V_W, GLA_QK_W), 1, GLA_DK)
    tq = lax.broadcasted_iota(jnp.int32, (c_len, nh * c_len), 0)
    sk = lax.broadcasted_iota(jnp.int32, (c_len, nh * c_len), 1) % c_len
    keep = (sk >= tq) if reverse else (sk <= tq)
    chunks = range(tb // c_len)
    for c in (reversed(chunks) if reverse else chunks):
        rs = slice(c * c_len, (c + 1) * c_len)
        bcum = bcum_all[rs]
        blast = bcum[0:1] if reverse else bcum[c_len - 1:c_len]
        k_up = (k_all[rs] * jnp.exp(blast - bcum)).astype(BF16)
        q_in = q_all[rs]
        k_exp = jnp.where(kmask, jnp.concatenate([kin_all[rs]] * nh, axis=0), 0.0).astype(BF16)
        v_exp = jnp.where(vmask, jnp.concatenate([v[rs]] * nh, axis=0), 0.0).astype(BF16)
        att = jnp.where(keep, _dot_nt(q_in, k_exp), 0.0)
        st = s_ref[d]
        o_ref[0, rs, :] = _dot(att.astype(BF16), v_exp) + _dot_nt(q_in, st.astype(BF16))
        s_ref[d] = st * jnp.exp(blast) + jnp.where(smask, _dot_tn(v_bf[rs], k_up), 0.0)


def _gla_kernel(qkf_ref, vf_ref, zf_ref, qkb_ref, vb_ref, zb_ref, wg_ref, bg_ref, of_ref, ob_ref, s_ref):
    @pl.when(pl.program_id(1) == 0)
    def _():
        s_ref[...] = jnp.zeros_like(s_ref)

    _gla_direction(qkf_ref[0], vf_ref[0], zf_ref[0], wg_ref[0], bg_ref[0], s_ref, 0, of_ref, False)
    _gla_direction(qkb_ref[0], vb_ref[0], zb_ref[0], wg_ref[1], bg_ref[1], s_ref, 1, ob_ref, True)


def _gla(gqk, gv, gz, wg_pad, bg):
    bsz, nt, _ = gqk.shape
    tb = TOKEN_TILE
    tiles = nt // tb
    fwd = lambda b, i: (b, i, 0)
    bwd = lambda b, i: (b, jnp.where(i == 0, 0, tiles - i), 0)
    spec = lambda width, im: pl.BlockSpec((1, tb, width), im)
    return pl.pallas_call(
        _gla_kernel,
        grid=(bsz, tiles),
        in_specs=[spec(2 * GLA_QK_W, fwd), spec(GLA_V_W, fwd), spec(GZ_PAD, fwd),
                  spec(2 * GLA_QK_W, bwd), spec(GLA_V_W, bwd), spec(GZ_PAD, bwd),
                  pl.BlockSpec((2, GZ_PAD, GLA_QK_W), lambda b, i: (0, 0, 0)),
                  pl.BlockSpec((2, 1, GLA_QK_W), lambda b, i: (0, 0, 0))],
        out_specs=[spec(GLA_V_W, fwd), spec(GLA_V_W, bwd)],
        out_shape=[jax.ShapeDtypeStruct((bsz, nt, GLA_V_W), F32), jax.ShapeDtypeStruct((bsz, nt, GLA_V_W), F32)],
        scratch_shapes=[pltpu.VMEM((2, GLA_V_W, GLA_QK_W), F32)],
        compiler_params=_cparams(("arbitrary", "arbitrary")),
    )(gqk, gv, gz, gqk, gv, gz, wg_pad, bg)


def _outproj_kernel(ctx_ref, x_ref, m_ref, att_ref, y_ref, u_ref, of_ref, ob_ref, r_ref,
                    ssmd_ref, wglu_ref, bglu_ref, gng_ref, wo_ref, ln1g_ref, ln1b_ref, router_ref,
                    x1_ref, h2_ref, aff_ref, *, t0):
    nb, tm, d = x_ref.shape
    x_in = jnp.where(pl.program_id(1) + t0 == 0, ctx_ref[...], x_ref[...])
    rows = nb * tm
    flat = lambda ref: ref[...].reshape(rows, ref.shape[-1])
    m = m_ref[0]
    zin = flat(y_ref) + ssmd_ref[...] * flat(u_ref)
    z = 0.5 * zin * (1.0 + jnp.tanh(math.sqrt(2.0 / math.pi) * (zin + 0.044715 * (zin * zin * zin))))
    ssm = z * _sigmoid(_dot(z.astype(BF16), wglu_ref[...]) + bglu_ref[...])
    o = flat(of_ref) + flat(ob_ref)
    r = flat(r_ref)
    gng = gng_ref[...]
    mix = [flat(att_ref), ssm.astype(BF16)]
    for hd in range(GLA_HEADS):
        vs = slice(hd * GLA_DV, (hd + 1) * GLA_DV)
        oh = o[:, vs]
        rh = r[:, vs]
        oh = oh * lax.rsqrt(jnp.mean(oh * oh, axis=-1, keepdims=True) + LN_EPS) * gng
        mix.append((oh * (rh * _sigmoid(rh))).astype(BF16))
    proj = _dot(jnp.concatenate(mix, axis=1), wo_ref[...]).reshape(nb, tm, d)
    x1 = _ln(ALPHA * x_in + m[:, 2:3] * proj) * ln1g_ref[...] + ln1b_ref[...]
    x1_ref[...] = x1
    h2 = (_ln(x1) * (1.0 + m[:, 4:5]) + m[:, 3:4]).reshape(rows, d)
    h_hi = h2.astype(BF16)
    h2_ref[...] = h_hi.reshape(nb, tm, d)
    l_hi = _dot(h_hi, router_ref[...])
    logits = l_hi + pltpu.roll(l_hi, 128 - N_EXPERTS, 1)
    col = lax.broadcasted_iota(jnp.int32, logits.shape, 1)
    logits = jnp.where(col < N_EXPERTS, logits, NEG_INF)
    e = jnp.exp(logits - jnp.max(logits, axis=-1, keepdims=True))
    aff_ref[...] = (e / jnp.sum(e, axis=-1, keepdims=True)).reshape(nb, tm, 128)


def _outproj(ctx_src, x_src, x_off, mods, att, y, su, o_f, o_b, gr, ssm_d, w_glu, b_glu, gn_g, w_out, ln1_g, ln1_b,
             router_split, with_ctx):
    bsz, nt, _ = y.shape
    d = x_src.shape[-1]
    tm = TOKEN_TILE
    nb = BATCH_TILE
    t0 = 0 if with_ctx else CTX_LEN // tm
    tiles = nt // tm - t0
    att_t0 = 0 if with_ctx else -t0
    tok_in = lambda width: pl.BlockSpec((nb, tm, width), lambda b, i: (b, i + t0, 0))
    tok_out = lambda width: pl.BlockSpec((nb, tm, width), lambda b, i: (b, i, 0))
    const = lambda shape: pl.BlockSpec(shape, lambda b, i: tuple(0 for _ in shape))
    once = lambda shape: pl.BlockSpec(shape, lambda b, i: tuple(0 for _ in shape), pipeline_mode=pl.Buffered(1))
    shp = lambda width, dt: jax.ShapeDtypeStruct((bsz, tiles * tm, width), dt)
    return pl.pallas_call(
        functools.partial(_outproj_kernel, t0=t0),
        grid=(bsz // nb, tiles),
        in_specs=[
            pl.BlockSpec((nb, tm, d), lambda b, i: (b, 0, 0)),
            pl.BlockSpec((nb, tm, d), lambda b, i: (b, jnp.maximum(i + t0 - x_off, 0), 0)),
            pl.BlockSpec((1, nb, N_MOD, d), lambda b, i: (jnp.minimum(i + t0, 1), b, 0, 0)),
            pl.BlockSpec((nb, tm, ATT_Q_W), lambda b, i: (b, i + t0 + att_t0, 0)),
            tok_in(SSM_WIDTH), tok_in(SSM_WIDTH), tok_in(GLA_V_W), tok_in(GLA_V_W), tok_in(GLA_V_W),
            const((1, SSM_WIDTH)), const((SSM_WIDTH, SSM_WIDTH)), const((1, SSM_WIDTH)), const((1, GLA_DV)),
            once((d, d)), const((1, d)), const((1, d)), const((d, 128)),
        ],
        out_specs=[tok_out(d), tok_out(d), tok_out(128)],
        out_shape=[shp(d, F32), shp(d, BF16), shp(128, F32)],
        compiler_params=_cparams(("arbitrary", "arbitrary")),
    )(ctx_src, x_src, mods, att, y, su, o_f, o_b, gr, ssm_d, w_glu, b_glu, gn_g, w_out, ln1_g, ln1_b, router_split)


def _expert_kernel(x_ref, g_ref, wg_ref, wu_ref, wd_ref, o_ref, hid_ref, *, nf, tf):
    s = pl.program_id(1)

    @pl.when(s < nf)
    def _():
        x = x_ref[0]
        a = _dot(x, wg_ref[0, 0].astype(BF16))
        u = _dot(x, wu_ref[0, 0].astype(BF16))
        col = pl.multiple_of(s * tf, tf)
        hid_ref[:, pl.ds(col, tf)] = ((a * _sigmoid(a)) * u).astype(BF16)

    @pl.when(s >= nf)
    def _():
        o_ref[0] = (_dot(hid_ref[...], wd_ref[0, 0].astype(BF16)) * g_ref[0]).astype(BF16)


def _experts(xe, ge, w_gate, w_up, w_down, layer, e0):
    e, r, d = xe.shape
    ff = w_gate.shape[3]
    tf = 256
    nf = ff // tf
    nd = d // tf
    up = lambda i, s: (layer, e0 + i, 0, jnp.minimum(s, nf - 1))
    down = lambda i, s: (layer, e0 + i, 0, jnp.maximum(s - nf, 0))
    return pl.pallas_call(
        functools.partial(_expert_kernel, nf=nf, tf=tf),
        grid=(e, nf + nd),
        in_specs=[pl.BlockSpec((1, r, d), lambda i, s: (i, 0, 0)),
                  pl.BlockSpec((1, r, 1), lambda i, s: (i, 0, 0)),
                  pl.BlockSpec((1, 1, d, tf), up),
                  pl.BlockSpec((1, 1, d, tf), up),
                  pl.BlockSpec((1, 1, ff, tf), down)],
        out_specs=pl.BlockSpec((1, r, tf), lambda i, s: (i, 0, jnp.maximum(s - nf, 0))),
        out_shape=jax.ShapeDtypeStruct((e, r, d), BF16),
        scratch_shapes=[pltpu.VMEM((r, ff), BF16)],
        compiler_params=_cparams(("arbitrary", "arbitrary")),
    )(xe, ge, w_gate, w_up, w_down)


COMBINE_WIN = 64
COMBINE_ALIGN = 16


def _combine_kernel(lo_ref, hi_ref, x1_ref, m_ref, g_ref, b_ref, tok_hbm, *rest, tiles, nsteps, group_rows):
    ye_groups, (o_ref, buf, tokbuf, sem) = rest[:-4], rest[-4:]
    _combine_body(lo_ref, hi_ref, x1_ref, m_ref, g_ref, b_ref, tok_hbm, ye_groups, o_ref, buf, tokbuf, sem,
                  tiles, nsteps, group_rows)


def _combine_body(lo_ref, hi_ref, x1_ref, m_ref, g_ref, b_ref, tok_hbm, ye_groups, o_ref, buf, tokbuf, sem,
                  tiles, nsteps, group_rows):
    n_exp = N_EXPERTS
    win = COMBINE_WIN
    tm = x1_ref.shape[1]
    step = pl.program_id(0) * tiles + pl.program_id(1)
    slot = step % 2
    extra = 2

    per_group = n_exp // len(ye_groups)

    def first_row(st, e, k):
        u = (lo_ref[st * n_exp + e] // COMBINE_ALIGN) * COMBINE_ALIGN + k * win
        group_end = (e // per_group + 1) * group_rows
        return u, pl.multiple_of(jnp.minimum(u, group_end - win), COMBINE_ALIGN)

    def copies(st, k, dst):
        out = []
        for e in range(n_exp):
            s0 = first_row(st, e, k)[1]
            local = pl.ds(pl.multiple_of(s0 - (e // per_group) * group_rows, COMBINE_ALIGN), win)
            out.append(pltpu.make_async_copy(ye_groups[e // per_group].at[local, :], buf.at[dst, e], sem.at[dst]))
            out.append(pltpu.make_async_copy(tok_hbm.at[pl.ds(s0, win), :], tokbuf.at[dst, e], sem.at[dst]))
        return out

    @pl.when(step == 0)
    def _():
        for cp in copies(step, 0, slot):
            cp.start()

    @pl.when(step + 1 < nsteps)
    def _():
        for cp in copies(step + 1, 0, 1 - slot):
            cp.start()

    tile_tok = pl.program_id(1) * tm + lax.broadcasted_iota(jnp.int32, (win, tm), 1)
    jrow = lax.broadcasted_iota(jnp.int32, (win, 1), 0)

    def spread(k, src):
        parts = []
        for e in range(n_exp):
            u, s0 = first_row(step, e, k)
            row = s0 + jrow
            mine = jnp.logical_and(row >= jnp.maximum(u, lo_ref[step * n_exp + e]), row < hi_ref[step * n_exp + e])
            hit = jnp.logical_and(tokbuf[src, e][:, 0:1] == tile_tok, mine)
            parts.append(jnp.where(hit, 1.0, 0.0).astype(BF16))
        onehot_t = jnp.concatenate(parts, axis=0)
        return _dot_tn(onehot_t, buf[src].reshape(n_exp * win, buf.shape[-1]))

    for cp in copies(step, 0, slot):
        cp.wait()
    acc = spread(0, slot)

    rounds = jnp.int32(1)
    for e in range(n_exp):
        span = hi_ref[step * n_exp + e] - (lo_ref[step * n_exp + e] // COMBINE_ALIGN) * COMBINE_ALIGN
        rounds = jnp.maximum(rounds, (span + win - 1) // win)

    def more(k, acc):
        for cp in copies(step, k, extra):
            cp.start()
        for cp in copies(step, k, extra):
            cp.wait()
        return acc + spread(k, extra)

    acc = lax.fori_loop(1, rounds, more, acc)
    m = m_ref[0, 0]
    o_ref[0] = _ln(ALPHA * x1_ref[0] + m[5:6] * acc) * g_ref[...] + b_ref[...]


def _combine(lo, hi, x1, mods, ln2_g, ln2_b, tok_of_row, ye_groups, with_ctx):
    bsz, nt, d = x1.shape
    tm = TOKEN_TILE
    tiles = nt // tm
    kind0 = 0 if with_ctx else 1
    tok = lambda width: pl.BlockSpec((1, tm, width), lambda b, i, lo, hi: (b, i, 0))
    vec = pl.BlockSpec((1, d), lambda b, i, lo, hi: (0, 0))
    hbm = pl.BlockSpec(memory_space=pl.ANY)
    return pl.pallas_call(
        functools.partial(_combine_kernel, tiles=tiles, nsteps=bsz * tiles, group_rows=ye_groups[0].shape[0]),
        grid_spec=pltpu.PrefetchScalarGridSpec(
            num_scalar_prefetch=2,
            grid=(bsz, tiles),
            in_specs=[tok(d),
                      pl.BlockSpec((1, 1, N_MOD, d), lambda b, i, lo, hi: (jnp.minimum(i + kind0, 1), b, 0, 0)),
                      vec, vec, hbm] + [hbm] * len(ye_groups),
            out_specs=tok(d),
            scratch_shapes=[pltpu.VMEM((3, N_EXPERTS, COMBINE_WIN, d), BF16),
                            pltpu.VMEM((3, N_EXPERTS, COMBINE_WIN, 128), jnp.int32),
                            pltpu.SemaphoreType.DMA((3,))],
        ),
        out_shape=jax.ShapeDtypeStruct((bsz, nt, d), F32),
        compiler_params=_cparams(("arbitrary", "arbitrary")),
    )(lo, hi, x1, mods, ln2_g, ln2_b, tok_of_row, *ye_groups)


def _route(aff, cap):
    gate, idx = lax.top_k(jnp.swapaxes(aff, 1, 2), cap)
    return gate, idx


def _rope_tables(seq):
    half = HEAD_DIM // 2
    nf = half // 2
    inv = ROPE_BASE ** (-jnp.arange(nf, dtype=F32) / nf)
    pos = jnp.arange(seq)
    ang_r = (pos // GRID_W).astype(F32)[:, None] * inv
    ang_c = (pos % GRID_W).astype(F32)[:, None] * inv
    zeros = jnp.zeros_like(ang_r)
    cos = jnp.concatenate([jnp.cos(ang_r), jnp.cos(ang_r), jnp.cos(ang_c), jnp.cos(ang_c)], axis=-1)
    sa = jnp.concatenate([-jnp.sin(ang_r), zeros, -jnp.sin(ang_c), zeros], axis=-1)
    sb = jnp.concatenate([zeros, jnp.sin(ang_r), zeros, jnp.sin(ang_c)], axis=-1)
    ctx = lambda fill: jnp.full((CTX_LEN, HEAD_DIM), fill, F32)
    return (jnp.concatenate([ctx(1.0), cos], axis=0), jnp.concatenate([ctx(0.0), sa], axis=0),
            jnp.concatenate([ctx(0.0), sb], axis=0))


def kernel(x, c, ctx, c_ctx, w_ada, b_ada, w_in, attn_sink, ssm_lam_re, ssm_lam_im, ssm_log_dt,
           ssm_b_re, ssm_b_im, ssm_c_re, ssm_c_im, ssm_d, ssm_w_glu, ssm_b_glu, gla_w_gate,
           gla_b_gate, gla_norm_g, w_out, ln1_g, ln1_b, ln2_g, ln2_b, router, exp_w_gate,
           exp_w_up, exp_w_down):
    bsz, seq, d = x.shape
    n_exp = router.shape[-1]
    cc = jnp.zeros((8, d), F32).at[:bsz].set(c).at[bsz].set(c_ctx)
    mod_all = _ada(cc, w_ada, b_ada)
    cos_t, sa_t, sb_t = _rope_tables(seq)
    ctx_src, x_src, x_off = ctx, x, CTX_LEN // TOKEN_TILE
    nt = CTX_LEN + seq
    bidx = jnp.arange(bsz)[None, :, None]

    for l in range(DEPTH):
        with_ctx = l < DEPTH - 1
        m = mod_all[l].reshape(8, N_MOD, d)
        mods = jnp.stack([jnp.broadcast_to(m[bsz], (bsz, N_MOD, d)), m[:bsz]], axis=0)
        w_in_p = jnp.pad(w_in[l], ((0, 0), (0, N_IN_PAD - N_IN))).astype(BF16)
        q, k, v, su, gqk, gv, gr, gz = _inproj(ctx_src, x_src, x_off, nt, mods, cos_t, sa_t, sb_t, w_in_p)

        att = _attention(attn_sink[l], q, k, v, with_ctx)

        ops = _ssm_ops(ssm_lam_re[l], ssm_lam_im[l], ssm_log_dt[l], ssm_b_re[l], ssm_b_im[l], ssm_c_re[l], ssm_c_im[l])
        y = _ssm_mix(su, ops)

        wg_pad = jnp.zeros((2, GZ_PAD, GLA_QK_W), F32)
        wg_pad = wg_pad.at[0, 0:GLA_RANK].set(gla_w_gate[l, 0]).at[1, GLA_RANK:2 * GLA_RANK].set(gla_w_gate[l, 1])
        o_f, o_b = _gla(gqk, gv, gz, wg_pad.astype(BF16), gla_b_gate[l][:, None, :])

        r_hi = router[l].astype(BF16)
        r_lo = (router[l] - r_hi.astype(F32)).astype(BF16)
        router_split = jnp.pad(jnp.concatenate([r_hi, r_lo], axis=1), ((0, 0), (0, 128 - 2 * n_exp)))
        x1, h2, aff = _outproj(ctx_src, x_src, x_off, mods, att, y, su, o_f, o_b, gr,
                               ssm_d[l][None], ssm_w_glu[l].astype(BF16), ssm_b_glu[l][None], gla_norm_g[l][None],
                               w_out[l].astype(BF16), ln1_g[l][None], ln1_b[l][None], router_split, with_ctx)

        t0 = CTX_LEN if with_ctx else 0
        cap = CAPACITY_FACTOR * seq // n_exp
        gate, idx = _route(aff[:, t0:, :n_exp], cap)
        idx = idx + t0
        if with_ctx:
            cap_c = CAPACITY_FACTOR * CTX_LEN // n_exp
            gate_c, idx_c = _route(aff[:, :CTX_LEN, :n_exp], cap_c)
            gate = jnp.concatenate([gate, gate_c], axis=-1)
            idx = jnp.concatenate([idx, idx_c], axis=-1)
        idx, gate = lax.sort((idx, gate), dimension=-1, num_keys=1)
        idx_e = jnp.swapaxes(idx, 0, 1)
        gate_e = jnp.swapaxes(gate, 0, 1)
        slots = idx_e.shape[-1]
        rows_e = bsz * slots
        ye_groups = []
        eg = n_exp // EXPERT_GROUPS
        for grp in range(EXPERT_GROUPS):
            sl = slice(grp * eg, (grp + 1) * eg)
            xe = h2[bidx, idx_e[sl]].reshape(eg, rows_e, d)
            ye = _experts(xe, gate_e[sl].reshape(eg, rows_e, 1), exp_w_gate, exp_w_up, exp_w_down, l, grp * eg)
            ye_groups.append(ye.reshape(eg * rows_e, d))
        nt_out = x1.shape[1]
        bounds = jnp.arange(nt_out // TOKEN_TILE + 1) * TOKEN_TILE
        cnt = jnp.sum(idx_e[..., None] < bounds, axis=2).astype(jnp.int32)
        base = (jnp.arange(n_exp)[:, None, None] * rows_e + jnp.arange(bsz)[None, :, None] * slots).astype(jnp.int32)
        lo = jnp.transpose(base + cnt[:, :, :-1], (1, 2, 0)).reshape(-1)
        hi = jnp.transpose(base + cnt[:, :, 1:], (1, 2, 0)).reshape(-1)
        tok_of_row = jnp.broadcast_to(idx_e.reshape(n_exp * rows_e, 1).astype(jnp.int32), (n_exp * rows_e, 128))
        x_new = _combine(lo, hi, x1, mods, ln2_g[l][None], ln2_b[l][None], tok_of_row, ye_groups, with_ctx)
        ctx_src, x_src, x_off = x_new, x_new, 0
    return x_new
```

```python
import functools
import math

import jax
import jax.numpy as jnp
from jax import lax
from jax.experimental import pallas as pl
from jax.experimental.pallas import tpu as pltpu

F32 = jnp.float32
BF16 = jnp.bfloat16
HIGHEST = lax.Precision.HIGHEST

D_MODEL = 2048
DEPTH = 2
GRID_W = 64
CTX_LEN = 256
ATT_HEADS = 8
ATT_KV_HEADS = 2
HEAD_DIM = 128
ATT_BLOCK = 128
ATT_QBLOCKS = 2
ROPE_BASE = 10000.0
SSM_WIDTH = 512
SSM_GROUP = 16
SSM_GROUPS = 32
SSM_STATE = 64
GLA_HEADS = 4
GLA_DK = 64
GLA_DV = 128
GLA_RANK = 16
GLA_TAU = 16.0
GLA_CHUNK = 64
N_EXPERTS = 16
EXPERT_FF = 2048
CAPACITY_FACTOR = 2
N_MOD = 6
ALPHA = (2 * DEPTH) ** 0.25
LN_EPS = 1e-6
NEG_INF = -1e30

ATT_Q_W = ATT_HEADS * HEAD_DIM
ATT_KV_W = ATT_KV_HEADS * HEAD_DIM
GLA_QK_W = GLA_HEADS * GLA_DK
GLA_V_W = GLA_HEADS * GLA_DV
N_IN = 3616
N_IN_PAD = 3712
GZ_PAD = N_IN_PAD - 3584

TOKEN_TILE = 256
BATCH_TILE = 2
EXPERT_GROUPS = 4
SSM_T = 32
SSM_NCTX = CTX_LEN // SSM_T
VMEM_LIMIT = 56 * 1024 * 1024


def _cparams(sem, vmem=VMEM_LIMIT):
    return pltpu.CompilerParams(dimension_semantics=sem, vmem_limit_bytes=vmem)


def _dot(a, b, precision=None):
    return jnp.dot(a, b, preferred_element_type=F32, precision=precision)


def _dot_nt(a, b, precision=None):
    return lax.dot_general(a, b, (((1,), (1,)), ((), ())), preferred_element_type=F32, precision=precision)


def _dot_tn(a, b, precision=None):
    return lax.dot_general(a, b, (((0,), (0,)), ((), ())), preferred_element_type=F32, precision=precision)


def _sigmoid(x):
    return 1.0 / (1.0 + jnp.exp(-x))


def _ln(x):
    mu = jnp.mean(x, axis=-1, keepdims=True)
    xc = x - mu
    var = jnp.mean(xc * xc, axis=-1, keepdims=True)
    return xc * lax.rsqrt(var + LN_EPS)


def _ada_kernel(c_ref, w_ref, b_ref, o_ref):
    c = c_ref[...]
    s = c * _sigmoid(c)
    w = w_ref[0]
    w_hi = w.astype(BF16)
    w_lo = (w - w_hi.astype(F32)).astype(BF16)
    s_hi = s.astype(BF16)
    s_lo = (s - s_hi.astype(F32)).astype(BF16)
    both = _dot(jnp.concatenate([s_hi, s_lo], axis=0), w_hi)
    rows = s.shape[0]
    o_ref[0] = both[0:rows] + both[rows:2 * rows] + _dot(s_hi, w_lo) + b_ref[0]


def _ada(cc, w_ada, b_ada):
    depth, d, n = w_ada.shape
    tn = 1024
    return pl.pallas_call(
        _ada_kernel,
        grid=(depth, n // tn),
        in_specs=[
            pl.BlockSpec((8, d), lambda l, j: (0, 0)),
            pl.BlockSpec((1, d, tn), lambda l, j: (l, 0, j)),
            pl.BlockSpec((1, 1, tn), lambda l, j: (l, 0, j)),
        ],
        out_specs=pl.BlockSpec((1, 8, tn), lambda l, j: (l, 0, j)),
        out_shape=jax.ShapeDtypeStruct((depth, 8, n), F32),
        compiler_params=_cparams(("arbitrary", "arbitrary")),
    )(cc, w_ada, b_ada.reshape(depth, 1, n))


def _inproj_kernel(ctx_ref, x_ref, m_ref, cos_ref, sa_ref, sb_ref, w_ref,
                   q_ref, k_ref, v_ref, su_ref, gqk_ref, gv_ref, gr_ref, gz_ref):
    nb, tm, d = x_ref.shape
    rows = nb * tm
    m = m_ref[0]
    x = jnp.where(pl.program_id(1) == 0, ctx_ref[...], x_ref[...])
    h = _ln(x) * (1.0 + m[:, 1:2]) + m[:, 0:1]
    h = h.reshape(rows, d).astype(BF16)
    tile = lambda ref: jnp.concatenate([ref[...]] * nb, axis=0)
    cos = tile(cos_ref)
    sa = tile(sa_ref)
    sb = tile(sb_ref)

    def rope(t):
        return t * cos + pltpu.roll(t, 96, 1) * sa + pltpu.roll(t, 32, 1) * sb

    def put(ref, val, sl=slice(None)):
        ref[:, :, sl] = val.reshape(nb, tm, val.shape[-1])

    scale = HEAD_DIM ** -0.5
    q = _dot(h, w_ref[:, 0:ATT_Q_W])
    for hd in range(ATT_HEADS):
        sl = slice(hd * HEAD_DIM, (hd + 1) * HEAD_DIM)
        put(q_ref, (rope(q[:, sl]) * scale).astype(BF16), sl)
    kk = _dot(h, w_ref[:, 1024:1280])
    for hd in range(ATT_KV_HEADS):
        sl = slice(hd * HEAD_DIM, (hd + 1) * HEAD_DIM)
        put(k_ref, rope(kk[:, sl]).astype(BF16), sl)
    put(v_ref, _dot(h, w_ref[:, 1280:1536]).astype(BF16))
    put(su_ref, _dot(h, w_ref[:, 1536:2048]))
    put(gqk_ref, _dot(h, w_ref[:, 2048:2560]))
    put(gv_ref, _dot(h, w_ref[:, 2560:3072]))
    put(gr_ref, _dot(h, w_ref[:, 3072:3584]))
    put(gz_ref, _dot(h, w_ref[:, 3584:N_IN_PAD]))


def _inproj(ctx_src, x_src, x_off, nt, mods, cos_t, sa_t, sb_t, w_in_bf16):
    bsz, _, d = x_src.shape
    tm = TOKEN_TILE
    nb = BATCH_TILE
    tiles = nt // tm
    tok = lambda width: pl.BlockSpec((nb, tm, width), lambda b, i: (b, i, 0))
    tab = pl.BlockSpec((tm, HEAD_DIM), lambda b, i: (i, 0))
    shp = lambda width, dt: jax.ShapeDtypeStruct((bsz, nt, width), dt)
    return pl.pallas_call(
        _inproj_kernel,
        grid=(bsz // nb, tiles),
        in_specs=[
            pl.BlockSpec((nb, tm, d), lambda b, i: (b, 0, 0)),
            pl.BlockSpec((nb, tm, d), lambda b, i: (b, jnp.maximum(i - x_off, 0), 0)),
            pl.BlockSpec((1, nb, N_MOD, d), lambda b, i: (jnp.minimum(i, 1), b, 0, 0)),
            tab, tab, tab,
            pl.BlockSpec((d, N_IN_PAD), lambda b, i: (0, 0), pipeline_mode=pl.Buffered(1)),
        ],
        out_specs=[tok(ATT_Q_W), tok(ATT_KV_W), tok(ATT_KV_W), tok(SSM_WIDTH),
                   tok(2 * GLA_QK_W), tok(GLA_V_W), tok(GLA_V_W), tok(GZ_PAD)],
        out_shape=[shp(ATT_Q_W, BF16), shp(ATT_KV_W, BF16), shp(ATT_KV_W, BF16), shp(SSM_WIDTH, F32),
                   shp(2 * GLA_QK_W, F32), shp(GLA_V_W, F32), shp(GLA_V_W, F32), shp(GZ_PAD, F32)],
        compiler_params=_cparams(("arbitrary", "arbitrary")),
    )(ctx_src, x_src, mods, cos_t, sa_t, sb_t, w_in_bf16)


def _attn_kernel(sink_ref, q_ref, k_ref, v_ref, o_ref, *, blk0, nblk):
    for sub in range(ATT_QBLOCKS):
        _attn_block(sink_ref, q_ref, k_ref, v_ref, o_ref, sub, pl.program_id(1) * ATT_QBLOCKS + sub + blk0, nblk)


def _attn_block(sink_ref, q_ref, k_ref, v_ref, o_ref, sub, qb, nblk):
    w = ATT_BLOCK
    qrows = slice(sub * w, (sub + 1) * w)
    cb = CTX_LEN // w
    n = qb - cb
    span = 3 * w
    start = jnp.clip((n - 1) * w, 0, (nblk - cb) * w - span)
    band = pl.ds(pl.multiple_of(CTX_LEN + start, w), span)
    g = ATT_HEADS // ATT_KV_HEADS
    rows = g * w
    q_pos = n * w + lax.broadcasted_iota(jnp.int32, (rows, span), 0) % w
    rel = q_pos - (start + lax.broadcasted_iota(jnp.int32, (rows, span), 1))
    in_window = jnp.logical_and(jnp.logical_and(rel <= w, rel >= -w), n >= 0)
    rowg = lax.broadcasted_iota(jnp.int32, (rows, 1), 0) // w

    for kvh in range(ATT_KV_HEADS):
        sl = slice(kvh * HEAD_DIM, (kvh + 1) * HEAD_DIM)
        qs = jnp.concatenate(
            [q_ref[0, qrows, (kvh * g + a) * HEAD_DIM:(kvh * g + a + 1) * HEAD_DIM] for a in range(g)], axis=0)
        sink = jnp.zeros((rows, 1), F32)
        for a in range(g):
            sink = jnp.where(rowg == a, sink_ref[kvh * g + a], sink)
        s_b = jnp.where(in_window, _dot_nt(qs, k_ref[0, band, sl]), NEG_INF)
        s_x = _dot_nt(qs, k_ref[0, 0:CTX_LEN, sl])
        mx = jnp.maximum(jnp.maximum(jnp.max(s_b, axis=-1, keepdims=True), jnp.max(s_x, axis=-1, keepdims=True)), sink)
        p_b = jnp.exp(s_b - mx)
        p_x = jnp.exp(s_x - mx)
        den = jnp.sum(p_b, axis=-1, keepdims=True) + jnp.sum(p_x, axis=-1, keepdims=True) + jnp.exp(sink - mx)
        o = _dot(p_b.astype(BF16), v_ref[0, band, sl]) + _dot(p_x.astype(BF16), v_ref[0, 0:CTX_LEN, sl])
        o = o / den
        for a in range(g):
            hs = slice((kvh * g + a) * HEAD_DIM, (kvh * g + a + 1) * HEAD_DIM)
            o_ref[0, qrows, hs] = o[a * w:(a + 1) * w].astype(BF16)


def _attention(sink, q_all, k_all, v_all, with_ctx):
    bsz, nt, _ = q_all.shape
    w = ATT_BLOCK
    nblk = nt // w
    blk0 = 0 if with_ctx else CTX_LEN // w
    wq = w * ATT_QBLOCKS
    step0 = blk0 // ATT_QBLOCKS
    return pl.pallas_call(
        functools.partial(_attn_kernel, blk0=blk0, nblk=nblk),
        grid=(bsz, (nblk - blk0) // ATT_QBLOCKS),
        in_specs=[
            pl.BlockSpec(memory_space=pltpu.SMEM),
            pl.BlockSpec((1, wq, ATT_Q_W), lambda b, j: (b, j + step0, 0)),
            pl.BlockSpec((1, nt, ATT_KV_W), lambda b, j: (b, 0, 0)),
            pl.BlockSpec((1, nt, ATT_KV_W), lambda b, j: (b, 0, 0)),
        ],
        out_specs=pl.BlockSpec((1, wq, ATT_Q_W), lambda b, j: (b, j, 0)),
        out_shape=jax.ShapeDtypeStruct((bsz, (nblk - blk0) * w, ATT_Q_W), BF16),
        compiler_params=_cparams(("arbitrary", "arbitrary")),
    )(sink, q_all, k_all, v_all)


def _ssm_ops_kernel(lr_ref, li_ref, ldt_ref, lrc_ref, lic_ref, ldtc_ref, bre_ref, bim_ref, cret_ref, cimt_ref,
                    r1_ref, r2_ref, are_ref, aim_ref):
    t, h, p = SSM_T, SSM_GROUP, SSM_STATE
    th = t * h
    kpad = t + _ROWS8
    lane_lag = lax.broadcasted_iota(jnp.int32, (kpad, th), 1) // h
    krow = lax.broadcasted_iota(jnp.int32, (kpad, th), 0)
    pick = lambda lag_of_lane: jnp.where(krow == lag_of_lane, 1.0, 0.0).astype(F32)
    tile_h = jnp.where(lax.broadcasted_iota(jnp.int32, (h, th), 0)
                       == lax.broadcasted_iota(jnp.int32, (h, th), 1) % h, 1.0, 0.0).astype(F32)
    lane = lax.broadcasted_iota(jnp.int32, (h, th), 1)
    z, gs = [], []
    for d in range(2):
        lam_r, lam_i = lr_ref[d, 0], li_ref[d, 0]
        dt = jnp.exp(ldt_ref[d, 0])
        ldr, ldi = lam_r * dt, lam_i * dt
        mag = jnp.exp(ldr)
        lbr, lbi = mag * jnp.cos(ldi), mag * jnp.sin(ldi)
        den = lam_r * lam_r + lam_i * lam_i
        nr = lbr - 1.0
        cf_r = (nr * lam_r + lbi * lam_i) / den
        cf_i = (lbi * lam_r - nr * lam_i) / den
        br, bi = bre_ref[d, 0], bim_ref[d, 0]
        bbr = cf_r * br - cf_i * bi
        bbi = cf_r * bi + cf_i * br
        kcol = jnp.minimum(lax.broadcasted_iota(jnp.int32, (kpad, 1), 0), t).astype(F32)
        mk = jnp.exp(kcol * ldr)
        wr_all, wi_all = mk * jnp.cos(kcol * ldi), mk * jnp.sin(kcol * ldi)
        are_ref[d, 0] = wr_all[t:t + 1]
        aim_ref[d, 0] = wi_all[t:t + 1]
        gs.append([(bbr * wr_all[k:k + 1] - bbi * wi_all[k:k + 1], bbr * wi_all[k:k + 1] + bbi * wr_all[k:k + 1])
                   for k in range(t)])
        dtc = jnp.exp(ldtc_ref[d, 0])
        ldrc, ldic = lrc_ref[d, 0] * dtc, lic_ref[d, 0] * dtc
        krow_f = jnp.minimum(lax.broadcasted_iota(jnp.int32, (1, kpad), 1), t).astype(F32)
        mkc = jnp.exp(ldrc * krow_f)
        wrc, wic = mkc * jnp.cos(ldic * krow_f), mkc * jnp.sin(ldic * krow_f)
        ct_r = _dot(cret_ref[d, 0], tile_h, HIGHEST)
        ct_i = _dot(cimt_ref[d, 0], tile_h, HIGHEST)

        def c_lam(sel):
            er, ei = _dot(wrc, sel, HIGHEST), _dot(wic, sel, HIGHEST)
            return ct_r * er - ct_i * ei, ct_r * ei + ct_i * er

        lr_, li_ = c_lam(pick(lane_lag if d == 0 else t - 1 - lane_lag))
        z.append(_dot(bbr, lr_, HIGHEST) - _dot(bbi, li_, HIGHEST))
        er_, ei_ = c_lam(pick(lane_lag + 1 if d == 0 else t - lane_lag))
        r2_ref[0, (2 * d) * p:(2 * d + 1) * p, :] = er_.astype(BF16)
        r2_ref[0, (2 * d + 1) * p:(2 * d + 2) * p, :] = (-ei_).astype(BF16)
    zf, zb = z
    for s in range(t):
        fwd = zf if s == 0 else pltpu.roll(zf, s * h, 1)
        bwd = zb if s == t - 1 else pltpu.roll(zb, th - (t - 1 - s) * h, 1)
        taps = jnp.where(lane >= s * h, fwd, 0.0) + jnp.where(lane < (s + 1) * h, bwd, 0.0)
        inc = jnp.concatenate([gs[0][t - 1 - s][0], gs[0][t - 1 - s][1], gs[1][s][0], gs[1][s][1]], axis=1)
        r1_ref[0, s * h:(s + 1) * h, :] = jnp.concatenate([taps, inc], axis=1).astype(BF16)


def _ssm_ops(lam_re, lam_im, log_dt, b_re, b_im, c_re, c_im):
    g, p, h, t = SSM_GROUPS, SSM_STATE, SSM_GROUP, SSM_T
    row = lambda a: a.reshape(2, g, 1, p)
    col = lambda a: a.reshape(2, g, p, 1)
    ldt = jnp.broadcast_to(log_dt[:, :, None], (2, g, p))
    tr = lambda a: jnp.swapaxes(a, 2, 3)
    row_spec = pl.BlockSpec((2, 1, 1, p), lambda i: (0, i, 0, 0))
    col_spec = pl.BlockSpec((2, 1, p, 1), lambda i: (0, i, 0, 0))
    hp_spec = pl.BlockSpec((2, 1, h, p), lambda i: (0, i, 0, 0))
    ph_spec = pl.BlockSpec((2, 1, p, h), lambda i: (0, i, 0, 0))
    a_shape = jax.ShapeDtypeStruct((2, g, 1, p), F32)
    r1, r2, a_re, a_im = pl.pallas_call(
        _ssm_ops_kernel,
        grid=(g,),
        in_specs=[row_spec, row_spec, row_spec, col_spec, col_spec, col_spec, hp_spec, hp_spec, ph_spec, ph_spec],
        out_specs=[pl.BlockSpec((1, t * h, t * h + 4 * p), lambda i: (i, 0, 0)),
                   pl.BlockSpec((1, 4 * p, t * h), lambda i: (i, 0, 0)), row_spec, row_spec],
        out_shape=[jax.ShapeDtypeStruct((g, t * h, t * h + 4 * p), BF16),
                   jax.ShapeDtypeStruct((g, 4 * p, t * h), BF16), a_shape, a_shape],
        compiler_params=_cparams(("arbitrary",)),
    )(row(lam_re), row(lam_im), row(ldt), col(lam_re), col(lam_im), col(ldt), tr(b_re), tr(b_im), tr(c_re), tr(c_im))
    flat = lambda a: a.reshape(1, g * p)
    return r1, r2, flat(a_re[0]), flat(a_im[0]), flat(a_re[1]), flat(a_im[1])


_LANE_GROUPS = 128 // SSM_GROUP
_ROWS8 = 8


def _block_transpose(xs):
    h = SSM_GROUP
    blk = lax.broadcasted_iota(jnp.int32, xs[0].shape, 1) // h
    xs = list(xs)
    for d in (4, 2, 1):
        upper = (blk & d) != 0
        for i in range(_LANE_GROUPS):
            if i & d:
                continue
            a, b = xs[i], xs[i + d]
            xs[i] = jnp.where(upper, pltpu.roll(b, d * h, 1), a)
            xs[i + d] = jnp.where(upper, b, pltpu.roll(a, 128 - d * h, 1))
    return xs


def _sublane_transpose(xs):
    sub = lax.broadcasted_iota(jnp.int32, xs[0].shape, 0)
    xs = list(xs)
    for d in (4, 2, 1):
        upper = (sub & d) != 0
        for i in range(_ROWS8):
            if i & d:
                continue
            a, b = xs[i], xs[i + d]
            xs[i] = jnp.where(upper, pltpu.roll(b, d, 0), a)
            xs[i + d] = jnp.where(upper, b, pltpu.roll(a, _ROWS8 - d, 0))
    return xs


def _to_groups_kernel(su_ref, ug_ref, *, nc):
    t, h = SSM_T, SSM_GROUP

    def body(rb, carry):
        base = pl.multiple_of(rb * (_ROWS8 * t), _ROWS8 * t)
        rows = pl.ds(pl.multiple_of(rb * _ROWS8, _ROWS8), _ROWS8)
        for j in range(t * h // 128):
            toks = [su_ref[0, pl.ds(pl.multiple_of(base + c * t + _ROWS8 * j, _ROWS8), _ROWS8), :]
                    for c in range(_ROWS8)]
            for gm, tile in enumerate(_block_transpose(_sublane_transpose(toks))):
                ug_ref[gm, rows, j * 128:(j + 1) * 128] = tile
        return carry

    lax.fori_loop(0, nc // _ROWS8, body, 0)


def _to_groups(su):
    bsz, nt, w = su.shape
    nc = nt // SSM_T
    return pl.pallas_call(
        functools.partial(_to_groups_kernel, nc=nc),
        grid=(bsz, w // 128),
        in_specs=[pl.BlockSpec((1, nt, 128), lambda b, a: (b, 0, a))],
        out_specs=pl.BlockSpec((_LANE_GROUPS, nc, SSM_T * SSM_GROUP), lambda b, a: (a, b, 0)),
        out_shape=jax.ShapeDtypeStruct((SSM_GROUPS, bsz * nc, SSM_T * SSM_GROUP), F32),
        compiler_params=_cparams(("arbitrary", "arbitrary")),
    )(su)


def _from_groups_kernel(yg_ref, y_ref, *, nc):
    t, h = SSM_T, SSM_GROUP

    def body(rb, carry):
        base = pl.multiple_of(rb * (_ROWS8 * t), _ROWS8 * t)
        rows = pl.ds(pl.multiple_of(rb * _ROWS8, _ROWS8), _ROWS8)
        for j in range(t * h // 128):
            tiles = [yg_ref[gm, rows, j * 128:(j + 1) * 128] for gm in range(_LANE_GROUPS)]
            for c, tile in enumerate(_sublane_transpose(_block_transpose(tiles))):
                y_ref[0, pl.ds(pl.multiple_of(base + c * t + _ROWS8 * j, _ROWS8), _ROWS8), :] = tile
        return carry

    lax.fori_loop(0, nc // _ROWS8, body, 0)


def _from_groups(yg, bsz):
    g, n, th = yg.shape
    nc = n // bsz
    nt = nc * SSM_T
    return pl.pallas_call(
        functools.partial(_from_groups_kernel, nc=nc),
        grid=(bsz, g // _LANE_GROUPS),
        in_specs=[pl.BlockSpec((_LANE_GROUPS, nc, th), lambda b, a: (a, b, 0))],
        out_specs=pl.BlockSpec((1, nt, 128), lambda b, a: (b, 0, a)),
        out_shape=jax.ShapeDtypeStruct((bsz, nt, g * SSM_GROUP), F32),
        compiler_params=_cparams(("arbitrary", "arbitrary")),
    )(yg)


def _ssm_phase1_kernel(u_ref, r1_ref, y_ref, fr_ref, fi_ref, br_ref, bi_ref):
    th = SSM_T * SSM_GROUP
    p = SSM_STATE
    outs = [_dot(u_ref[k].astype(BF16), r1_ref[k]) for k in range(2)]
    for k in range(2):
        y_ref[k] = outs[k][:, 0:th]
    for idx, ref in enumerate((fr_ref, fi_ref, br_ref, bi_ref)):
        lo = th + idx * p
        ref[...] = jnp.concatenate([o[:, lo:lo + p] for o in outs], axis=1)


def _ssm_phase1(ug, r1):
    g, n, th = ug.shape
    p = SSM_STATE
    plane = pl.BlockSpec((n, 2 * p), lambda i: (0, i))
    plane_shape = jax.ShapeDtypeStruct((n, g * p), F32)
    return pl.pallas_call(
        _ssm_phase1_kernel,
        grid=(g // 2,),
        in_specs=[pl.BlockSpec((2, n, th), lambda i: (i, 0, 0)),
                  pl.BlockSpec((2, th, th + 4 * p), lambda i: (i, 0, 0))],
        out_specs=[pl.BlockSpec((2, n, th), lambda i: (i, 0, 0)), plane, plane, plane, plane],
        out_shape=[jax.ShapeDtypeStruct((g, n, th), F32), plane_shape, plane_shape, plane_shape, plane_shape],
        compiler_params=_cparams(("arbitrary",)),
    )(ug, r1)


def _ssm_scan_kernel(dfr_ref, dfi_ref, dbr_ref, dbi_ref, afr_ref, afi_ref, abr_ref, abi_ref,
                     sfr_ref, sfi_ref, sbr_ref, sbi_ref, *, nc, bsz):
    width = dfr_ref.shape[1]

    def run(dr_ref, di_ref, ar_ref, ai_ref, or_ref, oi_ref, chunk_of_step):
        ar = ar_ref[...]
        ai = ai_ref[...]

        def body(i, s):
            sr, si = s
            rows = pl.ds(chunk_of_step(i), bsz, stride=nc)
            or_ref[rows, :] = sr
            oi_ref[rows, :] = si
            return (ar * sr - ai * si + dr_ref[rows, :], ar * si + ai * sr + di_ref[rows, :])

        zero = jnp.zeros((bsz, width), F32)
        lax.fori_loop(0, nc, body, (zero, zero))

    run(dfr_ref, dfi_ref, afr_ref, afi_ref, sfr_ref, sfi_ref, lambda i: i)
    run(dbr_ref, dbi_ref, abr_ref, abi_ref, sbr_ref, sbi_ref,
        lambda i: jnp.where(i < SSM_NCTX, SSM_NCTX - 1 - i, nc + SSM_NCTX - 1 - i))


def _ssm_scan(planes, decays, bsz):
    n, width = planes[0].shape
    blk = pl.BlockSpec((n, 128), lambda i: (0, i))
    arow = pl.BlockSpec((1, 128), lambda i: (0, i))
    shape = jax.ShapeDtypeStruct((n, width), F32)
    return pl.pallas_call(
        functools.partial(_ssm_scan_kernel, nc=n // bsz, bsz=bsz),
        grid=(width // 128,),
        in_specs=[blk] * 4 + [arow] * 4,
        out_specs=[blk] * 4,
        out_shape=[shape] * 4,
        compiler_params=_cparams(("arbitrary",)),
    )(*planes, *decays)


def _ssm_phase2_kernel(y_ref, sfr_ref, sfi_ref, sbr_ref, sbi_ref, r2_ref, o_ref):
    p = SSM_STATE
    for k in range(2):
        acc = y_ref[k]
        for idx, ref in enumerate((sfr_ref, sfi_ref, sbr_ref, sbi_ref)):
            acc = acc + _dot(ref[:, k * p:(k + 1) * p].astype(BF16), r2_ref[k, idx * p:(idx + 1) * p, :])
        o_ref[k] = acc


def _ssm_phase2(yg, states, r2):
    g, n, th = yg.shape
    p = SSM_STATE
    plane = pl.BlockSpec((n, 2 * p), lambda i: (0, i))
    return pl.pallas_call(
        _ssm_phase2_kernel,
        grid=(g // 2,),
        in_specs=[pl.BlockSpec((2, n, th), lambda i: (i, 0, 0)), plane, plane, plane, plane,
                  pl.BlockSpec((2, 4 * p, th), lambda i: (i, 0, 0))],
        out_specs=pl.BlockSpec((2, n, th), lambda i: (i, 0, 0)),
        out_shape=jax.ShapeDtypeStruct((g, n, th), F32),
        compiler_params=_cparams(("arbitrary",)),
    )(yg, *states, r2)


def _ssm_mix(su_all, ops):
    r1, r2 = ops[0], ops[1]
    bsz = su_all.shape[0]
    ug = _to_groups(su_all)
    y1, *planes = _ssm_phase1(ug, r1)
    states = _ssm_scan(planes, ops[2:], bsz)
    return _from_groups(_ssm_phase2(y1, states, r2), bsz)


def _gla_direction(qk, v, z, wg, bg, s_ref, d, o_ref, reverse):
    tb = qk.shape[0]
    c_len = GLA_CHUNK
    x = _dot(z.astype(BF16), wg) + bg
    gate = (jnp.minimum(x, 0.0) - jnp.log(1.0 + jnp.exp(-jnp.abs(x)))) * (1.0 / GLA_TAU)
    ti = lax.broadcasted_iota(jnp.int32, (tb, tb), 0)
    si = lax.broadcasted_iota(jnp.int32, (tb, tb), 1)
    order = (si >= ti) if reverse else (si <= ti)
    tri = jnp.where(jnp.logical_and(order, ti // c_len == si // c_len), 1.0, 0.0).astype(F32)
    bcum_all = _dot(tri, gate, HIGHEST)
    q_all = (qk[:, 0:GLA_QK_W] * (GLA_DK ** -0.5) * jnp.exp(bcum_all)).astype(BF16)
    k_all = qk[:, GLA_QK_W:2 * GLA_QK_W]
    kin_all = k_all * jnp.exp(-bcum_all)
    v_bf = v.astype(BF16)
    nh = GLA_HEADS
    head_of = lambda shape, axis, size: lax.broadcasted_iota(jnp.int32, shape, axis) // size
    kmask = head_of((nh * c_len, GLA_QK_W), 0, c_len) == head_of((nh * c_len, GLA_QK_W), 1, GLA_DK)
    vmask = head_of((nh * c_len, GLA_V_W), 0, c_len) == head_of((nh * c_len, GLA_V_W), 1, GLA_DV)
    smask = head_of((GLA_V_W, GLA_QK_W), 0, GLA_DV) == head_of((GLA_V_W, GLA_QK_W), 1, GLA_DK)
    tq = lax.broadcasted_iota(jnp.int32, (c_len, nh * c_len), 0)
    sk = lax.broadcasted_iota(jnp.int32, (c_len, nh * c_len), 1) % c_len
    keep = (sk >= tq) if reverse else (sk <= tq)
    chunks = range(tb // c_len)
    for c in (reversed(chunks) if reverse else chunks):
        rs = slice(c * c_len, (c + 1) * c_len)
        bcum = bcum_all[rs]
        blast = bcum[0:1] if reverse else bcum[c_len - 1:c_len]
        k_up = (k_all[rs] * jnp.exp(blast - bcum)).astype(BF16)
        q_in = q_all[rs]
        k_exp = jnp.where(kmask, jnp.concatenate([kin_all[rs]] * nh, axis=0), 0.0).astype(BF16)
        v_exp = jnp.where(vmask, jnp.concatenate([v[rs]] * nh, axis=0), 0.0).astype(BF16)
        att = jnp.where(keep, _dot_nt(q_in, k_exp), 0.0)
        st = s_ref[d]
        o_ref[rs, :] = _dot(att.astype(BF16), v_exp) + _dot_nt(q_in, st.astype(BF16))
        s_ref[d] = st * jnp.exp(blast) + jnp.where(smask, _dot_tn(v_bf[rs], k_up), 0.0)


def _gla_kernel(qkf_ref, vf_ref, zf_ref, qkb_ref, vb_ref, zb_ref, wg_ref, bg_ref, of_ref, ob_ref, s_ref):
    @pl.when(pl.program_id(1) == 0)
    def _():
        s_ref[...] = jnp.zeros_like(s_ref)

    for bi in range(qkf_ref.shape[0]):
        _gla_direction(qkf_ref[bi], vf_ref[bi], zf_ref[bi], wg_ref[0], bg_ref[0], s_ref.at[bi], 0, of_ref.at[bi], False)
        _gla_direction(qkb_ref[bi], vb_ref[bi], zb_ref[bi], wg_ref[1], bg_ref[1], s_ref.at[bi], 1, ob_ref.at[bi], True)


def _gla(gqk, gv, gz, wg_pad, bg):
    bsz, nt, _ = gqk.shape
    tb = TOKEN_TILE
    nb = BATCH_TILE
    tiles = nt // tb
    fwd = lambda b, i: (b, i, 0)
    bwd = lambda b, i: (b, jnp.where(i == 0, 0, tiles - i), 0)
    spec = lambda width, im: pl.BlockSpec((nb, tb, width), im)
    return pl.pallas_call(
        _gla_kernel,
        grid=(bsz // nb, tiles),
        in_specs=[spec(2 * GLA_QK_W, fwd), spec(GLA_V_W, fwd), spec(GZ_PAD, fwd),
                  spec(2 * GLA_QK_W, bwd), spec(GLA_V_W, bwd), spec(GZ_PAD, bwd),
                  pl.BlockSpec((2, GZ_PAD, GLA_QK_W), lambda b, i: (0, 0, 0)),
                  pl.BlockSpec((2, 1, GLA_QK_W), lambda b, i: (0, 0, 0))],
        out_specs=[spec(GLA_V_W, fwd), spec(GLA_V_W, bwd)],
        out_shape=[jax.ShapeDtypeStruct((bsz, nt, GLA_V_W), F32), jax.ShapeDtypeStruct((bsz, nt, GLA_V_W), F32)],
        scratch_shapes=[pltpu.VMEM((nb, 2, GLA_V_W, GLA_QK_W), F32)],
        compiler_params=_cparams(("arbitrary", "arbitrary")),
    )(gqk, gv, gz, gqk, gv, gz, wg_pad, bg)


def _outproj_kernel(ctx_ref, x_ref, m_ref, att_ref, y_ref, u_ref, of_ref, ob_ref, r_ref,
                    ssmd_ref, wglu_ref, bglu_ref, gng_ref, wo_ref, ln1g_ref, ln1b_ref, router_ref,
                    x1_ref, h2_ref, aff_ref, *, t0):
    nb, tm, d = x_ref.shape
    x_in = jnp.where(pl.program_id(1) + t0 == 0, ctx_ref[...], x_ref[...])
    rows = nb * tm
    flat = lambda ref: ref[...].reshape(rows, ref.shape[-1])
    m = m_ref[0]
    zin = flat(y_ref) + ssmd_ref[...] * flat(u_ref)
    z = 0.5 * zin * (1.0 + jnp.tanh(math.sqrt(2.0 / math.pi) * (zin + 0.044715 * (zin * zin * zin))))
    ssm = z * _sigmoid(_dot(z.astype(BF16), wglu_ref[...]) + bglu_ref[...])
    o = flat(of_ref) + flat(ob_ref)
    r = flat(r_ref)
    gng = gng_ref[...]
    mix = [flat(att_ref), ssm.astype(BF16)]
    for hd in range(GLA_HEADS):
        vs = slice(hd * GLA_DV, (hd + 1) * GLA_DV)
        oh = o[:, vs]
        rh = r[:, vs]
        oh = oh * lax.rsqrt(jnp.mean(oh * oh, axis=-1, keepdims=True) + LN_EPS) * gng
        mix.append((oh * (rh * _sigmoid(rh))).astype(BF16))
    proj = _dot(jnp.concatenate(mix, axis=1), wo_ref[...]).reshape(nb, tm, d)
    x1 = _ln(ALPHA * x_in + m[:, 2:3] * proj) * ln1g_ref[...] + ln1b_ref[...]
    x1_ref[...] = x1
    h2 = (_ln(x1) * (1.0 + m[:, 4:5]) + m[:, 3:4]).reshape(rows, d)
    h_hi = h2.astype(BF16)
    h2_ref[...] = h_hi.reshape(nb, tm, d)
    h_lo = (h2 - h_hi.astype(F32)).astype(BF16)
    l_hi = _dot(h_hi, router_ref[...])
    logits = l_hi + pltpu.roll(l_hi, 128 - N_EXPERTS, 1) + _dot(h_lo, router_ref[...])
    col = lax.broadcasted_iota(jnp.int32, logits.shape, 1)
    logits = jnp.where(col < N_EXPERTS, logits, NEG_INF)
    e = jnp.exp(logits - jnp.max(logits, axis=-1, keepdims=True))
    aff_ref[...] = (e / jnp.sum(e, axis=-1, keepdims=True)).reshape(nb, tm, 128)


def _outproj(ctx_src, x_src, x_off, mods, att, y, su, o_f, o_b, gr, ssm_d, w_glu, b_glu, gn_g, w_out, ln1_g, ln1_b,
             router_split, with_ctx):
    bsz, nt, _ = y.shape
    d = x_src.shape[-1]
    tm = TOKEN_TILE
    nb = BATCH_TILE
    t0 = 0 if with_ctx else CTX_LEN // tm
    tiles = nt // tm - t0
    att_t0 = 0 if with_ctx else -t0
    tok_in = lambda width: pl.BlockSpec((nb, tm, width), lambda b, i: (b, i + t0, 0))
    tok_out = lambda width: pl.BlockSpec((nb, tm, width), lambda b, i: (b, i, 0))
    const = lambda shape: pl.BlockSpec(shape, lambda b, i: tuple(0 for _ in shape))
    once = lambda shape: pl.BlockSpec(shape, lambda b, i: tuple(0 for _ in shape), pipeline_mode=pl.Buffered(1))
    shp = lambda width, dt: jax.ShapeDtypeStruct((bsz, tiles * tm, width), dt)
    return pl.pallas_call(
        functools.partial(_outproj_kernel, t0=t0),
        grid=(bsz // nb, tiles),
        in_specs=[
            pl.BlockSpec((nb, tm, d), lambda b, i: (b, 0, 0)),
            pl.BlockSpec((nb, tm, d), lambda b, i: (b, jnp.maximum(i + t0 - x_off, 0), 0)),
            pl.BlockSpec((1, nb, N_MOD, d), lambda b, i: (jnp.minimum(i + t0, 1), b, 0, 0)),
            pl.BlockSpec((nb, tm, ATT_Q_W), lambda b, i: (b, i + t0 + att_t0, 0)),
            tok_in(SSM_WIDTH), tok_in(SSM_WIDTH), tok_in(GLA_V_W), tok_in(GLA_V_W), tok_in(GLA_V_W),
            const((1, SSM_WIDTH)), const((SSM_WIDTH, SSM_WIDTH)), const((1, SSM_WIDTH)), const((1, GLA_DV)),
            once((d, d)), const((1, d)), const((1, d)), const((d, 128)),
        ],
        out_specs=[tok_out(d), tok_out(d), tok_out(128)],
        out_shape=[shp(d, F32), shp(d, BF16), shp(128, F32)],
        compiler_params=_cparams(("arbitrary", "arbitrary")),
    )(ctx_src, x_src, mods, att, y, su, o_f, o_b, gr, ssm_d, w_glu, b_glu, gn_g, w_out, ln1_g, ln1_b, router_split)


def _expert_kernel(x_ref, g_ref, wg_ref, wu_ref, wd_ref, o_ref, hid_ref, *, nf, tf):
    s = pl.program_id(1)

    @pl.when(s < nf)
    def _():
        x = x_ref[0]
        a = _dot(x, wg_ref[0, 0].astype(BF16))
        u = _dot(x, wu_ref[0, 0].astype(BF16))
        col = pl.multiple_of(s * tf, tf)
        hid_ref[:, pl.ds(col, tf)] = ((a * _sigmoid(a)) * u).astype(BF16)

    @pl.when(s >= nf)
    def _():
        o_ref[0] = (_dot(hid_ref[...], wd_ref[0, 0].astype(BF16)) * g_ref[0]).astype(BF16)


def _experts(xe, ge, w_gate, w_up, w_down, layer, e0):
    e, r, d = xe.shape
    ff = w_gate.shape[3]
    tf = 256
    nf = ff // tf
    nd = d // tf
    up = lambda i, s: (layer, e0 + i, 0, jnp.minimum(s, nf - 1))
    down = lambda i, s: (layer, e0 + i, 0, jnp.maximum(s - nf, 0))
    return pl.pallas_call(
        functools.partial(_expert_kernel, nf=nf, tf=tf),
        grid=(e, nf + nd),
        in_specs=[pl.BlockSpec((1, r, d), lambda i, s: (i, 0, 0)),
                  pl.BlockSpec((1, r, 1), lambda i, s: (i, 0, 0)),
                  pl.BlockSpec((1, 1, d, tf), up),
                  pl.BlockSpec((1, 1, d, tf), up),
                  pl.BlockSpec((1, 1, ff, tf), down)],
        out_specs=pl.BlockSpec((1, r, tf), lambda i, s: (i, 0, jnp.maximum(s - nf, 0))),
        out_shape=jax.ShapeDtypeStruct((e, r, d), BF16),
        scratch_shapes=[pltpu.VMEM((r, ff), BF16)],
        compiler_params=_cparams(("arbitrary", "arbitrary")),
    )(xe, ge, w_gate, w_up, w_down)


COMBINE_WIN = 64
COMBINE_ALIGN = 16


def _combine_kernel(lo_ref, hi_ref, x1_ref, m_ref, g_ref, b_ref, tok_hbm, *rest, tiles, nsteps, group_rows):
    ye_groups, (o_ref, buf, tokbuf, sem) = rest[:-4], rest[-4:]
    _combine_body(lo_ref, hi_ref, x1_ref, m_ref, g_ref, b_ref, tok_hbm, ye_groups, o_ref, buf, tokbuf, sem,
                  tiles, nsteps, group_rows)


def _combine_body(lo_ref, hi_ref, x1_ref, m_ref, g_ref, b_ref, tok_hbm, ye_groups, o_ref, buf, tokbuf, sem,
                  tiles, nsteps, group_rows):
    n_exp = N_EXPERTS
    win = COMBINE_WIN
    tm = x1_ref.shape[1]
    step = pl.program_id(0) * tiles + pl.program_id(1)
    slot = step % 2
    extra = 2

    per_group = n_exp // len(ye_groups)

    def first_row(st, e, k):
        u = (lo_ref[st * n_exp + e] // COMBINE_ALIGN) * COMBINE_ALIGN + k * win
        group_end = (e // per_group + 1) * group_rows
        return u, pl.multiple_of(jnp.minimum(u, group_end - win), COMBINE_ALIGN)

    def copies(st, k, dst):
        out = []
        for e in range(n_exp):
            s0 = first_row(st, e, k)[1]
            local = pl.ds(pl.multiple_of(s0 - (e // per_group) * group_rows, COMBINE_ALIGN), win)
            out.append(pltpu.make_async_copy(ye_groups[e // per_group].at[local, :], buf.at[dst, e], sem.at[dst]))
            out.append(pltpu.make_async_copy(tok_hbm.at[pl.ds(s0, win), :], tokbuf.at[dst, e], sem.at[dst]))
        return out

    @pl.when(step == 0)
    def _():
        for cp in copies(step, 0, slot):
            cp.start()

    @pl.when(step + 1 < nsteps)
    def _():
        for cp in copies(step + 1, 0, 1 - slot):
            cp.start()

    tile_tok = pl.program_id(1) * tm + lax.broadcasted_iota(jnp.int32, (win, tm), 1)
    jrow = lax.broadcasted_iota(jnp.int32, (win, 1), 0)

    def spread(k, src):
        parts = []
        for e in range(n_exp):
            u, s0 = first_row(step, e, k)
            row = s0 + jrow
            mine = jnp.logical_and(row >= jnp.maximum(u, lo_ref[step * n_exp + e]), row < hi_ref[step * n_exp + e])
            hit = jnp.logical_and(tokbuf[src, e][:, 0:1] == tile_tok, mine)
            parts.append(jnp.where(hit, 1.0, 0.0).astype(BF16))
        onehot_t = jnp.concatenate(parts, axis=0)
        return _dot_tn(onehot_t, buf[src].reshape(n_exp * win, buf.shape[-1]))

    for cp in copies(step, 0, slot):
        cp.wait()
    acc = spread(0, slot)

    rounds = jnp.int32(1)
    for e in range(n_exp):
        span = hi_ref[step * n_exp + e] - (lo_ref[step * n_exp + e] // COMBINE_ALIGN) * COMBINE_ALIGN
        rounds = jnp.maximum(rounds, (span + win - 1) // win)

    def more(k, acc):
        for cp in copies(step, k, extra):
            cp.start()
        for cp in copies(step, k, extra):
            cp.wait()
        return acc + spread(k, extra)

    acc = lax.fori_loop(1, rounds, more, acc)
    m = m_ref[0, 0]
    o_ref[0] = _ln(ALPHA * x1_ref[0] + m[5:6] * acc) * g_ref[...] + b_ref[...]


def _combine(lo, hi, x1, mods, ln2_g, ln2_b, tok_of_row, ye_groups, with_ctx):
    bsz, nt, d = x1.shape
    tm = TOKEN_TILE
    tiles = nt // tm
    kind0 = 0 if with_ctx else 1
    tok = lambda width: pl.BlockSpec((1, tm, width), lambda b, i, lo, hi: (b, i, 0))
    vec = pl.BlockSpec((1, d), lambda b, i, lo, hi: (0, 0))
    hbm = pl.BlockSpec(memory_space=pl.ANY)
    return pl.pallas_call(
        functools.partial(_combine_kernel, tiles=tiles, nsteps=bsz * tiles, group_rows=ye_groups[0].shape[0]),
        grid_spec=pltpu.PrefetchScalarGridSpec(
            num_scalar_prefetch=2,
            grid=(bsz, tiles),
            in_specs=[tok(d),
                      pl.BlockSpec((1, 1, N_MOD, d), lambda b, i, lo, hi: (jnp.minimum(i + kind0, 1), b, 0, 0)),
                      vec, vec, hbm] + [hbm] * len(ye_groups),
            out_specs=tok(d),
            scratch_shapes=[pltpu.VMEM((3, N_EXPERTS, COMBINE_WIN, d), BF16),
                            pltpu.VMEM((3, N_EXPERTS, COMBINE_WIN, 128), jnp.int32),
                            pltpu.SemaphoreType.DMA((3,))],
        ),
        out_shape=jax.ShapeDtypeStruct((bsz, nt, d), F32),
        compiler_params=_cparams(("arbitrary", "arbitrary")),
    )(lo, hi, x1, mods, ln2_g, ln2_b, tok_of_row, *ye_groups)


def _route(aff, cap):
    gate, idx = lax.top_k(jnp.swapaxes(aff, 1, 2), cap)
    return gate, idx


def _rope_tables(seq):
    half = HEAD_DIM // 2
    nf = half // 2
    inv = ROPE_BASE ** (-jnp.arange(nf, dtype=F32) / nf)
    pos = jnp.arange(seq)
    ang_r = (pos // GRID_W).astype(F32)[:, None] * inv
    ang_c = (pos % GRID_W).astype(F32)[:, None] * inv
    zeros = jnp.zeros_like(ang_r)
    cos = jnp.concatenate([jnp.cos(ang_r), jnp.cos(ang_r), jnp.cos(ang_c), jnp.cos(ang_c)], axis=-1)
    sa = jnp.concatenate([-jnp.sin(ang_r), zeros, -jnp.sin(ang_c), zeros], axis=-1)
    sb = jnp.concatenate([zeros, jnp.sin(ang_r), zeros, jnp.sin(ang_c)], axis=-1)
    ctx = lambda fill: jnp.full((CTX_LEN, HEAD_DIM), fill, F32)
    return (jnp.concatenate([ctx(1.0), cos], axis=0), jnp.concatenate([ctx(0.0), sa], axis=0),
            jnp.concatenate([ctx(0.0), sb], axis=0))


def kernel(x, c, ctx, c_ctx, w_ada, b_ada, w_in, attn_sink, ssm_lam_re, ssm_lam_im, ssm_log_dt,
           ssm_b_re, ssm_b_im, ssm_c_re, ssm_c_im, ssm_d, ssm_w_glu, ssm_b_glu, gla_w_gate,
           gla_b_gate, gla_norm_g, w_out, ln1_g, ln1_b, ln2_g, ln2_b, router, exp_w_gate,
           exp_w_up, exp_w_down):
    bsz, seq, d = x.shape
    n_exp = router.shape[-1]
    cc = jnp.zeros((8, d), F32).at[:bsz].set(c).at[bsz].set(c_ctx)
    mod_all = _ada(cc, w_ada, b_ada)
    cos_t, sa_t, sb_t = _rope_tables(seq)
    ctx_src, x_src, x_off = ctx, x, CTX_LEN // TOKEN_TILE
    nt = CTX_LEN + seq
    bidx = jnp.arange(bsz)[None, :, None]

    for l in range(DEPTH):
        with_ctx = l < DEPTH - 1
        m = mod_all[l].reshape(8, N_MOD, d)
        mods = jnp.stack([jnp.broadcast_to(m[bsz], (bsz, N_MOD, d)), m[:bsz]], axis=0)
        w_in_p = jnp.pad(w_in[l], ((0, 0), (0, N_IN_PAD - N_IN))).astype(BF16)
        q, k, v, su, gqk, gv, gr, gz = _inproj(ctx_src, x_src, x_off, nt, mods, cos_t, sa_t, sb_t, w_in_p)

        att = _attention(attn_sink[l], q, k, v, with_ctx)

        ops = _ssm_ops(ssm_lam_re[l], ssm_lam_im[l], ssm_log_dt[l], ssm_b_re[l], ssm_b_im[l], ssm_c_re[l], ssm_c_im[l])
        y = _ssm_mix(su, ops)

        wg_pad = jnp.zeros((2, GZ_PAD, GLA_QK_W), F32)
        wg_pad = wg_pad.at[0, 0:GLA_RANK].set(gla_w_gate[l, 0]).at[1, GLA_RANK:2 * GLA_RANK].set(gla_w_gate[l, 1])
        o_f, o_b = _gla(gqk, gv, gz, wg_pad.astype(BF16), gla_b_gate[l][:, None, :])

        r_hi = router[l].astype(BF16)
        r_lo = (router[l] - r_hi.astype(F32)).astype(BF16)
        router_split = jnp.pad(jnp.concatenate([r_hi, r_lo], axis=1), ((0, 0), (0, 128 - 2 * n_exp)))
        x1, h2, aff = _outproj(ctx_src, x_src, x_off, mods, att, y, su, o_f, o_b, gr,
                               ssm_d[l][None], ssm_w_glu[l].astype(BF16), ssm_b_glu[l][None], gla_norm_g[l][None],
                               w_out[l].astype(BF16), ln1_g[l][None], ln1_b[l][None], router_split, with_ctx)

        t0 = CTX_LEN if with_ctx else 0
        cap = CAPACITY_FACTOR * seq // n_exp
        gate, idx = _route(aff[:, t0:, :n_exp], cap)
        idx = idx + t0
        if with_ctx:
            cap_c = CAPACITY_FACTOR * CTX_LEN // n_exp
            gate_c, idx_c = _route(aff[:, :CTX_LEN, :n_exp], cap_c)
            gate = jnp.concatenate([gate, gate_c], axis=-1)
            idx = jnp.concatenate([idx, idx_c], axis=-1)
        idx, gate = lax.sort((idx, gate), dimension=-1, num_keys=1)
        idx_e = jnp.swapaxes(idx, 0, 1)
        gate_e = jnp.swapaxes(gate, 0, 1)
        slots = idx_e.shape[-1]
        rows_e = bsz * slots
        ye_groups = []
        eg = n_exp // EXPERT_GROUPS
        for grp in range(EXPERT_GROUPS):
            sl = slice(grp * eg, (grp + 1) * eg)
            xe = h2[bidx, idx_e[sl]].reshape(eg, rows_e, d)
            ye = _experts(xe, gate_e[sl].reshape(eg, rows_e, 1), exp_w_gate, exp_w_up, exp_w_down, l, grp * eg)
            ye_groups.append(ye.reshape(eg * rows_e, d))
        nt_out = x1.shape[1]
        bounds = jnp.arange(nt_out // TOKEN_TILE + 1) * TOKEN_TILE
        cnt = jnp.sum(idx_e[..., None] < bounds, axis=2).astype(jnp.int32)
        base = (jnp.arange(n_exp)[:, None, None] * rows_e + jnp.arange(bsz)[None, :, None] * slots).astype(jnp.int32)
        lo = jnp.transpose(base + cnt[:, :, :-1], (1, 2, 0)).reshape(-1)
        hi = jnp.transpose(base + cnt[:, :, 1:], (1, 2, 0)).reshape(-1)
        tok_of_row = jnp.broadcast_to(idx_e.reshape(n_exp * rows_e, 1).astype(jnp.int32), (n_exp * rows_e, 128))
        x_new = _combine(lo, hi, x1, mods, ln2_g[l][None], ln2_b[l][None], tok_of_row, ye_groups, with_ctx)
        ctx_src, x_src, x_off = x_new, x_new, 0
    return x_new
```

```python
import functools
import math

import jax
import jax.numpy as jnp
from jax import lax
from jax.experimental import pallas as pl
from jax.experimental.pallas import tpu as pltpu

F32 = jnp.float32
BF16 = jnp.bfloat16
HIGHEST = lax.Precision.HIGHEST

D_MODEL = 2048
DEPTH = 2
GRID_W = 64
CTX_LEN = 256
ATT_HEADS = 8
ATT_KV_HEADS = 2
HEAD_DIM = 128
ATT_BLOCK = 128
ATT_QBLOCKS = 2
ROPE_BASE = 10000.0
SSM_WIDTH = 512
SSM_GROUP = 16
SSM_GROUPS = 32
SSM_STATE = 64
GLA_HEADS = 4
GLA_DK = 64
GLA_DV = 128
GLA_RANK = 16
GLA_TAU = 16.0
GLA_CHUNK = 64
N_EXPERTS = 16
EXPERT_FF = 2048
CAPACITY_FACTOR = 2
N_MOD = 6
ALPHA = (2 * DEPTH) ** 0.25
LN_EPS = 1e-6
NEG_INF = -1e30

ATT_Q_W = ATT_HEADS * HEAD_DIM
ATT_KV_W = ATT_KV_HEADS * HEAD_DIM
GLA_QK_W = GLA_HEADS * GLA_DK
GLA_V_W = GLA_HEADS * GLA_DV
N_IN = 3616
N_IN_PAD = 3712
GZ_PAD = N_IN_PAD - 3584

TOKEN_TILE = 256
BATCH_TILE = 2
EXPERT_GROUPS = 4
SSM_T = 32
SSM_OPS_GROUPS = 2
SSM_NCTX = CTX_LEN // SSM_T
VMEM_LIMIT = 56 * 1024 * 1024


def _cparams(sem, vmem=VMEM_LIMIT):
    return pltpu.CompilerParams(dimension_semantics=sem, vmem_limit_bytes=vmem)


def _dot(a, b, precision=None):
    return jnp.dot(a, b, preferred_element_type=F32, precision=precision)


def _dot_nt(a, b, precision=None):
    return lax.dot_general(a, b, (((1,), (1,)), ((), ())), preferred_element_type=F32, precision=precision)


def _dot_tn(a, b, precision=None):
    return lax.dot_general(a, b, (((0,), (0,)), ((), ())), preferred_element_type=F32, precision=precision)


def _sigmoid(x):
    return 1.0 / (1.0 + jnp.exp(-x))


def _ln(x):
    mu = jnp.mean(x, axis=-1, keepdims=True)
    xc = x - mu
    var = jnp.mean(xc * xc, axis=-1, keepdims=True)
    return xc * lax.rsqrt(var + LN_EPS)


def _ada_kernel(c_ref, w_ref, b_ref, o_ref):
    c = c_ref[...]
    s = c * _sigmoid(c)
    w = w_ref[0]
    w_hi = w.astype(BF16)
    w_lo = (w - w_hi.astype(F32)).astype(BF16)
    s_hi = s.astype(BF16)
    s_lo = (s - s_hi.astype(F32)).astype(BF16)
    both = _dot(jnp.concatenate([s_hi, s_lo], axis=0), w_hi)
    rows = s.shape[0]
    o_ref[0] = both[0:rows] + both[rows:2 * rows] + _dot(s_hi, w_lo) + b_ref[0]


def _ada(cc, w_ada, b_ada):
    depth, d, n = w_ada.shape
    tn = 1024
    return pl.pallas_call(
        _ada_kernel,
        grid=(depth, n // tn),
        in_specs=[
            pl.BlockSpec((8, d), lambda l, j: (0, 0)),
            pl.BlockSpec((1, d, tn), lambda l, j: (l, 0, j)),
            pl.BlockSpec((1, 1, tn), lambda l, j: (l, 0, j)),
        ],
        out_specs=pl.BlockSpec((1, 8, tn), lambda l, j: (l, 0, j)),
        out_shape=jax.ShapeDtypeStruct((depth, 8, n), F32),
        compiler_params=_cparams(("arbitrary", "arbitrary")),
    )(cc, w_ada, b_ada.reshape(depth, 1, n))


def _inproj_kernel(ctx_ref, x_ref, m_ref, cos_ref, sa_ref, sb_ref, w_ref,
                   q_ref, k_ref, v_ref, su_ref, gqk_ref, gv_ref, gr_ref, gz_ref):
    nb, tm, d = x_ref.shape
    rows = nb * tm
    m = m_ref[0]
    x = jnp.where(pl.program_id(1) == 0, ctx_ref[...], x_ref[...])
    h = _ln(x) * (1.0 + m[:, 1:2]) + m[:, 0:1]
    h = h.reshape(rows, d).astype(BF16)
    tile = lambda ref: jnp.concatenate([ref[...]] * nb, axis=0)
    cos = tile(cos_ref)
    sa = tile(sa_ref)
    sb = tile(sb_ref)

    def rope(t):
        return t * cos + pltpu.roll(t, 96, 1) * sa + pltpu.roll(t, 32, 1) * sb

    def put(ref, val, sl=slice(None)):
        ref[:, :, sl] = val.reshape(nb, tm, val.shape[-1])

    scale = HEAD_DIM ** -0.5
    q = _dot(h, w_ref[:, 0:ATT_Q_W])
    for hd in range(ATT_HEADS):
        sl = slice(hd * HEAD_DIM, (hd + 1) * HEAD_DIM)
        put(q_ref, (rope(q[:, sl]) * scale).astype(BF16), sl)
    kk = _dot(h, w_ref[:, 1024:1280])
    for hd in range(ATT_KV_HEADS):
        sl = slice(hd * HEAD_DIM, (hd + 1) * HEAD_DIM)
        put(k_ref, rope(kk[:, sl]).astype(BF16), sl)
    put(v_ref, _dot(h, w_ref[:, 1280:1536]).astype(BF16))
    put(su_ref, _dot(h, w_ref[:, 1536:2048]))
    put(gqk_ref, _dot(h, w_ref[:, 2048:2560]))
    put(gv_ref, _dot(h, w_ref[:, 2560:3072]))
    put(gr_ref, _dot(h, w_ref[:, 3072:3584]))
    put(gz_ref, _dot(h, w_ref[:, 3584:N_IN_PAD]))


def _inproj(ctx_src, x_src, x_off, nt, mods, cos_t, sa_t, sb_t, w_in_bf16):
    bsz, _, d = x_src.shape
    tm = TOKEN_TILE
    nb = BATCH_TILE
    tiles = nt // tm
    tok = lambda width: pl.BlockSpec((nb, tm, width), lambda b, i: (b, i, 0))
    tab = pl.BlockSpec((tm, HEAD_DIM), lambda b, i: (i, 0))
    shp = lambda width, dt: jax.ShapeDtypeStruct((bsz, nt, width), dt)
    return pl.pallas_call(
        _inproj_kernel,
        grid=(bsz // nb, tiles),
        in_specs=[
            pl.BlockSpec((nb, tm, d), lambda b, i: (b, 0, 0)),
            pl.BlockSpec((nb, tm, d), lambda b, i: (b, jnp.maximum(i - x_off, 0), 0)),
            pl.BlockSpec((1, nb, N_MOD, d), lambda b, i: (jnp.minimum(i, 1), b, 0, 0)),
            tab, tab, tab,
            pl.BlockSpec((d, N_IN_PAD), lambda b, i: (0, 0), pipeline_mode=pl.Buffered(1)),
        ],
        out_specs=[tok(ATT_Q_W), tok(ATT_KV_W), tok(ATT_KV_W), tok(SSM_WIDTH),
                   tok(2 * GLA_QK_W), tok(GLA_V_W), tok(GLA_V_W), tok(GZ_PAD)],
        out_shape=[shp(ATT_Q_W, BF16), shp(ATT_KV_W, BF16), shp(ATT_KV_W, BF16), shp(SSM_WIDTH, F32),
                   shp(2 * GLA_QK_W, F32), shp(GLA_V_W, F32), shp(GLA_V_W, F32), shp(GZ_PAD, F32)],
        compiler_params=_cparams(("arbitrary", "arbitrary")),
    )(ctx_src, x_src, mods, cos_t, sa_t, sb_t, w_in_bf16)


def _attn_kernel(sink_ref, q_ref, k_ref, v_ref, o_ref, *, blk0, nblk):
    for sub in range(ATT_QBLOCKS):
        _attn_block(sink_ref, q_ref, k_ref, v_ref, o_ref, sub, pl.program_id(1) * ATT_QBLOCKS + sub + blk0, nblk)


def _attn_block(sink_ref, q_ref, k_ref, v_ref, o_ref, sub, qb, nblk):
    w = ATT_BLOCK
    qrows = slice(sub * w, (sub + 1) * w)
    cb = CTX_LEN // w
    n = qb - cb
    span = 3 * w
    start = jnp.clip((n - 1) * w, 0, (nblk - cb) * w - span)
    band = pl.ds(pl.multiple_of(CTX_LEN + start, w), span)
    g = ATT_HEADS // ATT_KV_HEADS
    rows = g * w
    q_pos = n * w + lax.broadcasted_iota(jnp.int32, (rows, span), 0) % w
    rel = q_pos - (start + lax.broadcasted_iota(jnp.int32, (rows, span), 1))
    in_window = jnp.logical_and(jnp.logical_and(rel <= w, rel >= -w), n >= 0)
    rowg = lax.broadcasted_iota(jnp.int32, (rows, 1), 0) // w

    for kvh in range(ATT_KV_HEADS):
        sl = slice(kvh * HEAD_DIM, (kvh + 1) * HEAD_DIM)
        qs = jnp.concatenate(
            [q_ref[0, qrows, (kvh * g + a) * HEAD_DIM:(kvh * g + a + 1) * HEAD_DIM] for a in range(g)], axis=0)
        sink = jnp.zeros((rows, 1), F32)
        for a in range(g):
            sink = jnp.where(rowg == a, sink_ref[kvh * g + a], sink)
        s_b = jnp.where(in_window, _dot_nt(qs, k_ref[0, band, sl]), NEG_INF)
        s_x = _dot_nt(qs, k_ref[0, 0:CTX_LEN, sl])
        mx = jnp.maximum(jnp.maximum(jnp.max(s_b, axis=-1, keepdims=True), jnp.max(s_x, axis=-1, keepdims=True)), sink)
        p_b = jnp.exp(s_b - mx)
        p_x = jnp.exp(s_x - mx)
        den = jnp.sum(p_b, axis=-1, keepdims=True) + jnp.sum(p_x, axis=-1, keepdims=True) + jnp.exp(sink - mx)
        o = _dot(p_b.astype(BF16), v_ref[0, band, sl]) + _dot(p_x.astype(BF16), v_ref[0, 0:CTX_LEN, sl])
        o = o / den
        for a in range(g):
            hs = slice((kvh * g + a) * HEAD_DIM, (kvh * g + a + 1) * HEAD_DIM)
            o_ref[0, qrows, hs] = o[a * w:(a + 1) * w].astype(BF16)


def _attention(sink, q_all, k_all, v_all, with_ctx):
    bsz, nt, _ = q_all.shape
    w = ATT_BLOCK
    nblk = nt // w
    blk0 = 0 if with_ctx else CTX_LEN // w
    wq = w * ATT_QBLOCKS
    step0 = blk0 // ATT_QBLOCKS
    return pl.pallas_call(
        functools.partial(_attn_kernel, blk0=blk0, nblk=nblk),
        grid=(bsz, (nblk - blk0) // ATT_QBLOCKS),
        in_specs=[
            pl.BlockSpec(memory_space=pltpu.SMEM),
            pl.BlockSpec((1, wq, ATT_Q_W), lambda b, j: (b, j + step0, 0)),
            pl.BlockSpec((1, nt, ATT_KV_W), lambda b, j: (b, 0, 0)),
            pl.BlockSpec((1, nt, ATT_KV_W), lambda b, j: (b, 0, 0)),
        ],
        out_specs=pl.BlockSpec((1, wq, ATT_Q_W), lambda b, j: (b, j, 0)),
        out_shape=jax.ShapeDtypeStruct((bsz, (nblk - blk0) * w, ATT_Q_W), BF16),
        compiler_params=_cparams(("arbitrary", "arbitrary")),
    )(sink, q_all, k_all, v_all)


def _ssm_ops_kernel(*refs):
    for gi in range(SSM_OPS_GROUPS):
        _ssm_ops_group(gi, *refs)


def _ssm_ops_group(gi, lr_ref, li_ref, ldt_ref, lrc_ref, lic_ref, ldtc_ref, bre_ref, bim_ref, cret_ref, cimt_ref,
                   r1_ref, r2_ref, are_ref, aim_ref):
    t, h, p = SSM_T, SSM_GROUP, SSM_STATE
    th = t * h
    kpad = t + _ROWS8
    lane_lag = lax.broadcasted_iota(jnp.int32, (kpad, th), 1) // h
    krow = lax.broadcasted_iota(jnp.int32, (kpad, th), 0)
    pick = lambda lag_of_lane: jnp.where(krow == lag_of_lane, 1.0, 0.0).astype(F32)
    tile_h = jnp.where(lax.broadcasted_iota(jnp.int32, (h, th), 0)
                       == lax.broadcasted_iota(jnp.int32, (h, th), 1) % h, 1.0, 0.0).astype(F32)
    lane = lax.broadcasted_iota(jnp.int32, (h, th), 1)
    z, gs = [], []
    for d in range(2):
        lam_r, lam_i = lr_ref[d, gi], li_ref[d, gi]
        dt = jnp.exp(ldt_ref[d, gi])
        ldr, ldi = lam_r * dt, lam_i * dt
        mag = jnp.exp(ldr)
        lbr, lbi = mag * jnp.cos(ldi), mag * jnp.sin(ldi)
        den = lam_r * lam_r + lam_i * lam_i
        nr = lbr - 1.0
        cf_r = (nr * lam_r + lbi * lam_i) / den
        cf_i = (lbi * lam_r - nr * lam_i) / den
        br, bi = bre_ref[d, gi], bim_ref[d, gi]
        bbr = cf_r * br - cf_i * bi
        bbi = cf_r * bi + cf_i * br
        kcol = jnp.minimum(lax.broadcasted_iota(jnp.int32, (kpad, 1), 0), t).astype(F32)
        mk = jnp.exp(kcol * ldr)
        wr_all, wi_all = mk * jnp.cos(kcol * ldi), mk * jnp.sin(kcol * ldi)
        are_ref[d, gi] = wr_all[t:t + 1]
        aim_ref[d, gi] = wi_all[t:t + 1]
        gs.append([(bbr * wr_all[k:k + 1] - bbi * wi_all[k:k + 1], bbr * wi_all[k:k + 1] + bbi * wr_all[k:k + 1])
                   for k in range(t)])
        dtc = jnp.exp(ldtc_ref[d, gi])
        ldrc, ldic = lrc_ref[d, gi] * dtc, lic_ref[d, gi] * dtc
        krow_f = jnp.minimum(lax.broadcasted_iota(jnp.int32, (1, kpad), 1), t).astype(F32)
        mkc = jnp.exp(ldrc * krow_f)
        wrc, wic = mkc * jnp.cos(ldic * krow_f), mkc * jnp.sin(ldic * krow_f)
        ct_r = _dot(cret_ref[d, gi], tile_h, HIGHEST)
        ct_i = _dot(cimt_ref[d, gi], tile_h, HIGHEST)

        def c_lam(sel):
            er, ei = _dot(wrc, sel, HIGHEST), _dot(wic, sel, HIGHEST)
            return ct_r * er - ct_i * ei, ct_r * ei + ct_i * er

        lr_, li_ = c_lam(pick(lane_lag if d == 0 else t - 1 - lane_lag))
        z.append(_dot(bbr, lr_, HIGHEST) - _dot(bbi, li_, HIGHEST))
        er_, ei_ = c_lam(pick(lane_lag + 1 if d == 0 else t - lane_lag))
        r2_ref[gi, (2 * d) * p:(2 * d + 1) * p, :] = er_.astype(BF16)
        r2_ref[gi, (2 * d + 1) * p:(2 * d + 2) * p, :] = (-ei_).astype(BF16)
    zf, zb = z
    for s in range(t):
        fwd = zf if s == 0 else pltpu.roll(zf, s * h, 1)
        bwd = zb if s == t - 1 else pltpu.roll(zb, th - (t - 1 - s) * h, 1)
        taps = jnp.where(lane >= s * h, fwd, 0.0) + jnp.where(lane < (s + 1) * h, bwd, 0.0)
        inc = jnp.concatenate([gs[0][t - 1 - s][0], gs[0][t - 1 - s][1], gs[1][s][0], gs[1][s][1]], axis=1)
        r1_ref[gi, s * h:(s + 1) * h, :] = jnp.concatenate([taps, inc], axis=1).astype(BF16)


def _ssm_ops(lam_re, lam_im, log_dt, b_re, b_im, c_re, c_im):
    g, p, h, t = SSM_GROUPS, SSM_STATE, SSM_GROUP, SSM_T
    row = lambda a: a.reshape(2, g, 1, p)
    col = lambda a: a.reshape(2, g, p, 1)
    ldt = jnp.broadcast_to(log_dt[:, :, None], (2, g, p))
    tr = lambda a: jnp.swapaxes(a, 2, 3)
    gb = SSM_OPS_GROUPS
    row_spec = pl.BlockSpec((2, gb, 1, p), lambda i: (0, i, 0, 0))
    col_spec = pl.BlockSpec((2, gb, p, 1), lambda i: (0, i, 0, 0))
    hp_spec = pl.BlockSpec((2, gb, h, p), lambda i: (0, i, 0, 0))
    ph_spec = pl.BlockSpec((2, gb, p, h), lambda i: (0, i, 0, 0))
    a_shape = jax.ShapeDtypeStruct((2, g, 1, p), F32)
    r1, r2, a_re, a_im = pl.pallas_call(
        _ssm_ops_kernel,
        grid=(g // gb,),
        in_specs=[row_spec, row_spec, row_spec, col_spec, col_spec, col_spec, hp_spec, hp_spec, ph_spec, ph_spec],
        out_specs=[pl.BlockSpec((gb, t * h, t * h + 4 * p), lambda i: (i, 0, 0)),
                   pl.BlockSpec((gb, 4 * p, t * h), lambda i: (i, 0, 0)), row_spec, row_spec],
        out_shape=[jax.ShapeDtypeStruct((g, t * h, t * h + 4 * p), BF16),
                   jax.ShapeDtypeStruct((g, 4 * p, t * h), BF16), a_shape, a_shape],
        compiler_params=_cparams(("arbitrary",)),
    )(row(lam_re), row(lam_im), row(ldt), col(lam_re), col(lam_im), col(ldt), tr(b_re), tr(b_im), tr(c_re), tr(c_im))
    flat = lambda a: a.reshape(1, g * p)
    return r1, r2, flat(a_re[0]), flat(a_im[0]), flat(a_re[1]), flat(a_im[1])


_LANE_GROUPS = 128 // SSM_GROUP
_ROWS8 = 8


def _block_transpose(xs):
    h = SSM_GROUP
    blk = lax.broadcasted_iota(jnp.int32, xs[0].shape, 1) // h
    xs = list(xs)
    for d in (4, 2, 1):
        upper = (blk & d) != 0
        for i in range(_LANE_GROUPS):
            if i & d:
                continue
            a, b = xs[i], xs[i + d]
            xs[i] = jnp.where(upper, pltpu.roll(b, d * h, 1), a)
            xs[i + d] = jnp.where(upper, b, pltpu.roll(a, 128 - d * h, 1))
    return xs


def _sublane_transpose(xs):
    sub = lax.broadcasted_iota(jnp.int32, xs[0].shape, 0)
    xs = list(xs)
    for d in (4, 2, 1):
        upper = (sub & d) != 0
        for i in range(_ROWS8):
            if i & d:
                continue
            a, b = xs[i], xs[i + d]
            xs[i] = jnp.where(upper, pltpu.roll(b, d, 0), a)
            xs[i + d] = jnp.where(upper, b, pltpu.roll(a, _ROWS8 - d, 0))
    return xs


def _to_groups_kernel(su_ref, ug_ref, *, nc):
    t, h = SSM_T, SSM_GROUP

    def body(rb, carry):
        base = pl.multiple_of(rb * (_ROWS8 * t), _ROWS8 * t)
        rows = pl.ds(pl.multiple_of(rb * _ROWS8, _ROWS8), _ROWS8)
        for j in range(t * h // 128):
            toks = [su_ref[0, pl.ds(pl.multiple_of(base + c * t + _ROWS8 * j, _ROWS8), _ROWS8), :]
                    for c in range(_ROWS8)]
            for gm, tile in enumerate(_block_transpose(_sublane_transpose(toks))):
                ug_ref[gm, rows, j * 128:(j + 1) * 128] = tile
        return carry

    lax.fori_loop(0, nc // _ROWS8, body, 0)


def _to_groups(su):
    bsz, nt, w = su.shape
    nc = nt // SSM_T
    return pl.pallas_call(
        functools.partial(_to_groups_kernel, nc=nc),
        grid=(bsz, w // 128),
        in_specs=[pl.BlockSpec((1, nt, 128), lambda b, a: (b, 0, a))],
        out_specs=pl.BlockSpec((_LANE_GROUPS, nc, SSM_T * SSM_GROUP), lambda b, a: (a, b, 0)),
        out_shape=jax.ShapeDtypeStruct((SSM_GROUPS, bsz * nc, SSM_T * SSM_GROUP), F32),
        compiler_params=_cparams(("arbitrary", "arbitrary")),
    )(su)


def _from_groups_kernel(yg_ref, y_ref, *, nc):
    t, h = SSM_T, SSM_GROUP

    def body(rb, carry):
        base = pl.multiple_of(rb * (_ROWS8 * t), _ROWS8 * t)
        rows = pl.ds(pl.multiple_of(rb * _ROWS8, _ROWS8), _ROWS8)
        for j in range(t * h // 128):
            tiles = [yg_ref[gm, rows, j * 128:(j + 1) * 128] for gm in range(_LANE_GROUPS)]
            for c, tile in enumerate(_sublane_transpose(_block_transpose(tiles))):
                y_ref[0, pl.ds(pl.multiple_of(base + c * t + _ROWS8 * j, _ROWS8), _ROWS8), :] = tile
        return carry

    lax.fori_loop(0, nc // _ROWS8, body, 0)


def _from_groups(yg, bsz):
    g, n, th = yg.shape
    nc = n // bsz
    nt = nc * SSM_T
    return pl.pallas_call(
        functools.partial(_from_groups_kernel, nc=nc),
        grid=(bsz, g // _LANE_GROUPS),
        in_specs=[pl.BlockSpec((_LANE_GROUPS, nc, th), lambda b, a: (a, b, 0))],
        out_specs=pl.BlockSpec((1, nt, 128), lambda b, a: (b, 0, a)),
        out_shape=jax.ShapeDtypeStruct((bsz, nt, g * SSM_GROUP), F32),
        compiler_params=_cparams(("arbitrary", "arbitrary")),
    )(yg)


def _ssm_phase1_kernel(u_ref, r1_ref, y_ref, fr_ref, fi_ref, br_ref, bi_ref):
    th = SSM_T * SSM_GROUP
    p = SSM_STATE
    outs = [_dot(u_ref[k].astype(BF16), r1_ref[k]) for k in range(2)]
    for k in range(2):
        y_ref[k] = outs[k][:, 0:th]
    for idx, ref in enumerate((fr_ref, fi_ref, br_ref, bi_ref)):
        lo = th + idx * p
        ref[...] = jnp.concatenate([o[:, lo:lo + p] for o in outs], axis=1)


def _ssm_phase1(ug, r1):
    g, n, th = ug.shape
    p = SSM_STATE
    plane = pl.BlockSpec((n, 2 * p), lambda i: (0, i))
    plane_shape = jax.ShapeDtypeStruct((n, g * p), F32)
    return pl.pallas_call(
        _ssm_phase1_kernel,
        grid=(g // 2,),
        in_specs=[pl.BlockSpec((2, n, th), lambda i: (i, 0, 0)),
                  pl.BlockSpec((2, th, th + 4 * p), lambda i: (i, 0, 0))],
        out_specs=[pl.BlockSpec((2, n, th), lambda i: (i, 0, 0)), plane, plane, plane, plane],
        out_shape=[jax.ShapeDtypeStruct((g, n, th), F32), plane_shape, plane_shape, plane_shape, plane_shape],
        compiler_params=_cparams(("arbitrary",)),
    )(ug, r1)


def _ssm_scan_kernel(dfr_ref, dfi_ref, dbr_ref, dbi_ref, afr_ref, afi_ref, abr_ref, abi_ref,
                     sfr_ref, sfi_ref, sbr_ref, sbi_ref, *, nc, bsz):
    width = dfr_ref.shape[1]

    def run(dr_ref, di_ref, ar_ref, ai_ref, or_ref, oi_ref, chunk_of_step):
        ar = ar_ref[...]
        ai = ai_ref[...]

        def body(i, s):
            sr, si = s
            rows = pl.ds(chunk_of_step(i), bsz, stride=nc)
            or_ref[rows, :] = sr
            oi_ref[rows, :] = si
            return (ar * sr - ai * si + dr_ref[rows, :], ar * si + ai * sr + di_ref[rows, :])

        zero = jnp.zeros((bsz, width), F32)
        lax.fori_loop(0, nc, body, (zero, zero))

    run(dfr_ref, dfi_ref, afr_ref, afi_ref, sfr_ref, sfi_ref, lambda i: i)
    run(dbr_ref, dbi_ref, abr_ref, abi_ref, sbr_ref, sbi_ref,
        lambda i: jnp.where(i < SSM_NCTX, SSM_NCTX - 1 - i, nc + SSM_NCTX - 1 - i))


def _ssm_scan(planes, decays, bsz):
    n, width = planes[0].shape
    blk = pl.BlockSpec((n, 128), lambda i: (0, i))
    arow = pl.BlockSpec((1, 128), lambda i: (0, i))
    shape = jax.ShapeDtypeStruct((n, width), F32)
    return pl.pallas_call(
        functools.partial(_ssm_scan_kernel, nc=n // bsz, bsz=bsz),
        grid=(width // 128,),
        in_specs=[blk] * 4 + [arow] * 4,
        out_specs=[blk] * 4,
        out_shape=[shape] * 4,
        compiler_params=_cparams(("arbitrary",)),
    )(*planes, *decays)


def _ssm_phase2_kernel(y_ref, sfr_ref, sfi_ref, sbr_ref, sbi_ref, r2_ref, o_ref):
    p = SSM_STATE
    for k in range(2):
        acc = y_ref[k]
        for idx, ref in enumerate((sfr_ref, sfi_ref, sbr_ref, sbi_ref)):
            acc = acc + _dot(ref[:, k * p:(k + 1) * p].astype(BF16), r2_ref[k, idx * p:(idx + 1) * p, :])
        o_ref[k] = acc


def _ssm_phase2(yg, states, r2):
    g, n, th = yg.shape
    p = SSM_STATE
    plane = pl.BlockSpec((n, 2 * p), lambda i: (0, i))
    return pl.pallas_call(
        _ssm_phase2_kernel,
        grid=(g // 2,),
        in_specs=[pl.BlockSpec((2, n, th), lambda i: (i, 0, 0)), plane, plane, plane, plane,
                  pl.BlockSpec((2, 4 * p, th), lambda i: (i, 0, 0))],
        out_specs=pl.BlockSpec((2, n, th), lambda i: (i, 0, 0)),
        out_shape=jax.ShapeDtypeStruct((g, n, th), F32),
        compiler_params=_cparams(("arbitrary",)),
    )(yg, *states, r2)


def _ssm_mix(su_all, ops):
    r1, r2 = ops[0], ops[1]
    bsz = su_all.shape[0]
    ug = _to_groups(su_all)
    y1, *planes = _ssm_phase1(ug, r1)
    states = _ssm_scan(planes, ops[2:], bsz)
    return _from_groups(_ssm_phase2(y1, states, r2), bsz)


def _gla_direction(qk, v, z, wg, bg, s_ref, d, o_ref, reverse):
    tb = qk.shape[0]
    c_len = GLA_CHUNK
    x = _dot(z.astype(BF16), wg) + bg
    gate = (jnp.minimum(x, 0.0) - jnp.log(1.0 + jnp.exp(-jnp.abs(x)))) * (1.0 / GLA_TAU)
    ti = lax.broadcasted_iota(jnp.int32, (tb, tb), 0)
    si = lax.broadcasted_iota(jnp.int32, (tb, tb), 1)
    order = (si >= ti) if reverse else (si <= ti)
    tri = jnp.where(jnp.logical_and(order, ti // c_len == si // c_len), 1.0, 0.0).astype(F32)
    bcum_all = _dot(tri, gate, HIGHEST)
    q_all = (qk[:, 0:GLA_QK_W] * (GLA_DK ** -0.5) * jnp.exp(bcum_all)).astype(BF16)
    k_all = qk[:, GLA_QK_W:2 * GLA_QK_W]
    kin_all = k_all * jnp.exp(-bcum_all)
    v_bf = v.astype(BF16)
    nh = GLA_HEADS
    head_of = lambda shape, axis, size: lax.broadcasted_iota(jnp.int32, shape, axis) // size
    kmask = head_of((nh * c_len, GLA_QK_W), 0, c_len) == head_of((nh * c_len, GLA_QK_W), 1, GLA_DK)
    vmask = head_of((nh * c_len, GLA_V_W), 0, c_len) == head_of((nh * c_len, GLA_V_W), 1, GLA_DV)
    smask = head_of((GLA_V_W, GLA_QK_W), 0, GLA_DV) == head_of((GLA_V_W, GLA_QK_W), 1, GLA_DK)
    tq = lax.broadcasted_iota(jnp.int32, (c_len, nh * c_len), 0)
    sk = lax.broadcasted_iota(jnp.int32, (c_len, nh * c_len), 1) % c_len
    keep = (sk >= tq) if reverse else (sk <= tq)
    chunks = range(tb // c_len)
    for c in (reversed(chunks) if reverse else chunks):
        rs = slice(c * c_len, (c + 1) * c_len)
        bcum = bcum_all[rs]
        blast = bcum[0:1] if reverse else bcum[c_len - 1:c_len]
        k_up = (k_all[rs] * jnp.exp(blast - bcum)).astype(BF16)
        q_in = q_all[rs]
        k_exp = jnp.where(kmask, jnp.concatenate([kin_all[rs]] * nh, axis=0), 0.0).astype(BF16)
        v_exp = jnp.where(vmask, jnp.concatenate([v[rs]] * nh, axis=0), 0.0).astype(BF16)
        att = jnp.where(keep, _dot_nt(q_in, k_exp), 0.0)
        st = s_ref[d]
        o_ref[rs, :] = _dot(att.astype(BF16), v_exp) + _dot_nt(q_in, st.astype(BF16))
        s_ref[d] = st * jnp.exp(blast) + jnp.where(smask, _dot_tn(v_bf[rs], k_up), 0.0)


def _gla_kernel(qkf_ref, vf_ref, zf_ref, qkb_ref, vb_ref, zb_ref, wg_ref, bg_ref, of_ref, ob_ref, s_ref):
    @pl.when(pl.program_id(1) == 0)
    def _():
        s_ref[...] = jnp.zeros_like(s_ref)

    for bi in range(qkf_ref.shape[0]):
        _gla_direction(qkf_ref[bi], vf_ref[bi], zf_ref[bi], wg_ref[0], bg_ref[0], s_ref.at[bi], 0, of_ref.at[bi], False)
        _gla_direction(qkb_ref[bi], vb_ref[bi], zb_ref[bi], wg_ref[1], bg_ref[1], s_ref.at[bi], 1, ob_ref.at[bi], True)


def _gla(gqk, gv, gz, wg_pad, bg):
    bsz, nt, _ = gqk.shape
    tb = TOKEN_TILE
    nb = BATCH_TILE
    tiles = nt // tb
    fwd = lambda b, i: (b, i, 0)
    bwd = lambda b, i: (b, jnp.where(i == 0, 0, tiles - i), 0)
    spec = lambda width, im: pl.BlockSpec((nb, tb, width), im)
    return pl.pallas_call(
        _gla_kernel,
        grid=(bsz // nb, tiles),
        in_specs=[spec(2 * GLA_QK_W, fwd), spec(GLA_V_W, fwd), spec(GZ_PAD, fwd),
                  spec(2 * GLA_QK_W, bwd), spec(GLA_V_W, bwd), spec(GZ_PAD, bwd),
                  pl.BlockSpec((2, GZ_PAD, GLA_QK_W), lambda b, i: (0, 0, 0)),
                  pl.BlockSpec((2, 1, GLA_QK_W), lambda b, i: (0, 0, 0))],
        out_specs=[spec(GLA_V_W, fwd), spec(GLA_V_W, bwd)],
        out_shape=[jax.ShapeDtypeStruct((bsz, nt, GLA_V_W), F32), jax.ShapeDtypeStruct((bsz, nt, GLA_V_W), F32)],
        scratch_shapes=[pltpu.VMEM((nb, 2, GLA_V_W, GLA_QK_W), F32)],
        compiler_params=_cparams(("arbitrary", "arbitrary")),
    )(gqk, gv, gz, gqk, gv, gz, wg_pad, bg)


def _outproj_kernel(ctx_ref, x_ref, m_ref, att_ref, y_ref, u_ref, of_ref, ob_ref, r_ref,
                    ssmd_ref, wglu_ref, bglu_ref, gng_ref, wo_ref, ln1g_ref, ln1b_ref, router_ref,
                    x1_ref, h2_ref, aff_ref, *, t0):
    nb, tm, d = x_ref.shape
    x_in = jnp.where(pl.program_id(1) + t0 == 0, ctx_ref[...], x_ref[...])
    rows = nb * tm
    flat = lambda ref: ref[...].reshape(rows, ref.shape[-1])
    m = m_ref[0]
    zin = flat(y_ref) + ssmd_ref[...] * flat(u_ref)
    z = 0.5 * zin * (1.0 + jnp.tanh(math.sqrt(2.0 / math.pi) * (zin + 0.044715 * (zin * zin * zin))))
    ssm = z * _sigmoid(_dot(z.astype(BF16), wglu_ref[...]) + bglu_ref[...])
    o = flat(of_ref) + flat(ob_ref)
    r = flat(r_ref)
    gng = gng_ref[...]
    mix = [flat(att_ref), ssm.astype(BF16)]
    for hd in range(GLA_HEADS):
        vs = slice(hd * GLA_DV, (hd + 1) * GLA_DV)
        oh = o[:, vs]
        rh = r[:, vs]
        oh = oh * lax.rsqrt(jnp.mean(oh * oh, axis=-1, keepdims=True) + LN_EPS) * gng
        mix.append((oh * (rh * _sigmoid(rh))).astype(BF16))
    proj = _dot(jnp.concatenate(mix, axis=1), wo_ref[...]).reshape(nb, tm, d)
    x1 = _ln(ALPHA * x_in + m[:, 2:3] * proj) * ln1g_ref[...] + ln1b_ref[...]
    x1_ref[...] = x1
    h2 = (_ln(x1) * (1.0 + m[:, 4:5]) + m[:, 3:4]).reshape(rows, d)
    h_hi = h2.astype(BF16)
    h2_ref[...] = h_hi.reshape(nb, tm, d)
    h_lo = (h2 - h_hi.astype(F32)).astype(BF16)
    l_hi = _dot(h_hi, router_ref[...])
    logits = l_hi + pltpu.roll(l_hi, 128 - N_EXPERTS, 1) + _dot(h_lo, router_ref[...])
    col = lax.broadcasted_iota(jnp.int32, logits.shape, 1)
    logits = jnp.where(col < N_EXPERTS, logits, NEG_INF)
    e = jnp.exp(logits - jnp.max(logits, axis=-1, keepdims=True))
    aff_ref[...] = (e / jnp.sum(e, axis=-1, keepdims=True)).reshape(nb, tm, 128)


def _outproj(ctx_src, x_src, x_off, mods, att, y, su, o_f, o_b, gr, ssm_d, w_glu, b_glu, gn_g, w_out, ln1_g, ln1_b,
             router_split, with_ctx):
    bsz, nt, _ = y.shape
    d = x_src.shape[-1]
    tm = TOKEN_TILE
    nb = BATCH_TILE
    t0 = 0 if with_ctx else CTX_LEN // tm
    tiles = nt // tm - t0
    att_t0 = 0 if with_ctx else -t0
    tok_in = lambda width: pl.BlockSpec((nb, tm, width), lambda b, i: (b, i + t0, 0))
    tok_out = lambda width: pl.BlockSpec((nb, tm, width), lambda b, i: (b, i, 0))
    const = lambda shape: pl.BlockSpec(shape, lambda b, i: tuple(0 for _ in shape))
    once = lambda shape: pl.BlockSpec(shape, lambda b, i: tuple(0 for _ in shape), pipeline_mode=pl.Buffered(1))
    shp = lambda width, dt: jax.ShapeDtypeStruct((bsz, tiles * tm, width), dt)
    return pl.pallas_call(
        functools.partial(_outproj_kernel, t0=t0),
        grid=(bsz // nb, tiles),
        in_specs=[
            pl.BlockSpec((nb, tm, d), lambda b, i: (b, 0, 0)),
            pl.BlockSpec((nb, tm, d), lambda b, i: (b, jnp.maximum(i + t0 - x_off, 0), 0)),
            pl.BlockSpec((1, nb, N_MOD, d), lambda b, i: (jnp.minimum(i + t0, 1), b, 0, 0)),
            pl.BlockSpec((nb, tm, ATT_Q_W), lambda b, i: (b, i + t0 + att_t0, 0)),
            tok_in(SSM_WIDTH), tok_in(SSM_WIDTH), tok_in(GLA_V_W), tok_in(GLA_V_W), tok_in(GLA_V_W),
            const((1, SSM_WIDTH)), const((SSM_WIDTH, SSM_WIDTH)), const((1, SSM_WIDTH)), const((1, GLA_DV)),
            once((d, d)), const((1, d)), const((1, d)), const((d, 128)),
        ],
        out_specs=[tok_out(d), tok_out(d), tok_out(128)],
        out_shape=[shp(d, F32), shp(d, BF16), shp(128, F32)],
        compiler_params=_cparams(("arbitrary", "arbitrary")),
    )(ctx_src, x_src, mods, att, y, su, o_f, o_b, gr, ssm_d, w_glu, b_glu, gn_g, w_out, ln1_g, ln1_b, router_split)


def _expert_kernel(x_ref, g_ref, wg_ref, wu_ref, wd_ref, o_ref, hid_ref, *, nf, tf):
    s = pl.program_id(1)

    @pl.when(s < nf)
    def _():
        x = x_ref[0]
        a = _dot(x, wg_ref[0, 0].astype(BF16))
        u = _dot(x, wu_ref[0, 0].astype(BF16))
        col = pl.multiple_of(s * tf, tf)
        hid_ref[:, pl.ds(col, tf)] = ((a * _sigmoid(a)) * u).astype(BF16)

    @pl.when(s >= nf)
    def _():
        o_ref[0] = (_dot(hid_ref[...], wd_ref[0, 0].astype(BF16)) * g_ref[0]).astype(BF16)


def _experts(xe, ge, w_gate, w_up, w_down, layer, e0):
    e, r, d = xe.shape
    ff = w_gate.shape[3]
    tf = 256
    nf = ff // tf
    nd = d // tf
    up = lambda i, s: (layer, e0 + i, 0, jnp.minimum(s, nf - 1))
    down = lambda i, s: (layer, e0 + i, 0, jnp.maximum(s - nf, 0))
    return pl.pallas_call(
        functools.partial(_expert_kernel, nf=nf, tf=tf),
        grid=(e, nf + nd),
        in_specs=[pl.BlockSpec((1, r, d), lambda i, s: (i, 0, 0)),
                  pl.BlockSpec((1, r, 1), lambda i, s: (i, 0, 0)),
                  pl.BlockSpec((1, 1, d, tf), up),
                  pl.BlockSpec((1, 1, d, tf), up),
                  pl.BlockSpec((1, 1, ff, tf), down)],
        out_specs=pl.BlockSpec((1, r, tf), lambda i, s: (i, 0, jnp.maximum(s - nf, 0))),
        out_shape=jax.ShapeDtypeStruct((e, r, d), BF16),
        scratch_shapes=[pltpu.VMEM((r, ff), BF16)],
        compiler_params=_cparams(("arbitrary", "arbitrary")),
    )(xe, ge, w_gate, w_up, w_down)


COMBINE_WIN = 64
COMBINE_ALIGN = 16


def _combine_kernel(lo_ref, hi_ref, x1_ref, m_ref, g_ref, b_ref, tok_hbm, *rest, tiles, nsteps, group_rows):
    ye_groups, (o_ref, buf, tokbuf, sem) = rest[:-4], rest[-4:]
    _combine_body(lo_ref, hi_ref, x1_ref, m_ref, g_ref, b_ref, tok_hbm, ye_groups, o_ref, buf, tokbuf, sem,
                  tiles, nsteps, group_rows)


def _combine_body(lo_ref, hi_ref, x1_ref, m_ref, g_ref, b_ref, tok_hbm, ye_groups, o_ref, buf, tokbuf, sem,
                  tiles, nsteps, group_rows):
    n_exp = N_EXPERTS
    win = COMBINE_WIN
    tm = x1_ref.shape[1]
    step = pl.program_id(0) * tiles + pl.program_id(1)
    slot = step % 2
    extra = 2

    per_group = n_exp // len(ye_groups)

    def first_row(st, e, k):
        u = (lo_ref[st * n_exp + e] // COMBINE_ALIGN) * COMBINE_ALIGN + k * win
        group_end = (e // per_group + 1) * group_rows
        return u, pl.multiple_of(jnp.minimum(u, group_end - win), COMBINE_ALIGN)

    def copies(st, k, dst):
        out = []
        for e in range(n_exp):
            s0 = first_row(st, e, k)[1]
            local = pl.ds(pl.multiple_of(s0 - (e // per_group) * group_rows, COMBINE_ALIGN), win)
            out.append(pltpu.make_async_copy(ye_groups[e // per_group].at[local, :], buf.at[dst, e], sem.at[dst]))
            out.append(pltpu.make_async_copy(tok_hbm.at[pl.ds(s0, win), :], tokbuf.at[dst, e], sem.at[dst]))
        return out

    @pl.when(step == 0)
    def _():
        for cp in copies(step, 0, slot):
            cp.start()

    @pl.when(step + 1 < nsteps)
    def _():
        for cp in copies(step + 1, 0, 1 - slot):
            cp.start()

    tile_tok = pl.program_id(1) * tm + lax.broadcasted_iota(jnp.int32, (win, tm), 1)
    jrow = lax.broadcasted_iota(jnp.int32, (win, 1), 0)

    def spread(k, src):
        parts = []
        for e in range(n_exp):
            u, s0 = first_row(step, e, k)
            row = s0 + jrow
            mine = jnp.logical_and(row >= jnp.maximum(u, lo_ref[step * n_exp + e]), row < hi_ref[step * n_exp + e])
            hit = jnp.logical_and(tokbuf[src, e][:, 0:1] == tile_tok, mine)
            parts.append(jnp.where(hit, 1.0, 0.0).astype(BF16))
        onehot_t = jnp.concatenate(parts, axis=0)
        return _dot_tn(onehot_t, buf[src].reshape(n_exp * win, buf.shape[-1]))

    for cp in copies(step, 0, slot):
        cp.wait()
    acc = spread(0, slot)

    rounds = jnp.int32(1)
    for e in range(n_exp):
        span = hi_ref[step * n_exp + e] - (lo_ref[step * n_exp + e] // COMBINE_ALIGN) * COMBINE_ALIGN
        rounds = jnp.maximum(rounds, (span + win - 1) // win)

    def more(k, acc):
        for cp in copies(step, k, extra):
            cp.start()
        for cp in copies(step, k, extra):
            cp.wait()
        return acc + spread(k, extra)

    acc = lax.fori_loop(1, rounds, more, acc)
    m = m_ref[0, 0]
    o_ref[0] = _ln(ALPHA * x1_ref[0] + m[5:6] * acc) * g_ref[...] + b_ref[...]


def _combine(lo, hi, x1, mods, ln2_g, ln2_b, tok_of_row, ye_groups, with_ctx):
    bsz, nt, d = x1.shape
    tm = TOKEN_TILE
    tiles = nt // tm
    kind0 = 0 if with_ctx else 1
    tok = lambda width: pl.BlockSpec((1, tm, width), lambda b, i, lo, hi: (b, i, 0))
    vec = pl.BlockSpec((1, d), lambda b, i, lo, hi: (0, 0))
    hbm = pl.BlockSpec(memory_space=pl.ANY)
    return pl.pallas_call(
        functools.partial(_combine_kernel, tiles=tiles, nsteps=bsz * tiles, group_rows=ye_groups[0].shape[0]),
        grid_spec=pltpu.PrefetchScalarGridSpec(
            num_scalar_prefetch=2,
            grid=(bsz, tiles),
            in_specs=[tok(d),
                      pl.BlockSpec((1, 1, N_MOD, d), lambda b, i, lo, hi: (jnp.minimum(i + kind0, 1), b, 0, 0)),
                      vec, vec, hbm] + [hbm] * len(ye_groups),
            out_specs=tok(d),
            scratch_shapes=[pltpu.VMEM((3, N_EXPERTS, COMBINE_WIN, d), BF16),
                            pltpu.VMEM((3, N_EXPERTS, COMBINE_WIN, 128), jnp.int32),
                            pltpu.SemaphoreType.DMA((3,))],
        ),
        out_shape=jax.ShapeDtypeStruct((bsz, nt, d), F32),
        compiler_params=_cparams(("arbitrary", "arbitrary")),
    )(lo, hi, x1, mods, ln2_g, ln2_b, tok_of_row, *ye_groups)


def _route(aff, cap):
    gate, idx = lax.top_k(jnp.swapaxes(aff, 1, 2), cap)
    return gate, idx


def _rope_tables(seq):
    half = HEAD_DIM // 2
    nf = half // 2
    inv = ROPE_BASE ** (-jnp.arange(nf, dtype=F32) / nf)
    pos = jnp.arange(seq)
    ang_r = (pos // GRID_W).astype(F32)[:, None] * inv
    ang_c = (pos % GRID_W).astype(F32)[:, None] * inv
    zeros = jnp.zeros_like(ang_r)
    cos = jnp.concatenate([jnp.cos(ang_r), jnp.cos(ang_r), jnp.cos(ang_c), jnp.cos(ang_c)], axis=-1)
    sa = jnp.concatenate([-jnp.sin(ang_r), zeros, -jnp.sin(ang_c), zeros], axis=-1)
    sb = jnp.concatenate([zeros, jnp.sin(ang_r), zeros, jnp.sin(ang_c)], axis=-1)
    ctx = lambda fill: jnp.full((CTX_LEN, HEAD_DIM), fill, F32)
    return (jnp.concatenate([ctx(1.0), cos], axis=0), jnp.concatenate([ctx(0.0), sa], axis=0),
            jnp.concatenate([ctx(0.0), sb], axis=0))


def kernel(x, c, ctx, c_ctx, w_ada, b_ada, w_in, attn_sink, ssm_lam_re, ssm_lam_im, ssm_log_dt,
           ssm_b_re, ssm_b_im, ssm_c_re, ssm_c_im, ssm_d, ssm_w_glu, ssm_b_glu, gla_w_gate,
           gla_b_gate, gla_norm_g, w_out, ln1_g, ln1_b, ln2_g, ln2_b, router, exp_w_gate,
           exp_w_up, exp_w_down):
    bsz, seq, d = x.shape
    n_exp = router.shape[-1]
    cc = jnp.zeros((8, d), F32).at[:bsz].set(c).at[bsz].set(c_ctx)
    mod_all = _ada(cc, w_ada, b_ada)
    cos_t, sa_t, sb_t = _rope_tables(seq)
    ctx_src, x_src, x_off = ctx, x, CTX_LEN // TOKEN_TILE
    nt = CTX_LEN + seq
    bidx = jnp.arange(bsz)[None, :, None]

    for l in range(DEPTH):
        with_ctx = l < DEPTH - 1
        m = mod_all[l].reshape(8, N_MOD, d)
        mods = jnp.stack([jnp.broadcast_to(m[bsz], (bsz, N_MOD, d)), m[:bsz]], axis=0)
        w_in_p = jnp.pad(w_in[l], ((0, 0), (0, N_IN_PAD - N_IN))).astype(BF16)
        q, k, v, su, gqk, gv, gr, gz = _inproj(ctx_src, x_src, x_off, nt, mods, cos_t, sa_t, sb_t, w_in_p)

        att = _attention(attn_sink[l], q, k, v, with_ctx)

        ops = _ssm_ops(ssm_lam_re[l], ssm_lam_im[l], ssm_log_dt[l], ssm_b_re[l], ssm_b_im[l], ssm_c_re[l], ssm_c_im[l])
        y = _ssm_mix(su, ops)

        wg_pad = jnp.zeros((2, GZ_PAD, GLA_QK_W), F32)
        wg_pad = wg_pad.at[0, 0:GLA_RANK].set(gla_w_gate[l, 0]).at[1, GLA_RANK:2 * GLA_RANK].set(gla_w_gate[l, 1])
        o_f, o_b = _gla(gqk, gv, gz, wg_pad.astype(BF16), gla_b_gate[l][:, None, :])

        r_hi = router[l].astype(BF16)
        r_lo = (router[l] - r_hi.astype(F32)).astype(BF16)
        router_split = jnp.pad(jnp.concatenate([r_hi, r_lo], axis=1), ((0, 0), (0, 128 - 2 * n_exp)))
        x1, h2, aff = _outproj(ctx_src, x_src, x_off, mods, att, y, su, o_f, o_b, gr,
                               ssm_d[l][None], ssm_w_glu[l].astype(BF16), ssm_b_glu[l][None], gla_norm_g[l][None],
                               w_out[l].astype(BF16), ln1_g[l][None], ln1_b[l][None], router_split, with_ctx)

        t0 = CTX_LEN if with_ctx else 0
        cap = CAPACITY_FACTOR * seq // n_exp
        gate, idx = _route(aff[:, t0:, :n_exp], cap)
        idx = idx + t0
        if with_ctx:
            cap_c = CAPACITY_FACTOR * CTX_LEN // n_exp
            gate_c, idx_c = _route(aff[:, :CTX_LEN, :n_exp], cap_c)
            gate = jnp.concatenate([gate, gate_c], axis=-1)
            idx = jnp.concatenate([idx, idx_c], axis=-1)
        idx, gate = lax.sort((idx, gate), dimension=-1, num_keys=1)
        idx_e = jnp.swapaxes(idx, 0, 1)
        gate_e = jnp.swapaxes(gate, 0, 1)
        slots = idx_e.shape[-1]
        rows_e = bsz * slots
        ye_groups = []
        eg = n_exp // EXPERT_GROUPS
        for grp in range(EXPERT_GROUPS):
            sl = slice(grp * eg, (grp + 1) * eg)
            xe = h2[bidx, idx_e[sl]].reshape(eg, rows_e, d)
            ye = _experts(xe, gate_e[sl].reshape(eg, rows_e, 1), exp_w_gate, exp_w_up, exp_w_down, l, grp * eg)
            ye_groups.append(ye.reshape(eg * rows_e, d))
        nt_out = x1.shape[1]
        bounds = jnp.arange(nt_out // TOKEN_TILE + 1) * TOKEN_TILE
        cnt = jnp.sum(idx_e[..., None] < bounds, axis=2).astype(jnp.int32)
        base = (jnp.arange(n_exp)[:, None, None] * rows_e + jnp.arange(bsz)[None, :, None] * slots).astype(jnp.int32)
        lo = jnp.transpose(base + cnt[:, :, :-1], (1, 2, 0)).reshape(-1)
        hi = jnp.transpose(base + cnt[:, :, 1:], (1, 2, 0)).reshape(-1)
        tok_of_row = jnp.broadcast_to(idx_e.reshape(n_exp * rows_e, 1).astype(jnp.int32), (n_exp * rows_e, 128))
        x_new = _combine(lo, hi, x1, mods, ln2_g[l][None], ln2_b[l][None], tok_of_row, ye_groups, with_ctx)
        ctx_src, x_src, x_off = x_new, x_new, 0
    return x_new
```

```python
import functools
import math

import jax
import jax.numpy as jnp
from jax import lax
from jax.experimental import pallas as pl
from jax.experimental.pallas import tpu as pltpu

F32 = jnp.float32
BF16 = jnp.bfloat16
HIGHEST = lax.Precision.HIGHEST

D_MODEL = 2048
DEPTH = 2
GRID_W = 64
CTX_LEN = 256
ATT_HEADS = 8
ATT_KV_HEADS = 2
HEAD_DIM = 128
ATT_BLOCK = 128
ATT_QBLOCKS = 2
ROPE_BASE = 10000.0
SSM_WIDTH = 512
SSM_GROUP = 16
SSM_GROUPS = 32
SSM_STATE = 64
GLA_HEADS = 4
GLA_DK = 64
GLA_DV = 128
GLA_RANK = 16
GLA_TAU = 16.0
GLA_CHUNK = 64
N_EXPERTS = 16
EXPERT_FF = 2048
CAPACITY_FACTOR = 2
N_MOD = 6
ALPHA = (2 * DEPTH) ** 0.25
LN_EPS = 1e-6
NEG_INF = -1e30

ATT_Q_W = ATT_HEADS * HEAD_DIM
ATT_KV_W = ATT_KV_HEADS * HEAD_DIM
GLA_QK_W = GLA_HEADS * GLA_DK
GLA_V_W = GLA_HEADS * GLA_DV
N_IN = 3616
N_IN_PAD = 3712
GZ_PAD = N_IN_PAD - 3584

TOKEN_TILE = 256
BATCH_TILE = 2
EXPERT_GROUPS = 4
SSM_T = 32
SSM_NCTX = CTX_LEN // SSM_T
VMEM_LIMIT = 56 * 1024 * 1024


def _cparams(sem, vmem=VMEM_LIMIT):
    return pltpu.CompilerParams(dimension_semantics=sem, vmem_limit_bytes=vmem)


def _dot(a, b, precision=None):
    return jnp.dot(a, b, preferred_element_type=F32, precision=precision)


def _dot_nt(a, b, precision=None):
    return lax.dot_general(a, b, (((1,), (1,)), ((), ())), preferred_element_type=F32, precision=precision)


def _dot_tn(a, b, precision=None):
    return lax.dot_general(a, b, (((0,), (0,)), ((), ())), preferred_element_type=F32, precision=precision)


def _sigmoid(x):
    return 1.0 / (1.0 + jnp.exp(-x))


def _ln(x):
    mu = jnp.mean(x, axis=-1, keepdims=True)
    xc = x - mu
    var = jnp.mean(xc * xc, axis=-1, keepdims=True)
    return xc * lax.rsqrt(var + LN_EPS)


def _ada_kernel(c_ref, w_ref, b_ref, o_ref):
    c = c_ref[...]
    s = c * _sigmoid(c)
    w = w_ref[0]
    w_hi = w.astype(BF16)
    w_lo = (w - w_hi.astype(F32)).astype(BF16)
    s_hi = s.astype(BF16)
    s_lo = (s - s_hi.astype(F32)).astype(BF16)
    both = _dot(jnp.concatenate([s_hi, s_lo], axis=0), w_hi)
    rows = s.shape[0]
    o_ref[0] = both[0:rows] + both[rows:2 * rows] + _dot(s_hi, w_lo) + b_ref[0]


def _ada(cc, w_ada, b_ada):
    depth, d, n = w_ada.shape
    tn = 1024
    return pl.pallas_call(
        _ada_kernel,
        grid=(depth, n // tn),
        in_specs=[
            pl.BlockSpec((8, d), lambda l, j: (0, 0)),
            pl.BlockSpec((1, d, tn), lambda l, j: (l, 0, j)),
            pl.BlockSpec((1, 1, tn), lambda l, j: (l, 0, j)),
        ],
        out_specs=pl.BlockSpec((1, 8, tn), lambda l, j: (l, 0, j)),
        out_shape=jax.ShapeDtypeStruct((depth, 8, n), F32),
        compiler_params=_cparams(("arbitrary", "arbitrary")),
    )(cc, w_ada, b_ada.reshape(depth, 1, n))


def _inproj_kernel(ctx_ref, x_ref, m_ref, cos_ref, sa_ref, sb_ref, w_ref,
                   q_ref, k_ref, v_ref, su_ref, gqk_ref, gv_ref, gr_ref, gz_ref):
    nb, tm, d = x_ref.shape
    rows = nb * tm
    m = m_ref[0]
    x = jnp.where(pl.program_id(1) == 0, ctx_ref[...], x_ref[...])
    h = _ln(x) * (1.0 + m[:, 1:2]) + m[:, 0:1]
    h = h.reshape(rows, d).astype(BF16)
    tile = lambda ref: jnp.concatenate([ref[...]] * nb, axis=0)
    cos = tile(cos_ref)
    sa = tile(sa_ref)
    sb = tile(sb_ref)

    def rope(t):
        return t * cos + pltpu.roll(t, 96, 1) * sa + pltpu.roll(t, 32, 1) * sb

    def put(ref, val, sl=slice(None)):
        ref[:, :, sl] = val.reshape(nb, tm, val.shape[-1])

    scale = HEAD_DIM ** -0.5
    q = _dot(h, w_ref[:, 0:ATT_Q_W])
    for hd in range(ATT_HEADS):
        sl = slice(hd * HEAD_DIM, (hd + 1) * HEAD_DIM)
        put(q_ref, (rope(q[:, sl]) * scale).astype(BF16), sl)
    kk = _dot(h, w_ref[:, 1024:1280])
    for hd in range(ATT_KV_HEADS):
        sl = slice(hd * HEAD_DIM, (hd + 1) * HEAD_DIM)
        put(k_ref, rope(kk[:, sl]).astype(BF16), sl)
    put(v_ref, _dot(h, w_ref[:, 1280:1536]).astype(BF16))
    put(su_ref, _dot(h, w_ref[:, 1536:2048]))
    put(gqk_ref, _dot(h, w_ref[:, 2048:2560]))
    put(gv_ref, _dot(h, w_ref[:, 2560:3072]))
    put(gr_ref, _dot(h, w_ref[:, 3072:3584]))
    put(gz_ref, _dot(h, w_ref[:, 3584:N_IN_PAD]))


def _inproj(ctx_src, x_src, x_off, nt, mods, cos_t, sa_t, sb_t, w_in_bf16):
    bsz, _, d = x_src.shape
    tm = TOKEN_TILE
    nb = BATCH_TILE
    tiles = nt // tm
    tok = lambda width: pl.BlockSpec((nb, tm, width), lambda b, i: (b, i, 0))
    tab = pl.BlockSpec((tm, HEAD_DIM), lambda b, i: (i, 0))
    shp = lambda width, dt: jax.ShapeDtypeStruct((bsz, nt, width), dt)
    return pl.pallas_call(
        _inproj_kernel,
        grid=(bsz // nb, tiles),
        in_specs=[
            pl.BlockSpec((nb, tm, d), lambda b, i: (b, 0, 0)),
            pl.BlockSpec((nb, tm, d), lambda b, i: (b, jnp.maximum(i - x_off, 0), 0)),
            pl.BlockSpec((1, nb, N_MOD, d), lambda b, i: (jnp.minimum(i, 1), b, 0, 0)),
            tab, tab, tab,
            pl.BlockSpec((d, N_IN_PAD), lambda b, i: (0, 0), pipeline_mode=pl.Buffered(1)),
        ],
        out_specs=[tok(ATT_Q_W), tok(ATT_KV_W), tok(ATT_KV_W), tok(SSM_WIDTH),
                   tok(2 * GLA_QK_W), tok(GLA_V_W), tok(GLA_V_W), tok(GZ_PAD)],
        out_shape=[shp(ATT_Q_W, BF16), shp(ATT_KV_W, BF16), shp(ATT_KV_W, BF16), shp(SSM_WIDTH, F32),
                   shp(2 * GLA_QK_W, F32), shp(GLA_V_W, F32), shp(GLA_V_W, F32), shp(GZ_PAD, F32)],
        compiler_params=_cparams(("arbitrary", "arbitrary")),
    )(ctx_src, x_src, mods, cos_t, sa_t, sb_t, w_in_bf16)


def _attn_kernel(sink_ref, q_ref, k_ref, v_ref, o_ref, *, blk0, nblk):
    for sub in range(ATT_QBLOCKS):
        _attn_block(sink_ref, q_ref, k_ref, v_ref, o_ref, sub, pl.program_id(1) * ATT_QBLOCKS + sub + blk0, nblk)


def _attn_block(sink_ref, q_ref, k_ref, v_ref, o_ref, sub, qb, nblk):
    w = ATT_BLOCK
    qrows = slice(sub * w, (sub + 1) * w)
    cb = CTX_LEN // w
    n = qb - cb
    span = 3 * w
    start = jnp.clip((n - 1) * w, 0, (nblk - cb) * w - span)
    band = pl.ds(pl.multiple_of(CTX_LEN + start, w), span)
    g = ATT_HEADS // ATT_KV_HEADS
    rows = g * w
    q_pos = n * w + lax.broadcasted_iota(jnp.int32, (rows, span), 0) % w
    rel = q_pos - (start + lax.broadcasted_iota(jnp.int32, (rows, span), 1))
    in_window = jnp.logical_and(jnp.logical_and(rel <= w, rel >= -w), n >= 0)
    rowg = lax.broadcasted_iota(jnp.int32, (rows, 1), 0) // w

    for kvh in range(ATT_KV_HEADS):
        sl = slice(kvh * HEAD_DIM, (kvh + 1) * HEAD_DIM)
        qs = jnp.concatenate(
            [q_ref[0, qrows, (kvh * g + a) * HEAD_DIM:(kvh * g + a + 1) * HEAD_DIM] for a in range(g)], axis=0)
        sink = jnp.zeros((rows, 1), F32)
        for a in range(g):
            sink = jnp.where(rowg == a, sink_ref[kvh * g + a], sink)
        s_b = jnp.where(in_window, _dot_nt(qs, k_ref[0, band, sl]), NEG_INF)
        s_x = _dot_nt(qs, k_ref[0, 0:CTX_LEN, sl])
        mx = jnp.maximum(jnp.maximum(jnp.max(s_b, axis=-1, keepdims=True), jnp.max(s_x, axis=-1, keepdims=True)), sink)
        p_b = jnp.exp(s_b - mx)
        p_x = jnp.exp(s_x - mx)
        den = jnp.sum(p_b, axis=-1, keepdims=True) + jnp.sum(p_x, axis=-1, keepdims=True) + jnp.exp(sink - mx)
        o = _dot(p_b.astype(BF16), v_ref[0, band, sl]) + _dot(p_x.astype(BF16), v_ref[0, 0:CTX_LEN, sl])
        o = o / den
        for a in range(g):
            hs = slice((kvh * g + a) * HEAD_DIM, (kvh * g + a + 1) * HEAD_DIM)
            o_ref[0, qrows, hs] = o[a * w:(a + 1) * w].astype(BF16)


def _attention(sink, q_all, k_all, v_all, with_ctx):
    bsz, nt, _ = q_all.shape
    w = ATT_BLOCK
    nblk = nt // w
    blk0 = 0 if with_ctx else CTX_LEN // w
    wq = w * ATT_QBLOCKS
    step0 = blk0 // ATT_QBLOCKS
    return pl.pallas_call(
        functools.partial(_attn_kernel, blk0=blk0, nblk=nblk),
        grid=(bsz, (nblk - blk0) // ATT_QBLOCKS),
        in_specs=[
            pl.BlockSpec(memory_space=pltpu.SMEM),
            pl.BlockSpec((1, wq, ATT_Q_W), lambda b, j: (b, j + step0, 0)),
            pl.BlockSpec((1, nt, ATT_KV_W), lambda b, j: (b, 0, 0)),
            pl.BlockSpec((1, nt, ATT_KV_W), lambda b, j: (b, 0, 0)),
        ],
        out_specs=pl.BlockSpec((1, wq, ATT_Q_W), lambda b, j: (b, j, 0)),
        out_shape=jax.ShapeDtypeStruct((bsz, (nblk - blk0) * w, ATT_Q_W), BF16),
        compiler_params=_cparams(("arbitrary", "arbitrary")),
    )(sink, q_all, k_all, v_all)


def _ssm_ops_kernel(lr_ref, li_ref, ldt_ref, lrc_ref, lic_ref, ldtc_ref, bre_ref, bim_ref, cret_ref, cimt_ref,
                    r1_ref, r2_ref, are_ref, aim_ref):
    t, h, p = SSM_T, SSM_GROUP, SSM_STATE
    th = t * h
    kpad = t + _ROWS8
    lane_lag = lax.broadcasted_iota(jnp.int32, (kpad, th), 1) // h
    krow = lax.broadcasted_iota(jnp.int32, (kpad, th), 0)
    pick = lambda lag_of_lane: jnp.where(krow == lag_of_lane, 1.0, 0.0).astype(F32)
    tile_h = jnp.where(lax.broadcasted_iota(jnp.int32, (h, th), 0)
                       == lax.broadcasted_iota(jnp.int32, (h, th), 1) % h, 1.0, 0.0).astype(F32)
    lane = lax.broadcasted_iota(jnp.int32, (h, th), 1)
    z, gs = [], []
    for d in range(2):
        lam_r, lam_i = lr_ref[d, 0], li_ref[d, 0]
        dt = jnp.exp(ldt_ref[d, 0])
        ldr, ldi = lam_r * dt, lam_i * dt
        mag = jnp.exp(ldr)
        lbr, lbi = mag * jnp.cos(ldi), mag * jnp.sin(ldi)
        den = lam_r * lam_r + lam_i * lam_i
        nr = lbr - 1.0
        cf_r = (nr * lam_r + lbi * lam_i) / den
        cf_i = (lbi * lam_r - nr * lam_i) / den
        br, bi = bre_ref[d, 0], bim_ref[d, 0]
        bbr = cf_r * br - cf_i * bi
        bbi = cf_r * bi + cf_i * br
        kcol = jnp.minimum(lax.broadcasted_iota(jnp.int32, (kpad, 1), 0), t).astype(F32)
        mk = jnp.exp(kcol * ldr)
        wr_all, wi_all = mk * jnp.cos(kcol * ldi), mk * jnp.sin(kcol * ldi)
        are_ref[d, 0] = wr_all[t:t + 1]
        aim_ref[d, 0] = wi_all[t:t + 1]
        gs.append([(bbr * wr_all[k:k + 1] - bbi * wi_all[k:k + 1], bbr * wi_all[k:k + 1] + bbi * wr_all[k:k + 1])
                   for k in range(t)])
        dtc = jnp.exp(ldtc_ref[d, 0])
        ldrc, ldic = lrc_ref[d, 0] * dtc, lic_ref[d, 0] * dtc
        krow_f = jnp.minimum(lax.broadcasted_iota(jnp.int32, (1, kpad), 1), t).astype(F32)
        mkc = jnp.exp(ldrc * krow_f)
        wrc, wic = mkc * jnp.cos(ldic * krow_f), mkc * jnp.sin(ldic * krow_f)
        ct_r = _dot(cret_ref[d, 0], tile_h, HIGHEST)
        ct_i = _dot(cimt_ref[d, 0], tile_h, HIGHEST)

        def c_lam(sel):
            er, ei = _dot(wrc, sel, HIGHEST), _dot(wic, sel, HIGHEST)
            return ct_r * er - ct_i * ei, ct_r * ei + ct_i * er

        lr_, li_ = c_lam(pick(lane_lag if d == 0 else t - 1 - lane_lag))
        z.append(_dot(bbr, lr_, HIGHEST) - _dot(bbi, li_, HIGHEST))
        er_, ei_ = c_lam(pick(lane_lag + 1 if d == 0 else t - lane_lag))
        r2_ref[0, (2 * d) * p:(2 * d + 1) * p, :] = er_.astype(BF16)
        r2_ref[0, (2 * d + 1) * p:(2 * d + 2) * p, :] = (-ei_).astype(BF16)
    zf, zb = z
    for s in range(t):
        fwd = zf if s == 0 else pltpu.roll(zf, s * h, 1)
        bwd = zb if s == t - 1 else pltpu.roll(zb, th - (t - 1 - s) * h, 1)
        taps = jnp.where(lane >= s * h, fwd, 0.0) + jnp.where(lane < (s + 1) * h, bwd, 0.0)
        inc = jnp.concatenate([gs[0][t - 1 - s][0], gs[0][t - 1 - s][1], gs[1][s][0], gs[1][s][1]], axis=1)
        r1_ref[0, s * h:(s + 1) * h, :] = jnp.concatenate([taps, inc], axis=1).astype(BF16)


def _ssm_ops(lam_re, lam_im, log_dt, b_re, b_im, c_re, c_im):
    g, p, h, t = SSM_GROUPS, SSM_STATE, SSM_GROUP, SSM_T
    row = lambda a: a.reshape(2, g, 1, p)
    col = lambda a: a.reshape(2, g, p, 1)
    ldt = jnp.broadcast_to(log_dt[:, :, None], (2, g, p))
    tr = lambda a: jnp.swapaxes(a, 2, 3)
    row_spec = pl.BlockSpec((2, 1, 1, p), lambda i: (0, i, 0, 0))
    col_spec = pl.BlockSpec((2, 1, p, 1), lambda i: (0, i, 0, 0))
    hp_spec = pl.BlockSpec((2, 1, h, p), lambda i: (0, i, 0, 0))
    ph_spec = pl.BlockSpec((2, 1, p, h), lambda i: (0, i, 0, 0))
    a_shape = jax.ShapeDtypeStruct((2, g, 1, p), F32)
    r1, r2, a_re, a_im = pl.pallas_call(
        _ssm_ops_kernel,
        grid=(g,),
        in_specs=[row_spec, row_spec, row_spec, col_spec, col_spec, col_spec, hp_spec, hp_spec, ph_spec, ph_spec],
        out_specs=[pl.BlockSpec((1, t * h, t * h + 4 * p), lambda i: (i, 0, 0)),
                   pl.BlockSpec((1, 4 * p, t * h), lambda i: (i, 0, 0)), row_spec, row_spec],
        out_shape=[jax.ShapeDtypeStruct((g, t * h, t * h + 4 * p), BF16),
                   jax.ShapeDtypeStruct((g, 4 * p, t * h), BF16), a_shape, a_shape],
        compiler_params=_cparams(("arbitrary",)),
    )(row(lam_re), row(lam_im), row(ldt), col(lam_re), col(lam_im), col(ldt), tr(b_re), tr(b_im), tr(c_re), tr(c_im))
    flat = lambda a: a.reshape(1, g * p)
    return r1, r2, flat(a_re[0]), flat(a_im[0]), flat(a_re[1]), flat(a_im[1])


_LANE_GROUPS = 128 // SSM_GROUP
_ROWS8 = 8


def _block_transpose(xs):
    h = SSM_GROUP
    blk = lax.broadcasted_iota(jnp.int32, xs[0].shape, 1) // h
    xs = list(xs)
    for d in (4, 2, 1):
        upper = (blk & d) != 0
        for i in range(_LANE_GROUPS):
            if i & d:
                continue
            a, b = xs[i], xs[i + d]
            xs[i] = jnp.where(upper, pltpu.roll(b, d * h, 1), a)
            xs[i + d] = jnp.where(upper, b, pltpu.roll(a, 128 - d * h, 1))
    return xs


def _sublane_transpose(xs):
    sub = lax.broadcasted_iota(jnp.int32, xs[0].shape, 0)
    xs = list(xs)
    for d in (4, 2, 1):
        upper = (sub & d) != 0
        for i in range(_ROWS8):
            if i & d:
                continue
            a, b = xs[i], xs[i + d]
            xs[i] = jnp.where(upper, pltpu.roll(b, d, 0), a)
            xs[i + d] = jnp.where(upper, b, pltpu.roll(a, _ROWS8 - d, 0))
    return xs


def _to_groups_kernel(su_ref, ug_ref, *, nc):
    t, h = SSM_T, SSM_GROUP

    def body(rb, carry):
        base = pl.multiple_of(rb * (_ROWS8 * t), _ROWS8 * t)
        rows = pl.ds(pl.multiple_of(rb * _ROWS8, _ROWS8), _ROWS8)
        for j in range(t * h // 128):
            toks = [su_ref[0, pl.ds(pl.multiple_of(base + c * t + _ROWS8 * j, _ROWS8), _ROWS8), :]
                    for c in range(_ROWS8)]
            for gm, tile in enumerate(_block_transpose(_sublane_transpose(toks))):
                ug_ref[gm, rows, j * 128:(j + 1) * 128] = tile
        return carry

    lax.fori_loop(0, nc // _ROWS8, body, 0)


def _to_groups(su):
    bsz, nt, w = su.shape
    nc = nt // SSM_T
    return pl.pallas_call(
        functools.partial(_to_groups_kernel, nc=nc),
        grid=(bsz, w // 128),
        in_specs=[pl.BlockSpec((1, nt, 128), lambda b, a: (b, 0, a))],
        out_specs=pl.BlockSpec((_LANE_GROUPS, nc, SSM_T * SSM_GROUP), lambda b, a: (a, b, 0)),
        out_shape=jax.ShapeDtypeStruct((SSM_GROUPS, bsz * nc, SSM_T * SSM_GROUP), F32),
        compiler_params=_cparams(("arbitrary", "arbitrary")),
    )(su)


def _from_groups_kernel(yg_ref, y_ref, *, nc):
    t, h = SSM_T, SSM_GROUP

    def body(rb, carry):
        base = pl.multiple_of(rb * (_ROWS8 * t), _ROWS8 * t)
        rows = pl.ds(pl.multiple_of(rb * _ROWS8, _ROWS8), _ROWS8)
        for j in range(t * h // 128):
            tiles = [yg_ref[gm, rows, j * 128:(j + 1) * 128] for gm in range(_LANE_GROUPS)]
            for c, tile in enumerate(_sublane_transpose(_block_transpose(tiles))):
                y_ref[0, pl.ds(pl.multiple_of(base + c * t + _ROWS8 * j, _ROWS8), _ROWS8), :] = tile
        return carry

    lax.fori_loop(0, nc // _ROWS8, body, 0)


def _from_groups(yg, bsz):
    g, n, th = yg.shape
    nc = n // bsz
    nt = nc * SSM_T
    return pl.pallas_call(
        functools.partial(_from_groups_kernel, nc=nc),
        grid=(bsz, g // _LANE_GROUPS),
        in_specs=[pl.BlockSpec((_LANE_GROUPS, nc, th), lambda b, a: (a, b, 0))],
        out_specs=pl.BlockSpec((1, nt, 128), lambda b, a: (b, 0, a)),
        out_shape=jax.ShapeDtypeStruct((bsz, nt, g * SSM_GROUP), F32),
        compiler_params=_cparams(("arbitrary", "arbitrary")),
    )(yg)


def _ssm_phase1_kernel(u_ref, r1_ref, y_ref, fr_ref, fi_ref, br_ref, bi_ref):
    th = SSM_T * SSM_GROUP
    p = SSM_STATE
    outs = [_dot(u_ref[k].astype(BF16), r1_ref[k]) for k in range(2)]
    for k in range(2):
        y_ref[k] = outs[k][:, 0:th]
    for idx, ref in enumerate((fr_ref, fi_ref, br_ref, bi_ref)):
        lo = th + idx * p
        ref[...] = jnp.concatenate([o[:, lo:lo + p] for o in outs], axis=1)


def _ssm_phase1(ug, r1):
    g, n, th = ug.shape
    p = SSM_STATE
    plane = pl.BlockSpec((n, 2 * p), lambda i: (0, i))
    plane_shape = jax.ShapeDtypeStruct((n, g * p), F32)
    return pl.pallas_call(
        _ssm_phase1_kernel,
        grid=(g // 2,),
        in_specs=[pl.BlockSpec((2, n, th), lambda i: (i, 0, 0)),
                  pl.BlockSpec((2, th, th + 4 * p), lambda i: (i, 0, 0))],
        out_specs=[pl.BlockSpec((2, n, th), lambda i: (i, 0, 0)), plane, plane, plane, plane],
        out_shape=[jax.ShapeDtypeStruct((g, n, th), F32), plane_shape, plane_shape, plane_shape, plane_shape],
        compiler_params=_cparams(("arbitrary",)),
    )(ug, r1)


def _ssm_scan_kernel(dfr_ref, dfi_ref, dbr_ref, dbi_ref, afr_ref, afi_ref, abr_ref, abi_ref,
                     sfr_ref, sfi_ref, sbr_ref, sbi_ref, *, nc, bsz):
    width = dfr_ref.shape[1]

    def run(dr_ref, di_ref, ar_ref, ai_ref, or_ref, oi_ref, chunk_of_step):
        ar = ar_ref[...]
        ai = ai_ref[...]

        def body(i, s):
            sr, si = s
            rows = pl.ds(chunk_of_step(i), bsz, stride=nc)
            or_ref[rows, :] = sr
            oi_ref[rows, :] = si
            return (ar * sr - ai * si + dr_ref[rows, :], ar * si + ai * sr + di_ref[rows, :])

        zero = jnp.zeros((bsz, width), F32)
        lax.fori_loop(0, nc, body, (zero, zero))

    run(dfr_ref, dfi_ref, afr_ref, afi_ref, sfr_ref, sfi_ref, lambda i: i)
    run(dbr_ref, dbi_ref, abr_ref, abi_ref, sbr_ref, sbi_ref,
        lambda i: jnp.where(i < SSM_NCTX, SSM_NCTX - 1 - i, nc + SSM_NCTX - 1 - i))


def _ssm_scan(planes, decays, bsz):
    n, width = planes[0].shape
    blk = pl.BlockSpec((n, 128), lambda i: (0, i))
    arow = pl.BlockSpec((1, 128), lambda i: (0, i))
    shape = jax.ShapeDtypeStruct((n, width), F32)
    return pl.pallas_call(
        functools.partial(_ssm_scan_kernel, nc=n // bsz, bsz=bsz),
        grid=(width // 128,),
        in_specs=[blk] * 4 + [arow] * 4,
        out_specs=[blk] * 4,
        out_shape=[shape] * 4,
        compiler_params=_cparams(("arbitrary",)),
    )(*planes, *decays)


def _ssm_phase2_kernel(y_ref, sfr_ref, sfi_ref, sbr_ref, sbi_ref, r2_ref, o_ref):
    p = SSM_STATE
    for k in range(2):
        acc = y_ref[k]
        for idx, ref in enumerate((sfr_ref, sfi_ref, sbr_ref, sbi_ref)):
            acc = acc + _dot(ref[:, k * p:(k + 1) * p].astype(BF16), r2_ref[k, idx * p:(idx + 1) * p, :])
        o_ref[k] = acc


def _ssm_phase2(yg, states, r2):
    g, n, th = yg.shape
    p = SSM_STATE
    plane = pl.BlockSpec((n, 2 * p), lambda i: (0, i))
    return pl.pallas_call(
        _ssm_phase2_kernel,
        grid=(g // 2,),
        in_specs=[pl.BlockSpec((2, n, th), lambda i: (i, 0, 0)), plane, plane, plane, plane,
                  pl.BlockSpec((2, 4 * p, th), lambda i: (i, 0, 0))],
        out_specs=pl.BlockSpec((2, n, th), lambda i: (i, 0, 0)),
        out_shape=jax.ShapeDtypeStruct((g, n, th), F32),
        compiler_params=_cparams(("arbitrary",)),
    )(yg, *states, r2)


def _ssm_mix(su_all, ops):
    r1, r2 = ops[0], ops[1]
    bsz = su_all.shape[0]
    ug = _to_groups(su_all)
    y1, *planes = _ssm_phase1(ug, r1)
    states = _ssm_scan(planes, ops[2:], bsz)
    return _from_groups(_ssm_phase2(y1, states, r2), bsz)


def _gla_direction(qk, v, z, wg, bg, s_ref, d, o_ref, reverse):
    tb = qk.shape[0]
    c_len = GLA_CHUNK
    x = _dot(z.astype(BF16), wg) + bg
    gate = (jnp.minimum(x, 0.0) - jnp.log(1.0 + jnp.exp(-jnp.abs(x)))) * (1.0 / GLA_TAU)
    ti = lax.broadcasted_iota(jnp.int32, (tb, tb), 0)
    si = lax.broadcasted_iota(jnp.int32, (tb, tb), 1)
    order = (si >= ti) if reverse else (si <= ti)
    tri = jnp.where(jnp.logical_and(order, ti // c_len == si // c_len), 1.0, 0.0).astype(F32)
    bcum_all = _dot(tri, gate, HIGHEST)
    q_all = (qk[:, 0:GLA_QK_W] * (GLA_DK ** -0.5) * jnp.exp(bcum_all)).astype(BF16)
    k_all = qk[:, GLA_QK_W:2 * GLA_QK_W]
    kin_all = k_all * jnp.exp(-bcum_all)
    v_bf = v.astype(BF16)
    nh = GLA_HEADS
    head_of = lambda shape, axis, size: lax.broadcasted_iota(jnp.int32, shape, axis) // size
    kmask = head_of((nh * c_len, GLA_QK_W), 0, c_len) == head_of((nh * c_len, GLA_QK_W), 1, GLA_DK)
    vmask = head_of((nh * c_len, GLA_V_W), 0, c_len) == head_of((nh * c_len, GLA_V_W), 1, GLA_DV)
    smask = head_of((GLA_V_W, GLA_QK_W), 0, GLA_DV) == head_of((GLA_V_W, GLA_QK_W), 1, GLA_DK)
    tq = lax.broadcasted_iota(jnp.int32, (c_len, nh * c_len), 0)
    sk = lax.broadcasted_iota(jnp.int32, (c_len, nh * c_len), 1) % c_len
    keep = (sk >= tq) if reverse else (sk <= tq)
    chunks = range(tb // c_len)
    for c in (reversed(chunks) if reverse else chunks):
        rs = slice(c * c_len, (c + 1) * c_len)
        bcum = bcum_all[rs]
        blast = bcum[0:1] if reverse else bcum[c_len - 1:c_len]
        k_up = (k_all[rs] * jnp.exp(blast - bcum)).astype(BF16)
        q_in = q_all[rs]
        k_exp = jnp.where(kmask, jnp.concatenate([kin_all[rs]] * nh, axis=0), 0.0).astype(BF16)
        v_exp = jnp.where(vmask, jnp.concatenate([v[rs]] * nh, axis=0), 0.0).astype(BF16)
        att = jnp.where(keep, _dot_nt(q_in, k_exp), 0.0)
        st = s_ref[d]
        o_ref[rs, :] = _dot(att.astype(BF16), v_exp) + _dot_nt(q_in, st.astype(BF16))
        s_ref[d] = st * jnp.exp(blast) + jnp.where(smask, _dot_tn(v_bf[rs], k_up), 0.0)


def _gla_kernel(qkf_ref, vf_ref, zf_ref, qkb_ref, vb_ref, zb_ref, wg_ref, bg_ref, of_ref, ob_ref, s_ref):
    @pl.when(pl.program_id(1) == 0)
    def _():
        s_ref[...] = jnp.zeros_like(s_ref)

    for bi in range(qkf_ref.shape[0]):
        _gla_direction(qkf_ref[bi], vf_ref[bi], zf_ref[bi], wg_ref[0], bg_ref[0], s_ref.at[bi], 0, of_ref.at[bi], False)
        _gla_direction(qkb_ref[bi], vb_ref[bi], zb_ref[bi], wg_ref[1], bg_ref[1], s_ref.at[bi], 1, ob_ref.at[bi], True)


def _gla(gqk, gv, gz, wg_pad, bg):
    bsz, nt, _ = gqk.shape
    tb = TOKEN_TILE
    nb = BATCH_TILE
    tiles = nt // tb
    fwd = lambda b, i: (b, i, 0)
    bwd = lambda b, i: (b, jnp.where(i == 0, 0, tiles - i), 0)
    spec = lambda width, im: pl.BlockSpec((nb, tb, width), im)
    return pl.pallas_call(
        _gla_kernel,
        grid=(bsz // nb, tiles),
        in_specs=[spec(2 * GLA_QK_W, fwd), spec(GLA_V_W, fwd), spec(GZ_PAD, fwd),
                  spec(2 * GLA_QK_W, bwd), spec(GLA_V_W, bwd), spec(GZ_PAD, bwd),
                  pl.BlockSpec((2, GZ_PAD, GLA_QK_W), lambda b, i: (0, 0, 0)),
                  pl.BlockSpec((2, 1, GLA_QK_W), lambda b, i: (0, 0, 0))],
        out_specs=[spec(GLA_V_W, fwd), spec(GLA_V_W, bwd)],
        out_shape=[jax.ShapeDtypeStruct((bsz, nt, GLA_V_W), F32), jax.ShapeDtypeStruct((bsz, nt, GLA_V_W), F32)],
        scratch_shapes=[pltpu.VMEM((nb, 2, GLA_V_W, GLA_QK_W), F32)],
        compiler_params=_cparams(("arbitrary", "arbitrary")),
    )(gqk, gv, gz, gqk, gv, gz, wg_pad, bg)


def _outproj_kernel(ctx_ref, x_ref, m_ref, att_ref, y_ref, u_ref, of_ref, ob_ref, r_ref,
                    ssmd_ref, wglu_ref, bglu_ref, gng_ref, wo_ref, ln1g_ref, ln1b_ref, router_ref,
                    x1_ref, h2_ref, aff_ref, *, t0):
    nb, tm, d = x_ref.shape
    x_in = jnp.where(pl.program_id(1) + t0 == 0, ctx_ref[...], x_ref[...])
    rows = nb * tm
    flat = lambda ref: ref[...].reshape(rows, ref.shape[-1])
    m = m_ref[0]
    zin = flat(y_ref) + ssmd_ref[...] * flat(u_ref)
    z = 0.5 * zin * (1.0 + jnp.tanh(math.sqrt(2.0 / math.pi) * (zin + 0.044715 * (zin * zin * zin))))
    ssm = z * _sigmoid(_dot(z.astype(BF16), wglu_ref[...]) + bglu_ref[...])
    o = flat(of_ref) + flat(ob_ref)
    r = flat(r_ref)
    gng = gng_ref[...]
    mix = [flat(att_ref), ssm.astype(BF16)]
    for hd in range(GLA_HEADS):
        vs = slice(hd * GLA_DV, (hd + 1) * GLA_DV)
        oh = o[:, vs]
        rh = r[:, vs]
        oh = oh * lax.rsqrt(jnp.mean(oh * oh, axis=-1, keepdims=True) + LN_EPS) * gng
        mix.append((oh * (rh * _sigmoid(rh))).astype(BF16))
    proj = _dot(jnp.concatenate(mix, axis=1), wo_ref[...]).reshape(nb, tm, d)
    x1 = _ln(ALPHA * x_in + m[:, 2:3] * proj) * ln1g_ref[...] + ln1b_ref[...]
    x1_ref[...] = x1
    h2 = (_ln(x1) * (1.0 + m[:, 4:5]) + m[:, 3:4]).reshape(rows, d)
    h_hi = h2.astype(BF16)
    h2_ref[...] = h_hi.reshape(nb, tm, d)
    h_lo = (h2 - h_hi.astype(F32)).astype(BF16)
    l_hi = _dot(h_hi, router_ref[...])
    logits = l_hi + pltpu.roll(l_hi, 128 - N_EXPERTS, 1) + _dot(h_lo, router_ref[...])
    col = lax.broadcasted_iota(jnp.int32, logits.shape, 1)
    logits = jnp.where(col < N_EXPERTS, logits, NEG_INF)
    e = jnp.exp(logits - jnp.max(logits, axis=-1, keepdims=True))
    aff_ref[...] = (e / jnp.sum(e, axis=-1, keepdims=True)).reshape(nb, tm, 128)


def _outproj(ctx_src, x_src, x_off, mods, att, y, su, o_f, o_b, gr, ssm_d, w_glu, b_glu, gn_g, w_out, ln1_g, ln1_b,
             router_split, with_ctx):
    bsz, nt, _ = y.shape
    d = x_src.shape[-1]
    tm = TOKEN_TILE
    nb = BATCH_TILE
    t0 = 0 if with_ctx else CTX_LEN // tm
    tiles = nt // tm - t0
    att_t0 = 0 if with_ctx else -t0
    tok_in = lambda width: pl.BlockSpec((nb, tm, width), lambda b, i: (b, i + t0, 0))
    tok_out = lambda width: pl.BlockSpec((nb, tm, width), lambda b, i: (b, i, 0))
    const = lambda shape: pl.BlockSpec(shape, lambda b, i: tuple(0 for _ in shape))
    once = lambda shape: pl.BlockSpec(shape, lambda b, i: tuple(0 for _ in shape), pipeline_mode=pl.Buffered(1))
    shp = lambda width, dt: jax.ShapeDtypeStruct((bsz, tiles * tm, width), dt)
    return pl.pallas_call(
        functools.partial(_outproj_kernel, t0=t0),
        grid=(bsz // nb, tiles),
        in_specs=[
            pl.BlockSpec((nb, tm, d), lambda b, i: (b, 0, 0)),
            pl.BlockSpec((nb, tm, d), lambda b, i: (b, jnp.maximum(i + t0 - x_off, 0), 0)),
            pl.BlockSpec((1, nb, N_MOD, d), lambda b, i: (jnp.minimum(i + t0, 1), b, 0, 0)),
            pl.BlockSpec((nb, tm, ATT_Q_W), lambda b, i: (b, i + t0 + att_t0, 0)),
            tok_in(SSM_WIDTH), tok_in(SSM_WIDTH), tok_in(GLA_V_W), tok_in(GLA_V_W), tok_in(GLA_V_W),
            const((1, SSM_WIDTH)), const((SSM_WIDTH, SSM_WIDTH)), const((1, SSM_WIDTH)), const((1, GLA_DV)),
            once((d, d)), const((1, d)), const((1, d)), const((d, 128)),
        ],
        out_specs=[tok_out(d), tok_out(d), tok_out(128)],
        out_shape=[shp(d, F32), shp(d, BF16), shp(128, F32)],
        compiler_params=_cparams(("arbitrary", "arbitrary")),
    )(ctx_src, x_src, mods, att, y, su, o_f, o_b, gr, ssm_d, w_glu, b_glu, gn_g, w_out, ln1_g, ln1_b, router_split)


def _expert_kernel(x_ref, g_ref, wg_ref, wu_ref, wd_ref, o_ref, hid_ref, *, nf, tf):
    s = pl.program_id(1)

    @pl.when(s < nf)
    def _():
        x = x_ref[0]
        a = _dot(x, wg_ref[0, 0].astype(BF16))
        u = _dot(x, wu_ref[0, 0].astype(BF16))
        col = pl.multiple_of(s * tf, tf)
        hid_ref[:, pl.ds(col, tf)] = ((a * _sigmoid(a)) * u).astype(BF16)

    @pl.when(s >= nf)
    def _():
        o_ref[0] = (_dot(hid_ref[...], wd_ref[0, 0].astype(BF16)) * g_ref[0]).astype(BF16)


def _experts(xe, ge, w_gate, w_up, w_down, layer, e0):
    e, r, d = xe.shape
    ff = w_gate.shape[3]
    tf = 256
    td = 512
    nf = ff // tf
    nd = d // td
    up = lambda i, s: (layer, e0 + i, 0, jnp.minimum(s, nf - 1))
    down = lambda i, s: (layer, e0 + i, 0, jnp.maximum(s - nf, 0))
    return pl.pallas_call(
        functools.partial(_expert_kernel, nf=nf, tf=tf),
        grid=(e, nf + nd),
        in_specs=[pl.BlockSpec((1, r, d), lambda i, s: (i, 0, 0)),
                  pl.BlockSpec((1, r, 1), lambda i, s: (i, 0, 0)),
                  pl.BlockSpec((1, 1, d, tf), up),
                  pl.BlockSpec((1, 1, d, tf), up),
                  pl.BlockSpec((1, 1, ff, td), down)],
        out_specs=pl.BlockSpec((1, r, td), lambda i, s: (i, 0, jnp.maximum(s - nf, 0))),
        out_shape=jax.ShapeDtypeStruct((e, r, d), BF16),
        scratch_shapes=[pltpu.VMEM((r, ff), BF16)],
        compiler_params=_cparams(("arbitrary", "arbitrary")),
    )(xe, ge, w_gate, w_up, w_down)


COMBINE_WIN = 64
COMBINE_ALIGN = 16


def _combine_kernel(lo_ref, hi_ref, x1_ref, m_ref, g_ref, b_ref, tok_hbm, *rest, tiles, nsteps, group_rows):
    ye_groups, (o_ref, buf, tokbuf, sem) = rest[:-4], rest[-4:]
    _combine_body(lo_ref, hi_ref, x1_ref, m_ref, g_ref, b_ref, tok_hbm, ye_groups, o_ref, buf, tokbuf, sem,
                  tiles, nsteps, group_rows)


def _combine_body(lo_ref, hi_ref, x1_ref, m_ref, g_ref, b_ref, tok_hbm, ye_groups, o_ref, buf, tokbuf, sem,
                  tiles, nsteps, group_rows):
    n_exp = N_EXPERTS
    win = COMBINE_WIN
    tm = x1_ref.shape[1]
    step = pl.program_id(0) * tiles + pl.program_id(1)
    slot = step % 2
    extra = 2

    per_group = n_exp // len(ye_groups)

    def first_row(st, e, k):
        u = (lo_ref[st * n_exp + e] // COMBINE_ALIGN) * COMBINE_ALIGN + k * win
        group_end = (e // per_group + 1) * group_rows
        return u, pl.multiple_of(jnp.minimum(u, group_end - win), COMBINE_ALIGN)

    def copies(st, k, dst):
        out = []
        for e in range(n_exp):
            s0 = first_row(st, e, k)[1]
            local = pl.ds(pl.multiple_of(s0 - (e // per_group) * group_rows, COMBINE_ALIGN), win)
            out.append(pltpu.make_async_copy(ye_groups[e // per_group].at[local, :], buf.at[dst, e], sem.at[dst]))
            out.append(pltpu.make_async_copy(tok_hbm.at[pl.ds(s0, win), :], tokbuf.at[dst, e], sem.at[dst]))
        return out

    @pl.when(step == 0)
    def _():
        for cp in copies(step, 0, slot):
            cp.start()

    @pl.when(step + 1 < nsteps)
    def _():
        for cp in copies(step + 1, 0, 1 - slot):
            cp.start()

    tile_tok = pl.program_id(1) * tm + lax.broadcasted_iota(jnp.int32, (win, tm), 1)
    jrow = lax.broadcasted_iota(jnp.int32, (win, 1), 0)

    def spread(k, src):
        parts = []
        for e in range(n_exp):
            u, s0 = first_row(step, e, k)
            row = s0 + jrow
            mine = jnp.logical_and(row >= jnp.maximum(u, lo_ref[step * n_exp + e]), row < hi_ref[step * n_exp + e])
            hit = jnp.logical_and(tokbuf[src, e][:, 0:1] == tile_tok, mine)
            parts.append(jnp.where(hit, 1.0, 0.0).astype(BF16))
        onehot_t = jnp.concatenate(parts, axis=0)
        return _dot_tn(onehot_t, buf[src].reshape(n_exp * win, buf.shape[-1]))

    for cp in copies(step, 0, slot):
        cp.wait()
    acc = spread(0, slot)

    rounds = jnp.int32(1)
    for e in range(n_exp):
        span = hi_ref[step * n_exp + e] - (lo_ref[step * n_exp + e] // COMBINE_ALIGN) * COMBINE_ALIGN
        rounds = jnp.maximum(rounds, (span + win - 1) // win)

    def more(k, acc):
        for cp in copies(step, k, extra):
            cp.start()
        for cp in copies(step, k, extra):
            cp.wait()
        return acc + spread(k, extra)

    acc = lax.fori_loop(1, rounds, more, acc)
    m = m_ref[0, 0]
    o_ref[0] = _ln(ALPHA * x1_ref[0] + m[5:6] * acc) * g_ref[...] + b_ref[...]


def _combine(lo, hi, x1, mods, ln2_g, ln2_b, tok_of_row, ye_groups, with_ctx):
    bsz, nt, d = x1.shape
    tm = TOKEN_TILE
    tiles = nt // tm
    kind0 = 0 if with_ctx else 1
    tok = lambda width: pl.BlockSpec((1, tm, width), lambda b, i, lo, hi: (b, i, 0))
    vec = pl.BlockSpec((1, d), lambda b, i, lo, hi: (0, 0))
    hbm = pl.BlockSpec(memory_space=pl.ANY)
    return pl.pallas_call(
        functools.partial(_combine_kernel, tiles=tiles, nsteps=bsz * tiles, group_rows=ye_groups[0].shape[0]),
        grid_spec=pltpu.PrefetchScalarGridSpec(
            num_scalar_prefetch=2,
            grid=(bsz, tiles),
            in_specs=[tok(d),
                      pl.BlockSpec((1, 1, N_MOD, d), lambda b, i, lo, hi: (jnp.minimum(i + kind0, 1), b, 0, 0)),
                      vec, vec, hbm] + [hbm] * len(ye_groups),
            out_specs=tok(d),
            scratch_shapes=[pltpu.VMEM((3, N_EXPERTS, COMBINE_WIN, d), BF16),
                            pltpu.VMEM((3, N_EXPERTS, COMBINE_WIN, 128), jnp.int32),
                            pltpu.SemaphoreType.DMA((3,))],
        ),
        out_shape=jax.ShapeDtypeStruct((bsz, nt, d), F32),
        compiler_params=_cparams(("arbitrary", "arbitrary")),
    )(lo, hi, x1, mods, ln2_g, ln2_b, tok_of_row, *ye_groups)


def _route(aff, cap):
    gate, idx = lax.top_k(jnp.swapaxes(aff, 1, 2), cap)
    return gate, idx


def _rope_tables(seq):
    half = HEAD_DIM // 2
    nf = half // 2
    inv = ROPE_BASE ** (-jnp.arange(nf, dtype=F32) / nf)
    pos = jnp.arange(seq)
    ang_r = (pos // GRID_W).astype(F32)[:, None] * inv
    ang_c = (pos % GRID_W).astype(F32)[:, None] * inv
    zeros = jnp.zeros_like(ang_r)
    cos = jnp.concatenate([jnp.cos(ang_r), jnp.cos(ang_r), jnp.cos(ang_c), jnp.cos(ang_c)], axis=-1)
    sa = jnp.concatenate([-jnp.sin(ang_r), zeros, -jnp.sin(ang_c), zeros], axis=-1)
    sb = jnp.concatenate([zeros, jnp.sin(ang_r), zeros, jnp.sin(ang_c)], axis=-1)
    ctx = lambda fill: jnp.full((CTX_LEN, HEAD_DIM), fill, F32)
    return (jnp.concatenate([ctx(1.0), cos], axis=0), jnp.concatenate([ctx(0.0), sa], axis=0),
            jnp.concatenate([ctx(0.0), sb], axis=0))


def kernel(x, c, ctx, c_ctx, w_ada, b_ada, w_in, attn_sink, ssm_lam_re, ssm_lam_im, ssm_log_dt,
           ssm_b_re, ssm_b_im, ssm_c_re, ssm_c_im, ssm_d, ssm_w_glu, ssm_b_glu, gla_w_gate,
           gla_b_gate, gla_norm_g, w_out, ln1_g, ln1_b, ln2_g, ln2_b, router, exp_w_gate,
           exp_w_up, exp_w_down):
    bsz, seq, d = x.shape
    n_exp = router.shape[-1]
    cc = jnp.zeros((8, d), F32).at[:bsz].set(c).at[bsz].set(c_ctx)
    mod_all = _ada(cc, w_ada, b_ada)
    cos_t, sa_t, sb_t = _rope_tables(seq)
    ctx_src, x_src, x_off = ctx, x, CTX_LEN // TOKEN_TILE
    nt = CTX_LEN + seq
    bidx = jnp.arange(bsz)[None, :, None]

    for l in range(DEPTH):
        with_ctx = l < DEPTH - 1
        m = mod_all[l].reshape(8, N_MOD, d)
        mods = jnp.stack([jnp.broadcast_to(m[bsz], (bsz, N_MOD, d)), m[:bsz]], axis=0)
        w_in_p = jnp.pad(w_in[l], ((0, 0), (0, N_IN_PAD - N_IN))).astype(BF16)
        q, k, v, su, gqk, gv, gr, gz = _inproj(ctx_src, x_src, x_off, nt, mods, cos_t, sa_t, sb_t, w_in_p)

        att = _attention(attn_sink[l], q, k, v, with_ctx)

        ops = _ssm_ops(ssm_lam_re[l], ssm_lam_im[l], ssm_log_dt[l], ssm_b_re[l], ssm_b_im[l], ssm_c_re[l], ssm_c_im[l])
        y = _ssm_mix(su, ops)

        wg_pad = jnp.zeros((2, GZ_PAD, GLA_QK_W), F32)
        wg_pad = wg_pad.at[0, 0:GLA_RANK].set(gla_w_gate[l, 0]).at[1, GLA_RANK:2 * GLA_RANK].set(gla_w_gate[l, 1])
        o_f, o_b = _gla(gqk, gv, gz, wg_pad.astype(BF16), gla_b_gate[l][:, None, :])

        r_hi = router[l].astype(BF16)
        r_lo = (router[l] - r_hi.astype(F32)).astype(BF16)
        router_split = jnp.pad(jnp.concatenate([r_hi, r_lo], axis=1), ((0, 0), (0, 128 - 2 * n_exp)))
        x1, h2, aff = _outproj(ctx_src, x_src, x_off, mods, att, y, su, o_f, o_b, gr,
                               ssm_d[l][None], ssm_w_glu[l].astype(BF16), ssm_b_glu[l][None], gla_norm_g[l][None],
                               w_out[l].astype(BF16), ln1_g[l][None], ln1_b[l][None], router_split, with_ctx)

        t0 = CTX_LEN if with_ctx else 0
        cap = CAPACITY_FACTOR * seq // n_exp
        gate, idx = _route(aff[:, t0:, :n_exp], cap)
        idx = idx + t0
        if with_ctx:
            cap_c = CAPACITY_FACTOR * CTX_LEN // n_exp
            gate_c, idx_c = _route(aff[:, :CTX_LEN, :n_exp], cap_c)
            gate = jnp.concatenate([gate, gate_c], axis=-1)
            idx = jnp.concatenate([idx, idx_c], axis=-1)
        idx, gate = lax.sort((idx, gate), dimension=-1, num_keys=1)
        idx_e = jnp.swapaxes(idx, 0, 1)
        gate_e = jnp.swapaxes(gate, 0, 1)
        slots = idx_e.shape[-1]
        rows_e = bsz * slots
        ye_groups = []
        eg = n_exp // EXPERT_GROUPS
        for grp in range(EXPERT_GROUPS):
            sl = slice(grp * eg, (grp + 1) * eg)
            xe = h2[bidx, idx_e[sl]].reshape(eg, rows_e, d)
            ye = _experts(xe, gate_e[sl].reshape(eg, rows_e, 1), exp_w_gate, exp_w_up, exp_w_down, l, grp * eg)
            ye_groups.append(ye.reshape(eg * rows_e, d))
        nt_out = x1.shape[1]
        bounds = jnp.arange(nt_out // TOKEN_TILE + 1) * TOKEN_TILE
        cnt = jnp.sum(idx_e[..., None] < bounds, axis=2).astype(jnp.int32)
        base = (jnp.arange(n_exp)[:, None, None] * rows_e + jnp.arange(bsz)[None, :, None] * slots).astype(jnp.int32)
        lo = jnp.transpose(base + cnt[:, :, :-1], (1, 2, 0)).reshape(-1)
        hi = jnp.transpose(base + cnt[:, :, 1:], (1, 2, 0)).reshape(-1)
        tok_of_row = jnp.broadcast_to(idx_e.reshape(n_exp * rows_e, 1).astype(jnp.int32), (n_exp * rows_e, 128))
        x_new = _combine(lo, hi, x1, mods, ln2_g[l][None], ln2_b[l][None], tok_of_row, ye_groups, with_ctx)
        ctx_src, x_src, x_off = x_new, x_new, 0
    return x_new
```
